```python
import jax, jax.numpy as jnp
from jax import lax
import numpy as np

D_MODEL = 1024
BATCH = 4
SEQ = 4096
DEPTH = 2

N_A_LAYERS = DEPTH // 2
N_B_LAYERS = DEPTH - N_A_LAYERS
CONV_WIDTH = 3
N_HEADS = 16
HEAD_DIM = D_MODEL // N_HEADS
N_KV_HEADS = 4
GQA_GROUP = N_HEADS // N_KV_HEADS
CMP_BLOCK = 32
CMP_STRIDE = 16
CMP_HIDDEN = 4 * HEAD_DIM
SEL_BLOCK = 64
SEL_TOP = 16
WINDOW = 512
Q_BLOCK = 128
N_EXPERTS = 16
N_GROUPS = 4
EXPERTS_PER_GROUP = N_EXPERTS // N_GROUPS
TOP_K = 2
D_FF_EXPERT = D_MODEL // 2
ALPHA = (2 * DEPTH) ** 0.25
BETA = (8 * DEPTH) ** -0.25
LN_EPS = 1e-5
NEG = -1e30
FORCE = 1e9

kernel_name = "hybrid_shortconv_nsa_yoco_moe"


def layer_norm(x, g, b):
    xf = x.astype(jnp.float32)
    mu = jnp.mean(xf, -1, keepdims=True)
    var = jnp.mean(jnp.square(xf - mu), -1, keepdims=True)
    y = (xf - mu) * lax.rsqrt(var + LN_EPS)
    return (y * g.astype(jnp.float32) + b.astype(jnp.float32)).astype(x.dtype)


def masked_softmax(s, mask, axis):
    s = jnp.where(mask, s, NEG)
    m = jnp.max(s, axis=axis, keepdims=True)
    e = jnp.where(mask, jnp.exp(s - m), 0.0)
    return e / jnp.maximum(jnp.sum(e, axis=axis, keepdims=True), 1e-30)


def short_conv_mixer(u, w_in, conv_w, conv_b, w_out):
    bch = u @ w_in
    b_gate, c_gate, h = jnp.split(bch, 3, axis=-1)
    z = c_gate * h
    z = lax.conv_general_dilated(
        z, conv_w[:, None, :], window_strides=(1,),
        padding=[(CONV_WIDTH - 1, 0)],
        dimension_numbers=("NWC", "WIO", "NWC"),
        feature_group_count=D_MODEL) + conv_b
    return (b_gate * z) @ w_out


def nsa_shared_kv(s, w_kv, cmp_pe, cmp_w1, cmp_b1, cmp_w2, cmp_b2):
    B, S, _ = s.shape
    kv = (s @ w_kv).reshape(B, S, 6, N_KV_HEADS, HEAD_DIM)
    kv = jnp.transpose(kv, (2, 0, 3, 1, 4))
    k_c, v_c, k_s, v_s, k_w, v_w = kv[0], kv[1], kv[2], kv[3], kv[4], kv[5]
    n_cmp = (S - CMP_BLOCK) // CMP_STRIDE + 1
    idx = jnp.arange(n_cmp)[:, None] * CMP_STRIDE + jnp.arange(CMP_BLOCK)[None, :]

    def compress(t, i):
        blocks = t[:, :, idx] + cmp_pe[i]
        flat = blocks.reshape(B, N_KV_HEADS, n_cmp, CMP_BLOCK * HEAD_DIM)
        hdn = jax.nn.gelu(flat @ cmp_w1[i] + cmp_b1[i])
        return hdn @ cmp_w2[i] + cmp_b2[i]

    kc = compress(k_c, 0)
    vc = compress(v_c, 1)
    n_sel = S // SEL_BLOCK
    ks = k_s.reshape(B, N_KV_HEADS, n_sel, SEL_BLOCK, HEAD_DIM)
    vs = v_s.reshape(B, N_KV_HEADS, n_sel, SEL_BLOCK, HEAD_DIM)
    pad = ((0, 0), (0, 0), (WINDOW, 0), (0, 0))
    kw = jnp.pad(k_w, pad)
    vw = jnp.pad(v_w, pad)
    return kc, vc, ks, vs, kw, vw


def nsa_mixer(u, w_qg, b_g, w_o, kc, vc, ks, vs, kw, vw):
    B, S, _ = u.shape
    f32 = jnp.float32
    qg = u @ w_qg
    q = qg[..., :N_HEADS * HEAD_DIM].reshape(B, S, N_KV_HEADS, GQA_GROUP, HEAD_DIM)
    q = jnp.transpose(q, (0, 2, 3, 1, 4)) * (HEAD_DIM ** -0.5)
    g = jax.nn.sigmoid((qg[..., N_HEADS * HEAD_DIM:] + b_g).astype(f32))
    g = jnp.transpose(g.reshape(B, S, N_KV_HEADS, GQA_GROUP, 3), (0, 2, 3, 1, 4))

    n_cmp = kc.shape[2]
    n_sel = ks.shape[2]
    cmp_start = jnp.arange(n_cmp) * CMP_STRIDE
    cmp_end = cmp_start + CMP_BLOCK - 1
    sel_start = jnp.arange(n_sel) * SEL_BLOCK
    overlap = jnp.clip(
        jnp.minimum(cmp_start[:, None] + CMP_BLOCK, sel_start[None, :] + SEL_BLOCK)
        - jnp.maximum(cmp_start[:, None], sel_start[None, :]), 0, None).astype(f32) / CMP_BLOCK
    n_top = min(SEL_TOP, n_sel)
    bi = jnp.arange(B)[:, None, None, None]
    hi = jnp.arange(N_KV_HEADS)[None, :, None, None]
    jsel = jnp.arange(n_sel)

    def block(i):
        s0 = i * Q_BLOCK
        qb = lax.dynamic_slice_in_dim(q, s0, Q_BLOCK, axis=3)
        gb = lax.dynamic_slice_in_dim(g, s0, Q_BLOCK, axis=3)
        t = s0 + jnp.arange(Q_BLOCK)
        sc = jnp.einsum("bhgqd,bhnd->bhgqn", qb, kc).astype(f32)
        pc = masked_softmax(sc, cmp_end[None, :] <= t[:, None], -1)
        o_c = jnp.einsum("bhgqn,bhnd->bhgqd", pc.astype(vc.dtype), vc)
        imp = jnp.einsum("bhgqn,nj->bhqj", pc, overlap)
        cur = t // SEL_BLOCK
        allowed = jsel[None, :] <= cur[:, None]
        forced = (jsel[None, :] == 0) | (jsel[None, :] == cur[:, None]) | (jsel[None, :] == cur[:, None] - 1)
        score = jnp.where(forced & allowed, FORCE, jnp.where(allowed, imp, NEG))
        top_val, top_idx = lax.top_k(score, n_top)
        sel_ok = top_val > 0.5 * NEG
        kg = ks[bi, hi, top_idx]
        vg = vs[bi, hi, top_idx]
        kpos = top_idx[..., None] * SEL_BLOCK + jnp.arange(SEL_BLOCK)
        ms = sel_ok[..., None] & (kpos <= t[None, None, :, None, None])
        ssel = jnp.einsum("bhgqd,bhqnkd->bhgqnk", qb, kg).astype(f32)
        ps = masked_softmax(ssel, ms[:, :, None], (-2, -1))
        o_s = jnp.einsum("bhgqnk,bhqnkd->bhgqd", ps.astype(vg.dtype), vg)
        kwb = lax.dynamic_slice_in_dim(kw, s0, WINDOW + Q_BLOCK, axis=2)
        vwb = lax.dynamic_slice_in_dim(vw, s0, WINDOW + Q_BLOCK, axis=2)
        wpos = s0 - WINDOW + jnp.arange(WINDOW + Q_BLOCK)
        mw = (wpos[None, :] <= t[:, None]) & (wpos[None, :] > t[:, None] - WINDOW) & (wpos[None, :] >= 0)
        sw = jnp.einsum("bhgqd,bhkd->bhgqk", qb, kwb).astype(f32)
        pw = masked_softmax(sw, mw, -1)
        o_w = jnp.einsum("bhgqk,bhkd->bhgqd", pw.astype(vwb.dtype), vwb)
        o = gb[..., 0:1] * o_c + gb[..., 1:2] * o_s + gb[..., 2:3] * o_w
        return o.astype(u.dtype)

    o = lax.map(block, jnp.arange(S // Q_BLOCK))
    o = jnp.transpose(o, (1, 0, 4, 2, 3, 5)).reshape(B, S, N_HEADS * HEAD_DIM)
    return o @ w_o


def grouped_moe(u, router_w, router_bias, w_gu, w_down):
    B, S, D = u.shape
    f32 = jnp.float32
    tok = u.reshape(-1, D)
    aff = jax.nn.softmax((tok @ router_w).astype(f32), axis=-1)
    biased = aff + router_bias.astype(f32)
    grp_score = lax.top_k(biased.reshape(-1, N_GROUPS, EXPERTS_PER_GROUP), TOP_K)[0].sum(-1)
    best = jnp.argmax(grp_score, axis=-1)
    in_grp = (jnp.arange(N_EXPERTS) // EXPERTS_PER_GROUP)[None, :] == best[:, None]
    _, idx = lax.top_k(jnp.where(in_grp, biased, NEG), TOP_K)
    w = jnp.take_along_axis(aff, idx, axis=-1)
    w = w / jnp.sum(w, -1, keepdims=True)
    comb = jnp.sum(jax.nn.one_hot(idx, N_EXPERTS, dtype=f32) * w[..., None], axis=-2)
    out = jnp.zeros_like(tok)
    for e in range(N_EXPERTS):
        gate, up = jnp.split(tok @ w_gu[e], 2, axis=-1)
        out = out + comb[:, e:e + 1].astype(tok.dtype) * ((jax.nn.silu(gate) * up) @ w_down[e])
    return out.reshape(B, S, D)


def setup_inputs(seed: int = 0) -> dict:
    key = jax.random.key(seed)
    ks = jax.random.split(key, 24)
    f32 = jnp.float32
    D = D_MODEL
    nrm = lambda k, shape, s: jax.random.normal(k, shape, f32) * s
    gate_pattern = jnp.repeat(jnp.array([0.0, 0.0, 1.0, 0.0, 0.0, 1.0], f32), D)
    return {
        "x": nrm(ks[0], (BATCH, SEQ, D), 1.0),
        "c": nrm(ks[1], (BATCH, D), 1.0),
        "ada_w": nrm(ks[2], (DEPTH, D, 6 * D), 0.1 * D ** -0.5),
        "ada_b": gate_pattern[None, :] + nrm(ks[3], (DEPTH, 6 * D), 0.02),
        "ln_g": 1.0 + nrm(ks[4], (DEPTH, 2, D), 0.02),
        "ln_b": nrm(ks[5], (DEPTH, 2, D), 0.02),
        "conv_w_in": nrm(ks[6], (N_A_LAYERS, D, 3 * D), D ** -0.5),
        "conv_w": nrm(ks[7], (N_A_LAYERS, CONV_WIDTH, D), CONV_WIDTH ** -0.5),
        "conv_b": nrm(ks[8], (N_A_LAYERS, D), 0.02),
        "conv_w_out": nrm(ks[9], (N_A_LAYERS, D, D), BETA * D ** -0.5),
        "w_kv": nrm(ks[10], (D, 6 * N_KV_HEADS * HEAD_DIM), D ** -0.5),
        "cmp_pe": nrm(ks[11], (2, CMP_BLOCK, HEAD_DIM), 0.02),
        "cmp_w1": nrm(ks[12], (2, CMP_BLOCK * HEAD_DIM, CMP_HIDDEN), (CMP_BLOCK * HEAD_DIM) ** -0.5),
        "cmp_b1": nrm(ks[13], (2, CMP_HIDDEN), 0.02),
        "cmp_w2": nrm(ks[14], (2, CMP_HIDDEN, HEAD_DIM), CMP_HIDDEN ** -0.5),
        "cmp_b2": nrm(ks[15], (2, HEAD_DIM), 0.02),
        "w_qg": nrm(ks[16], (N_B_LAYERS, D, N_HEADS * HEAD_DIM + 3 * N_HEADS), D ** -0.5),
        "b_g": nrm(ks[17], (N_B_LAYERS, 3 * N_HEADS), 0.02),
        "w_o": nrm(ks[18], (N_B_LAYERS, N_HEADS * HEAD_DIM, D), BETA * (N_HEADS * HEAD_DIM) ** -0.5),
        "router_w": nrm(ks[19], (D, N_EXPERTS), D ** -0.5),
        "router_bias": nrm(ks[20], (N_EXPERTS,), 0.01),
        "w_gu": nrm(ks[21], (DEPTH, N_EXPERTS, D, 2 * D_FF_EXPERT), D ** -0.5),
        "w_down": nrm(ks[22], (DEPTH, N_EXPERTS, D_FF_EXPERT, D), BETA * D_FF_EXPERT ** -0.5),
    }


def reference(x, c, ada_w, ada_b, ln_g, ln_b, conv_w_in, conv_w, conv_b, conv_w_out,
              w_kv, cmp_pe, cmp_w1, cmp_b1, cmp_w2, cmp_b2, w_qg, b_g, w_o,
              router_w, router_bias, w_gu, w_down):
    mod = jnp.einsum("bd,lde->lbe", jax.nn.silu(c), ada_w) + ada_b[:, None, :]
    shared = None
    for l in range(DEPTH):
        sh1, sc1, g1, sh2, sc2, g2 = jnp.split(mod[l][:, None, :], 6, axis=-1)
        u = x * (1.0 + sc1) + sh1
        if l < N_A_LAYERS:
            y = short_conv_mixer(u, conv_w_in[l], conv_w[l], conv_b[l], conv_w_out[l])
        else:
            if l == N_A_LAYERS:
                shared = nsa_shared_kv(x, w_kv, cmp_pe, cmp_w1, cmp_b1, cmp_w2, cmp_b2)
            lb = l - N_A_LAYERS
            y = nsa_mixer(u, w_qg[lb], b_g[lb], w_o[lb], *shared)
        x = layer_norm(ALPHA * x + g1 * y, ln_g[l, 0], ln_b[l, 0])
        u = x * (1.0 + sc2) + sh2
        y = grouped_moe(u, router_w, router_bias, w_gu[l], w_down[l])
        x = layer_norm(ALPHA * x + g2 * y, ln_g[l, 1], ln_b[l, 1])
    return x
```

```python
import functools

import jax
import jax.numpy as jnp
from jax import lax
from jax.experimental import pallas as pl
from jax.experimental.pallas import tpu as pltpu

F32 = jnp.float32
BF16 = jnp.bfloat16

D_MODEL = 1024
DEPTH = 2
N_A_LAYERS = DEPTH // 2
CONV_WIDTH = 3
N_HEADS = 16
HEAD_DIM = D_MODEL // N_HEADS
N_KV_HEADS = 4
GQA_GROUP = N_HEADS // N_KV_HEADS
CMP_BLOCK = 32
CMP_STRIDE = 16
CMP_HIDDEN = 4 * HEAD_DIM
SEL_BLOCK = 64
SEL_TOP = 16
WINDOW = 512
Q_BLOCK = 128
N_EXPERTS = 16
N_GROUPS = 4
EXPERTS_PER_GROUP = N_EXPERTS // N_GROUPS
TOP_K = 2
D_FF_EXPERT = D_MODEL // 2
ALPHA = (2 * DEPTH) ** 0.25
LN_EPS = 1e-5
NEG = -1e30
FORCE = 1e9

LANES = 128
VMEM_LIMIT_BYTES = 56 * 1024 * 1024

SEL_KEY_TILE = 256
WIN_KEY_TILE = Q_BLOCK
WIN_TILES = WINDOW // WIN_KEY_TILE + 1


def _dot(a, b):
    return jnp.dot(a, b, preferred_element_type=F32)


def _dot_nt(a, b):
    return lax.dot_general(a, b, (((1,), (1,)), ((), ())), preferred_element_type=F32)


def _split(x):
    hi = x.astype(BF16)
    lo = (x - hi.astype(F32)).astype(BF16)
    return hi, lo


def _dot3(a, b):
    ah, al = _split(a)
    bh, bl = _split(b)
    return _dot(ah, bh) + (_dot(ah, bl) + _dot(al, bh))


def _dot3_nt(a, b):
    ah, al = _split(a)
    bh, bl = _split(b)
    return _dot_nt(ah, bh) + (_dot_nt(ah, bl) + _dot_nt(al, bh))


def _layer_norm(r, g, b):
    mu = jnp.mean(r, axis=-1, keepdims=True)
    d = r - mu
    var = jnp.mean(d * d, axis=-1, keepdims=True)
    return d * lax.rsqrt(var + LN_EPS) * g + b


def _params(*sem):
    return pltpu.CompilerParams(dimension_semantics=sem, vmem_limit_bytes=VMEM_LIMIT_BYTES)


def _const_spec(shape):
    zeros = (0,) * len(shape)
    return pl.BlockSpec(shape, lambda *_: zeros)


def _mod_kernel(c_ref, w_ref, b_ref, o_ref):
    c = c_ref[...]
    s = c * jax.nn.sigmoid(c)
    o_ref[0] = _dot3(s, w_ref[0]) + b_ref[0]


def _modulation(c, ada_w, ada_b):
    depth, d, n = ada_w.shape
    bsz = c.shape[0]
    tn = 1536
    return pl.pallas_call(
        _mod_kernel,
        grid=(depth, n // tn),
        in_specs=[
            pl.BlockSpec((bsz, d), lambda l, j: (0, 0)),
            pl.BlockSpec((1, d, tn), lambda l, j: (l, 0, j)),
            pl.BlockSpec((1, 1, tn), lambda l, j: (l, 0, j)),
        ],
        out_specs=pl.BlockSpec((1, bsz, tn), lambda l, j: (l, 0, j)),
        out_shape=jax.ShapeDtypeStruct((depth, bsz, n), F32),
        compiler_params=_params("parallel", "parallel"),
        name="adaln_mod",
    )(c, ada_w, ada_b.reshape(depth, 1, n))


CONV_HALO = 8


def _conv_kernel(x_ref, mod_ref, win_ref, cw_ref, cb_ref, wout_ref, g_ref, b_ref, o_ref, z_ref):
    tm = x_ref.shape[1]
    d = x_ref.shape[2]

    @pl.when(pl.program_id(1) == 0)
    def _():
        z_ref[0:CONV_HALO, :] = jnp.zeros((CONV_HALO, d), F32)

    x = x_ref[0]
    sh = mod_ref[0, 0:1, :]
    sc = mod_ref[0, 1:2, :]
    gate = mod_ref[0, 2:3, :]
    u = (x * (1.0 + sc) + sh).astype(BF16)
    bch = _dot(u, win_ref[...])
    z = bch[:, d:2 * d] * bch[:, 2 * d:]
    z_ref[CONV_HALO:CONV_HALO + tm, :] = z
    z1 = z_ref[CONV_HALO - 1:CONV_HALO - 1 + tm, :]
    z2 = z_ref[CONV_HALO - 2:CONV_HALO - 2 + tm, :]
    conv = cw_ref[0:1, :] * z2 + cw_ref[1:2, :] * z1 + cw_ref[2:3, :] * z + cb_ref[...]
    v = (bch[:, :d] * conv).astype(BF16)
    y = _dot(v, wout_ref[...])
    o_ref[0] = _layer_norm(ALPHA * x + gate * y, g_ref[...], b_ref[...])
    z_ref[0:CONV_HALO, :] = z_ref[tm:tm + CONV_HALO, :]


def _conv_layer(x, mod, w_in, conv_w, conv_b, w_out, ln_g, ln_b):
    bsz, s, d = x.shape
    tm = 512
    return pl.pallas_call(
        _conv_kernel,
        grid=(bsz, s // tm),
        in_specs=[
            pl.BlockSpec((1, tm, d), lambda b, j: (b, j, 0)),
            pl.BlockSpec((1, 6, d), lambda b, j: (b, 0, 0)),
            _const_spec((d, 3 * d)),
            _const_spec((CONV_WIDTH, d)),
            _const_spec((1, d)),
            _const_spec((d, d)),
            _const_spec((1, d)),
            _const_spec((1, d)),
        ],
        out_specs=pl.BlockSpec((1, tm, d), lambda b, j: (b, j, 0)),
        out_shape=jax.ShapeDtypeStruct((bsz, s, d), F32),
        scratch_shapes=[pltpu.VMEM((tm + CONV_HALO, d), F32)],
        compiler_params=_params("arbitrary", "arbitrary"),
        name="conv_mixer",
    )(x, mod, w_in, conv_w, conv_b.reshape(1, d), w_out, ln_g.reshape(1, d), ln_b.reshape(1, d))


def _router_kernel(x_ref, mod_ref, rwt_ref, rb_ref, comb_ref):
    tm = x_ref.shape[0]
    x = x_ref[...]
    u = x * (1.0 + mod_ref[0, 4:5, :]) + mod_ref[0, 3:4, :]
    logits = _dot3_nt(rwt_ref[...], u)
    m = jnp.max(logits, axis=0, keepdims=True)
    e = jnp.exp(logits - m)
    aff = e / jnp.sum(e, axis=0, keepdims=True)
    biased = aff + rb_ref[...]
    aff_r = [aff[i:i + 1, :] for i in range(N_EXPERTS)]
    row = [biased[i:i + 1, :] for i in range(N_EXPERTS)]

    best_s, best = None, None
    for g in range(N_GROUPS):
        r = row[g * EXPERTS_PER_GROUP:(g + 1) * EXPERTS_PER_GROUP]
        gs = None
        for i in range(EXPERTS_PER_GROUP):
            for j in range(i + 1, EXPERTS_PER_GROUP):
                p = r[i] + r[j]
                gs = p if gs is None else jnp.maximum(gs, p)
        if g == 0:
            best_s, best = gs, jnp.zeros((1, tm), jnp.int32)
        else:
            upd = gs > best_s
            best = jnp.where(upd, g, best)
            best_s = jnp.where(upd, gs, best_s)

    masked = [jnp.where(best == (i // EXPERTS_PER_GROUP), row[i], NEG) for i in range(N_EXPERTS)]

    def first_argmax(vals):
        v, idx = vals[0], jnp.zeros((1, tm), jnp.int32)
        for i in range(1, N_EXPERTS):
            upd = vals[i] > v
            idx = jnp.where(upd, i, idx)
            v = jnp.where(upd, vals[i], v)
        return idx

    i0 = first_argmax(masked)
    i1 = first_argmax([jnp.where(i0 == i, -jnp.inf, masked[i]) for i in range(N_EXPERTS)])
    w0 = sum(jnp.where(i0 == i, aff_r[i], 0.0) for i in range(N_EXPERTS))
    w1 = sum(jnp.where(i1 == i, aff_r[i], 0.0) for i in range(N_EXPERTS))
    tot = w0 + w1
    w0 = w0 / tot
    w1 = w1 / tot
    rows = [jnp.where(i0 == i, w0, 0.0) + jnp.where(i1 == i, w1, 0.0) for i in range(N_EXPERTS)]
    comb_t = jnp.concatenate(rows + [jnp.zeros((LANES - N_EXPERTS, tm), F32)], axis=0)
    comb_ref[...] = comb_t.T


def _router(x, mod, router_wt, router_bias, seq):
    t, d = x.shape
    tm = 1024
    return pl.pallas_call(
        _router_kernel,
        grid=(t // tm,),
        in_specs=[
            pl.BlockSpec((tm, d), lambda i: (i, 0)),
            pl.BlockSpec((1, 6, d), lambda i: ((i * tm) // seq, 0, 0)),
            _const_spec((N_EXPERTS, d)),
            _const_spec((N_EXPERTS, 1)),
        ],
        out_specs=pl.BlockSpec((tm, LANES), lambda i: (i, 0)),
        out_shape=jax.ShapeDtypeStruct((t, LANES), F32),
        compiler_params=_params("parallel"),
        name="moe_router",
    )(x, mod, router_wt, router_bias.reshape(N_EXPERTS, 1))


def _moe_kernel(x_ref, mod_ref, comb_ref, wgu_ref, wdn_ref, g_ref, b_ref, o_ref, u_ref, acc_ref):
    e = pl.program_id(1)
    f = wdn_ref.shape[1]

    @pl.when(e == 0)
    def _():
        u = x_ref[...] * (1.0 + mod_ref[0, 4:5, :]) + mod_ref[0, 3:4, :]
        u_ref[...] = u.astype(BF16)
        acc_ref[...] = jnp.zeros(acc_ref.shape, F32)

    h = _dot(u_ref[...], wgu_ref[0])
    gate = h[:, :f]
    a = (gate * jax.nn.sigmoid(gate) * h[:, f:]).astype(BF16)
    y = _dot(a, wdn_ref[0])
    lane = lax.broadcasted_iota(jnp.int32, comb_ref.shape, 1)
    col = jnp.sum(jnp.where(lane == e, comb_ref[...], 0.0), axis=1, keepdims=True)
    acc_ref[...] += col * y

    @pl.when(e == pl.num_programs(1) - 1)
    def _():
        r = ALPHA * x_ref[...] + mod_ref[0, 5:6, :] * acc_ref[...]
        o_ref[...] = _layer_norm(r, g_ref[...], b_ref[...])


def _moe_layer(x, mod, comb, w_gu, w_down, ln_g, ln_b, seq):
    t, d = x.shape
    n_e, _, f2 = w_gu.shape
    f = f2 // 2
    tm = 1024
    return pl.pallas_call(
        _moe_kernel,
        grid=(t // tm, n_e),
        in_specs=[
            pl.BlockSpec((tm, d), lambda i, e: (i, 0)),
            pl.BlockSpec((1, 6, d), lambda i, e: ((i * tm) // seq, 0, 0)),
            pl.BlockSpec((tm, LANES), lambda i, e: (i, 0)),
            pl.BlockSpec((1, d, f2), lambda i, e: (e, 0, 0)),
            pl.BlockSpec((1, f, d), lambda i, e: (e, 0, 0)),
            _const_spec((1, d)),
            _const_spec((1, d)),
        ],
        out_specs=pl.BlockSpec((tm, d), lambda i, e: (i, 0)),
        out_shape=jax.ShapeDtypeStruct((t, d), F32),
        scratch_shapes=[pltpu.VMEM((tm, d), BF16), pltpu.VMEM((tm, d), F32)],
        compiler_params=_params("parallel", "arbitrary"),
        name="moe_experts",
    )(x, mod, comb, w_gu, w_down, ln_g.reshape(1, d), ln_b.reshape(1, d))


def _proj_kernel(x_ref, mod_ref, wn_ref, wvt_ref, wqt_ref, bg_ref,
                 kc_ref, vc_ref, ks_ref, kw_ref, vst_ref, vwt_ref, qt_ref, gt_ref):
    tm = x_ref.shape[1]
    j = pl.program_id(1)
    x = x_ref[0]
    xb = x.astype(BF16)
    nat = _dot(xb, wn_ref[...])
    vt = _dot_nt(wvt_ref[...], xb)
    u = (x * (1.0 + mod_ref[0, 1:2, :]) + mod_ref[0, 0:1, :]).astype(BF16)
    qg = _dot_nt(wqt_ref[...], u)
    nq = N_HEADS * HEAD_DIM
    qt_ref[0] = (qg[:nq, :] * (HEAD_DIM ** -0.5)).astype(BF16)
    gates = jax.nn.sigmoid(qg[nq:, :] + bg_ref[...])
    per = 3 * GQA_GROUP
    blk = (j * tm + lax.broadcasted_iota(jnp.int32, (tm, SEL_BLOCK), 0)) // SEL_BLOCK
    onehot = (blk == lax.broadcasted_iota(jnp.int32, (tm, SEL_BLOCK), 1)).astype(BF16)
    hw = N_KV_HEADS * HEAD_DIM
    for h in range(N_KV_HEADS):
        c0 = h * HEAD_DIM
        kc_ref[0, h] = nat[:, c0:c0 + HEAD_DIM]
        vc_ref[0, h] = nat[:, hw + c0:hw + c0 + HEAD_DIM]
        ks = nat[:, 2 * hw + c0:2 * hw + c0 + HEAD_DIM].astype(BF16)
        ks_ref[0, h] = jnp.concatenate([ks, onehot], axis=1)
        kw_ref[0, h] = nat[:, 3 * hw + c0:3 * hw + c0 + HEAD_DIM].astype(BF16)
        vs_t = vt[c0:c0 + HEAD_DIM, :].astype(BF16)
        vw_t = vt[hw + c0:hw + c0 + HEAD_DIM, :].astype(BF16)
        for p in range(tm // SEL_KEY_TILE):
            vst_ref[0, h, p] = vs_t[:, p * SEL_KEY_TILE:(p + 1) * SEL_KEY_TILE]
        for p in range(tm // WIN_KEY_TILE):
            vwt_ref[0, h, p] = vw_t[:, p * WIN_KEY_TILE:(p + 1) * WIN_KEY_TILE]
        gt_ref[0, h] = gates[h * per:(h + 1) * per, :]


def _projections(x, mod, w_nat, w_vt, w_qgt, b_g):
    bsz, s, d = x.shape
    tm = 512
    hkv, dh = N_KV_HEADS, HEAD_DIM
    per = 3 * GQA_GROUP
    nqg = w_qgt.shape[0]
    out_shape = (
        jax.ShapeDtypeStruct((bsz, hkv, s, dh), F32),
        jax.ShapeDtypeStruct((bsz, hkv, s, dh), F32),
        jax.ShapeDtypeStruct((bsz, hkv, s, 2 * dh), BF16),
        jax.ShapeDtypeStruct((bsz, hkv, s, dh), BF16),
        jax.ShapeDtypeStruct((bsz, hkv, s // SEL_KEY_TILE, dh, SEL_KEY_TILE), BF16),
        jax.ShapeDtypeStruct((bsz, hkv, s // WIN_KEY_TILE, dh, WIN_KEY_TILE), BF16),
        jax.ShapeDtypeStruct((bsz, N_HEADS * dh, s), BF16),
        jax.ShapeDtypeStruct((bsz, hkv, per, s), F32),
    )
    out_specs = (
        pl.BlockSpec((1, hkv, tm, dh), lambda b, j: (b, 0, j, 0)),
        pl.BlockSpec((1, hkv, tm, dh), lambda b, j: (b, 0, j, 0)),
        pl.BlockSpec((1, hkv, tm, 2 * dh), lambda b, j: (b, 0, j, 0)),
        pl.BlockSpec((1, hkv, tm, dh), lambda b, j: (b, 0, j, 0)),
        pl.BlockSpec((1, hkv, tm // SEL_KEY_TILE, dh, SEL_KEY_TILE), lambda b, j: (b, 0, j, 0, 0)),
        pl.BlockSpec((1, hkv, tm // WIN_KEY_TILE, dh, WIN_KEY_TILE), lambda b, j: (b, 0, j, 0, 0)),
        pl.BlockSpec((1, N_HEADS * dh, tm), lambda b, j: (b, 0, j)),
        pl.BlockSpec((1, hkv, per, tm), lambda b, j: (b, 0, 0, j)),
    )
    return pl.pallas_call(
        _proj_kernel,
        grid=(bsz, s // tm),
        in_specs=[
            pl.BlockSpec((1, tm, d), lambda b, j: (b, j, 0)),
            pl.BlockSpec((1, 6, d), lambda b, j: (b, 0, 0)),
            _const_spec(w_nat.shape),
            _const_spec(w_vt.shape),
            _const_spec(w_qgt.shape),
            _const_spec((nqg - N_HEADS * dh, 1)),
        ],
        out_specs=out_specs,
        out_shape=out_shape,
        compiler_params=_params("parallel", "parallel"),
        name="nsa_projections",
    )(x, mod, w_nat, w_vt, w_qgt, b_g)


def _compress_kernel(c_ref, pe_ref, w1_ref, b1_ref, w2_ref, b2_ref, nat_ref, tr_ref):
    half = c_ref.shape[-1]
    c = c_ref[0, 0, 0]
    lo = (c + pe_ref[0, :, :half]).astype(BF16)
    hi = (c + pe_ref[0, :, half:]).astype(BF16)
    p = _dot(lo, w1_ref[0, :half, :])
    q = _dot(hi, w1_ref[0, half:, :])
    n = c.shape[0]
    pre = p + pltpu.roll(q, n - 1, 0) + b1_ref[0]
    hdn = 0.5 * pre * (1.0 + jnp.tanh(0.7978845608028654 * (pre + 0.044715 * (pre * pre * pre))))
    out = _dot(hdn.astype(BF16), w2_ref[0]) + b2_ref[0]
    nat_ref[0, 0, 0] = out.astype(BF16)
    tr_ref[0, 0, 0] = out.T.astype(BF16)


def _compress(kv_c, pe, w1, b1, w2, b2):
    two, bsz, hkv, n, width = kv_c.shape
    hid = w1.shape[-1]
    dh = w2.shape[-1]
    return pl.pallas_call(
        _compress_kernel,
        grid=(two, bsz, hkv),
        in_specs=[
            pl.BlockSpec((1, 1, 1, n, width), lambda i, b, h: (i, b, h, 0, 0)),
            pl.BlockSpec((1, 1, 2 * width), lambda i, b, h: (i, 0, 0)),
            pl.BlockSpec((1, 2 * width, hid), lambda i, b, h: (i, 0, 0)),
            pl.BlockSpec((1, 1, hid), lambda i, b, h: (i, 0, 0)),
            pl.BlockSpec((1, hid, dh), lambda i, b, h: (i, 0, 0)),
            pl.BlockSpec((1, 1, dh), lambda i, b, h: (i, 0, 0)),
        ],
        out_specs=(
            pl.BlockSpec((1, 1, 1, n, dh), lambda i, b, h: (i, b, h, 0, 0)),
            pl.BlockSpec((1, 1, 1, dh, n), lambda i, b, h: (i, b, h, 0, 0)),
        ),
        out_shape=(
            jax.ShapeDtypeStruct((two, bsz, hkv, n, dh), BF16),
            jax.ShapeDtypeStruct((two, bsz, hkv, dh, n), BF16),
        ),
        compiler_params=_params("parallel", "parallel", "parallel"),
        name="nsa_compress",
    )(kv_c, pe, w1, b1, w2, b2)


def _softmax_step(s, vt, m, l, acc):
    m_new = jnp.maximum(m, jnp.max(s, axis=0, keepdims=True))
    alpha = jnp.exp(m - m_new)
    p = jnp.exp(s - m_new)
    l = alpha * l + jnp.sum(p, axis=0, keepdims=True)
    acc = alpha * acc + _dot(vt, p.astype(BF16))
    return m_new, l, acc


def _attn_kernel(q_ref, g_ref, kc_ref, vct_ref, ks_ref, vst_ref, kw_ref, vwt_ref, o_ref, sc_ref):
    i = pl.program_id(2)
    nq = GQA_GROUP * Q_BLOCK
    dh = HEAD_DIM
    qa = q_ref[0]
    qt = jnp.concatenate([qa[g * dh:(g + 1) * dh, :] for g in range(GQA_GROUP)], axis=1)
    t_row = i * Q_BLOCK + lax.broadcasted_iota(jnp.int32, (1, Q_BLOCK), 1)
    t4 = jnp.concatenate([t_row] * GQA_GROUP, axis=1)

    n_cmp = kc_ref.shape[2]
    s_c = _dot(kc_ref[0, 0], qt)
    cmp_end = lax.broadcasted_iota(jnp.int32, (n_cmp, 1), 0) * CMP_STRIDE + (CMP_BLOCK - 1)
    vis = cmp_end <= t4
    s_c = jnp.where(vis, s_c, NEG)
    m_c = jnp.max(s_c, axis=0, keepdims=True)
    e_c = jnp.where(vis, jnp.exp(s_c - m_c), 0.0)
    p_c = e_c / jnp.maximum(jnp.sum(e_c, axis=0, keepdims=True), 1e-30)
    o_c = _dot(vct_ref[0, 0], p_c.astype(BF16))

    n_sel = ks_ref.shape[2] // SEL_BLOCK
    p_sum = sum(p_c[:, g * Q_BLOCK:(g + 1) * Q_BLOCK] for g in range(GQA_GROUP))
    jj = lax.broadcasted_iota(jnp.int32, (n_sel, n_cmp), 0) * SEL_BLOCK
    nn = lax.broadcasted_iota(jnp.int32, (n_sel, n_cmp), 1) * CMP_STRIDE
    ov = jnp.minimum(nn + CMP_BLOCK, jj + SEL_BLOCK) - jnp.maximum(nn, jj)
    ov_t = jnp.maximum(ov, 0).astype(F32) * (1.0 / CMP_BLOCK)
    imp = _dot3(ov_t, p_sum)
    jb = lax.broadcasted_iota(jnp.int32, (n_sel, 1), 0)
    cur = t_row // SEL_BLOCK
    allowed = jb <= cur
    forced = (jb == 0) | (jb == cur) | (jb == cur - 1)
    score = jnp.where(forced & allowed, FORCE, jnp.where(allowed, imp, NEG))
    sc_ref[...] = score
    n_top = min(SEL_TOP, n_sel)
    sub = 8
    groups = [score[r:r + sub, :] for r in range(0, n_sel, sub)]
    ranks = [jnp.zeros((sub, Q_BLOCK), F32) for _ in groups]
    for k in range(n_sel):
        row = jnp.broadcast_to(sc_ref[k:k + 1, :], (sub, Q_BLOCK))
        for r, grp in enumerate(groups):
            if r * sub > k:
                ahead = row >= grp
            elif r * sub + sub - 1 < k:
                ahead = row > grp
            else:
                later = (r * sub + lax.broadcasted_iota(jnp.int32, (sub, 1), 0)) > k
                ahead = (row > grp) | ((row == grp) & later)
            ranks[r] = ranks[r] + jnp.where(ahead, 1.0, 0.0)
    rank = jnp.concatenate(ranks, axis=0)
    chosen = (rank < n_top) & (score > 0.5 * NEG)
    bias = jnp.where(chosen, 0.0, NEG).astype(BF16)
    pad = ks_ref.shape[3] - dh - n_sel
    if pad:
        bias = jnp.concatenate([bias, jnp.zeros((pad, Q_BLOCK), BF16)], axis=0)
    q_aug = jnp.concatenate([qt, jnp.concatenate([bias] * GQA_GROUP, axis=1)], axis=0)

    m0 = jnp.full((1, nq), NEG, F32)
    l0 = jnp.zeros((1, nq), F32)
    a0 = jnp.zeros((dh, nq), F32)

    def sel_tile(kt, carry, causal):
        k_aug = ks_ref[0, 0, pl.ds(pl.multiple_of(kt * SEL_KEY_TILE, SEL_KEY_TILE), SEL_KEY_TILE), :]
        s = _dot(k_aug, q_aug)
        if causal:
            key = kt * SEL_KEY_TILE + lax.broadcasted_iota(jnp.int32, (SEL_KEY_TILE, 1), 0)
            s = jnp.where(key <= t4, s, NEG)
        return _softmax_step(s, vst_ref[0, 0, kt], *carry)

    last = (i * Q_BLOCK) // SEL_KEY_TILE
    carry = lax.fori_loop(0, last, lambda kt, c: sel_tile(kt, c, False), (m0, l0, a0))
    _, l_s, a_s = sel_tile(last, carry, True)
    o_s = a_s / l_s

    def win_tile(w, carry):
        kt = i - w
        k = kw_ref[0, 0, pl.ds(pl.multiple_of(kt * WIN_KEY_TILE, WIN_KEY_TILE), WIN_KEY_TILE), :]
        s = _dot(k, qt)
        key = kt * WIN_KEY_TILE + lax.broadcasted_iota(jnp.int32, (WIN_KEY_TILE, 1), 0)
        s = jnp.where((key <= t4) & (key > t4 - WINDOW), s, NEG)
        return _softmax_step(s, vwt_ref[0, 0, kt], *carry)

    _, l_w, a_w = lax.fori_loop(0, jnp.minimum(i + 1, WIN_TILES), win_tile, (m0, l0, a0))
    o_w = a_w / l_w

    gates = g_ref[0, 0]
    def gate_row(br):
        return jnp.concatenate([gates[br * GQA_GROUP + g:br * GQA_GROUP + g + 1, :] for g in range(GQA_GROUP)], axis=1)
    o = gate_row(0) * o_c + gate_row(1) * o_s + gate_row(2) * o_w
    o_ref[0] = jnp.concatenate([o[:, g * Q_BLOCK:(g + 1) * Q_BLOCK].T for g in range(GQA_GROUP)], axis=1).astype(BF16)


def _attention(q_t, gates_t, kc, vc_t, ks_aug, vs_t, kw, vw_t):
    bsz, hkv, s, dh = kw.shape
    n_cmp = kc.shape[2]
    gd = GQA_GROUP * dh
    return pl.pallas_call(
        _attn_kernel,
        grid=(bsz, hkv, s // Q_BLOCK),
        in_specs=[
            pl.BlockSpec((1, gd, Q_BLOCK), lambda b, h, i: (b, h, i)),
            pl.BlockSpec((1, 1, 3 * GQA_GROUP, Q_BLOCK), lambda b, h, i: (b, h, 0, i)),
            pl.BlockSpec((1, 1, n_cmp, dh), lambda b, h, i: (b, h, 0, 0)),
            pl.BlockSpec((1, 1, dh, n_cmp), lambda b, h, i: (b, h, 0, 0)),
            pl.BlockSpec((1, 1, s, 2 * dh), lambda b, h, i: (b, h, 0, 0)),
            pl.BlockSpec((1, 1, s // SEL_KEY_TILE, dh, SEL_KEY_TILE), lambda b, h, i: (b, h, 0, 0, 0)),
            pl.BlockSpec((1, 1, s, dh), lambda b, h, i: (b, h, 0, 0)),
            pl.BlockSpec((1, 1, s // WIN_KEY_TILE, dh, WIN_KEY_TILE), lambda b, h, i: (b, h, 0, 0, 0)),
        ],
        out_specs=pl.BlockSpec((1, Q_BLOCK, gd), lambda b, h, i: (b, i, h)),
        out_shape=jax.ShapeDtypeStruct((bsz, s, hkv * gd), BF16),
        scratch_shapes=[pltpu.VMEM((s // SEL_BLOCK, Q_BLOCK), F32)],
        compiler_params=_params("parallel", "parallel", "arbitrary"),
        name="nsa_attention",
    )(q_t, gates_t, kc, vc_t, ks_aug, vs_t, kw, vw_t)


def _oproj_kernel(o_ref, x_ref, mod_ref, w_ref, g_ref, b_ref, out_ref):
    y = _dot(o_ref[0], w_ref[...])
    out_ref[0] = _layer_norm(ALPHA * x_ref[0] + mod_ref[0, 2:3, :] * y, g_ref[...], b_ref[...])


def _out_projection(o, x, mod, w_o, ln_g, ln_b):
    bsz, s, d = x.shape
    tm = 512
    return pl.pallas_call(
        _oproj_kernel,
        grid=(bsz, s // tm),
        in_specs=[
            pl.BlockSpec((1, tm, o.shape[-1]), lambda b, j: (b, j, 0)),
            pl.BlockSpec((1, tm, d), lambda b, j: (b, j, 0)),
            pl.BlockSpec((1, 6, d), lambda b, j: (b, 0, 0)),
            _const_spec(w_o.shape),
            _const_spec((1, d)),
            _const_spec((1, d)),
        ],
        out_specs=pl.BlockSpec((1, tm, d), lambda b, j: (b, j, 0)),
        out_shape=jax.ShapeDtypeStruct((bsz, s, d), F32),
        compiler_params=_params("parallel", "parallel"),
        name="nsa_out_proj",
    )(o, x, mod, w_o, ln_g.reshape(1, d), ln_b.reshape(1, d))


def _moe_block(x, mod, router_wt, router_bias, w_gu, w_down, ln_g, ln_b):
    bsz, s, d = x.shape
    xf = x.reshape(bsz * s, d)
    comb = _router(xf, mod, router_wt, router_bias, s)
    out = _moe_layer(xf, mod, comb, w_gu, w_down, ln_g, ln_b, s)
    return out.reshape(bsz, s, d)


def _nsa_layer(x, mod, w_kv, cmp_pe, cmp_w1, cmp_b1, cmp_w2, cmp_b2, w_qg, b_g, w_o, ln_g, ln_b):
    bsz, s, d = x.shape
    hkv, dh, grp = N_KV_HEADS, HEAD_DIM, GQA_GROUP
    hw = hkv * dh
    kvw = w_kv.reshape(d, 6, hw)
    w_nat = jnp.concatenate([kvw[:, 0], kvw[:, 1], kvw[:, 2], kvw[:, 4]], axis=1).astype(BF16)
    w_vt = jnp.concatenate([kvw[:, 3], kvw[:, 5]], axis=1).T.astype(BF16)
    nq = N_HEADS * dh
    wg = w_qg[:, nq:].reshape(d, hkv, grp, 3).transpose(0, 1, 3, 2).reshape(d, 3 * N_HEADS)
    w_qgt = jnp.concatenate([w_qg[:, :nq], wg], axis=1).T.astype(BF16)
    bg = b_g.reshape(hkv, grp, 3).transpose(0, 2, 1).reshape(3 * N_HEADS, 1)
    kc, vc, ks_aug, kw, vs_t, vw_t, q_t, gates_t = _projections(x, mod, w_nat, w_vt, w_qgt, bg)
    n_chunk = s // CMP_STRIDE
    kv_c = jnp.stack([kc, vc]).reshape(2, bsz, hkv, n_chunk, CMP_STRIDE * dh)
    cmp_nat, cmp_t = _compress(kv_c, cmp_pe.reshape(2, 1, CMP_BLOCK * dh), cmp_w1.astype(BF16),
                               cmp_b1.reshape(2, 1, CMP_HIDDEN), cmp_w2.astype(BF16), cmp_b2.reshape(2, 1, dh))
    o = _attention(q_t, gates_t, cmp_nat[0], cmp_t[1], ks_aug, vs_t, kw, vw_t)
    return _out_projection(o, x, mod, w_o.astype(BF16), ln_g, ln_b)


def kernel(x, c, ada_w, ada_b, ln_g, ln_b, conv_w_in, conv_w, conv_b, conv_w_out, w_kv, cmp_pe, cmp_w1, cmp_b1, cmp_w2, cmp_b2, w_qg, b_g, w_o, router_w, router_bias, w_gu, w_down):
    bsz, s, d = x.shape
    mod = _modulation(c, ada_w, ada_b).reshape(DEPTH, bsz, 6, d)
    router_wt = router_w.T
    w_gu_b = w_gu.astype(BF16)
    w_down_b = w_down.astype(BF16)

    x = _conv_layer(x, mod[0], conv_w_in[0].astype(BF16), conv_w[0], conv_b[0], conv_w_out[0].astype(BF16),
                    ln_g[0, 0], ln_b[0, 0])
    x = _moe_block(x, mod[0], router_wt, router_bias, w_gu_b[0], w_down_b[0], ln_g[0, 1], ln_b[0, 1])

    x = _nsa_layer(x, mod[1], w_kv, cmp_pe, cmp_w1, cmp_b1, cmp_w2, cmp_b2, w_qg[0], b_g[0], w_o[0],
                   ln_g[1, 0], ln_b[1, 0])
    x = _moe_block(x, mod[1], router_wt, router_bias, w_gu_b[1], w_down_b[1], ln_g[1, 1], ln_b[1, 1])
    return x
```

```python
import functools
import math

import jax
import jax.numpy as jnp
import numpy as np
from jax import lax
from jax.experimental import pallas as pl
from jax.experimental.pallas import tpu as pltpu

F32 = jnp.float32
BF16 = jnp.bfloat16

D_MODEL = 1024
DEPTH = 2
N_A_LAYERS = DEPTH // 2
CONV_WIDTH = 3
N_HEADS = 16
HEAD_DIM = D_MODEL // N_HEADS
N_KV_HEADS = 4
GQA_GROUP = N_HEADS // N_KV_HEADS
CMP_BLOCK = 32
CMP_STRIDE = 16
CMP_HIDDEN = 4 * HEAD_DIM
SEL_BLOCK = 64
SEL_TOP = 16
WINDOW = 512
Q_BLOCK = 128
N_EXPERTS = 16
N_GROUPS = 4
EXPERTS_PER_GROUP = N_EXPERTS // N_GROUPS
TOP_K = 2
D_FF_EXPERT = D_MODEL // 2
ALPHA = (2 * DEPTH) ** 0.25
LN_EPS = 1e-5
NEG = -1e30
FORCE = 1e9

LANES = 128
VMEM_LIMIT_BYTES = 56 * 1024 * 1024

LOG2E = math.log2(math.e)

KEY_TILE = 256
WIN_PAD_LO = WINDOW
WIN_PAD_HI = Q_BLOCK
PAT_ZERO, PAT_DIAG, PAT_DIAG_SHIFT, PAT_ANTI, PAT_UPPER, PAT_NONE = range(6)


def _dot(a, b):
    return jnp.dot(a, b, preferred_element_type=F32)


def _dot_nt(a, b):
    return lax.dot_general(a, b, (((1,), (1,)), ((), ())), preferred_element_type=F32)


def _split(x):
    hi = x.astype(BF16)
    lo = (x - hi.astype(F32)).astype(BF16)
    return hi, lo


def _dot3(a, b):
    ah, al = _split(a)
    bh, bl = _split(b)
    return _dot(ah, bh) + (_dot(ah, bl) + _dot(al, bh))


def _dot3_nt(a, b):
    ah, al = _split(a)
    bh, bl = _split(b)
    return _dot_nt(ah, bh) + (_dot_nt(ah, bl) + _dot_nt(al, bh))


def _layer_norm(r, g, b):
    mu = jnp.mean(r, axis=-1, keepdims=True)
    d = r - mu
    var = jnp.mean(d * d, axis=-1, keepdims=True)
    return d * lax.rsqrt(var + LN_EPS) * g + b


def _params(*sem):
    return pltpu.CompilerParams(dimension_semantics=sem, vmem_limit_bytes=VMEM_LIMIT_BYTES)


def _const_spec(shape):
    zeros = (0,) * len(shape)
    return pl.BlockSpec(shape, lambda *_: zeros)


def _mod_kernel(c_ref, w_ref, b_ref, o_ref):
    c = c_ref[...]
    s = c * jax.nn.sigmoid(c)
    o_ref[0] = _dot3(s, w_ref[0]) + b_ref[0]


def _modulation(c, ada_w, ada_b):
    depth, d, n = ada_w.shape
    bsz = c.shape[0]
    tn = 1536
    return pl.pallas_call(
        _mod_kernel,
        grid=(depth, n // tn),
        in_specs=[
            pl.BlockSpec((bsz, d), lambda l, j: (0, 0)),
            pl.BlockSpec((1, d, tn), lambda l, j: (l, 0, j)),
            pl.BlockSpec((1, 1, tn), lambda l, j: (l, 0, j)),
        ],
        out_specs=pl.BlockSpec((1, bsz, tn), lambda l, j: (l, 0, j)),
        out_shape=jax.ShapeDtypeStruct((depth, bsz, n), F32),
        compiler_params=_params("parallel", "parallel"),
        name="adaln_mod",
    )(c, ada_w, ada_b.reshape(depth, 1, n))


CONV_HALO = 8


def _conv_kernel(x_ref, mod_ref, win_ref, cw_ref, cb_ref, wout_ref, g_ref, b_ref, o_ref, z_ref):
    tm = x_ref.shape[1]
    d = x_ref.shape[2]

    @pl.when(pl.program_id(1) == 0)
    def _():
        z_ref[0:CONV_HALO, :] = jnp.zeros((CONV_HALO, d), F32)

    x = x_ref[0]
    sh = mod_ref[0, 0:1, :]
    sc = mod_ref[0, 1:2, :]
    gate = mod_ref[0, 2:3, :]
    u = (x * (1.0 + sc) + sh).astype(BF16)
    bch = _dot(u, win_ref[...])
    z = bch[:, d:2 * d] * bch[:, 2 * d:]
    z_ref[CONV_HALO:CONV_HALO + tm, :] = z
    z1 = z_ref[CONV_HALO - 1:CONV_HALO - 1 + tm, :]
    z2 = z_ref[CONV_HALO - 2:CONV_HALO - 2 + tm, :]
    conv = cw_ref[0:1, :] * z2 + cw_ref[1:2, :] * z1 + cw_ref[2:3, :] * z + cb_ref[...]
    v = (bch[:, :d] * conv).astype(BF16)
    y = _dot(v, wout_ref[...])
    o_ref[0] = _layer_norm(ALPHA * x + gate * y, g_ref[...], b_ref[...])
    z_ref[0:CONV_HALO, :] = z_ref[tm:tm + CONV_HALO, :]


def _conv_layer(x, mod, w_in, conv_w, conv_b, w_out, ln_g, ln_b):
    bsz, s, d = x.shape
    tm = 512
    return pl.pallas_call(
        _conv_kernel,
        grid=(bsz, s // tm),
        in_specs=[
            pl.BlockSpec((1, tm, d), lambda b, j: (b, j, 0)),
            pl.BlockSpec((1, 6, d), lambda b, j: (b, 0, 0)),
            _const_spec((d, 3 * d)),
            _const_spec((CONV_WIDTH, d)),
            _const_spec((1, d)),
            _const_spec((d, d)),
            _const_spec((1, d)),
            _const_spec((1, d)),
        ],
        out_specs=pl.BlockSpec((1, tm, d), lambda b, j: (b, j, 0)),
        out_shape=jax.ShapeDtypeStruct((bsz, s, d), F32),
        scratch_shapes=[pltpu.VMEM((tm + CONV_HALO, d), F32)],
        compiler_params=_params("arbitrary", "arbitrary"),
        name="conv_mixer",
    )(x, mod, w_in, conv_w, conv_b.reshape(1, d), w_out, ln_g.reshape(1, d), ln_b.reshape(1, d))


def _router_kernel(x_ref, mod_ref, rwt_ref, rb_ref, comb_ref):
    tm = x_ref.shape[0]
    x = x_ref[...]
    u = x * (1.0 + mod_ref[0, 4:5, :]) + mod_ref[0, 3:4, :]
    logits = _dot3_nt(rwt_ref[...], u)
    m = jnp.max(logits, axis=0, keepdims=True)
    e = jnp.exp(logits - m)
    aff = e / jnp.sum(e, axis=0, keepdims=True)
    biased = aff + rb_ref[...]
    aff_r = [aff[i:i + 1, :] for i in range(N_EXPERTS)]
    row = [biased[i:i + 1, :] for i in range(N_EXPERTS)]

    best_s, best = None, None
    for g in range(N_GROUPS):
        r = row[g * EXPERTS_PER_GROUP:(g + 1) * EXPERTS_PER_GROUP]
        gs = None
        for i in range(EXPERTS_PER_GROUP):
            for j in range(i + 1, EXPERTS_PER_GROUP):
                p = r[i] + r[j]
                gs = p if gs is None else jnp.maximum(gs, p)
        if g == 0:
            best_s, best = gs, jnp.zeros((1, tm), jnp.int32)
        else:
            upd = gs > best_s
            best = jnp.where(upd, g, best)
            best_s = jnp.where(upd, gs, best_s)

    masked = [jnp.where(best == (i // EXPERTS_PER_GROUP), row[i], NEG) for i in range(N_EXPERTS)]

    def first_argmax(vals):
        v, idx = vals[0], jnp.zeros((1, tm), jnp.int32)
        for i in range(1, N_EXPERTS):
            upd = vals[i] > v
            idx = jnp.where(upd, i, idx)
            v = jnp.where(upd, vals[i], v)
        return idx

    i0 = first_argmax(masked)
    i1 = first_argmax([jnp.where(i0 == i, -jnp.inf, masked[i]) for i in range(N_EXPERTS)])
    w0 = sum(jnp.where(i0 == i, aff_r[i], 0.0) for i in range(N_EXPERTS))
    w1 = sum(jnp.where(i1 == i, aff_r[i], 0.0) for i in range(N_EXPERTS))
    tot = w0 + w1
    w0 = w0 / tot
    w1 = w1 / tot
    rows = [jnp.where(i0 == i, w0, 0.0) + jnp.where(i1 == i, w1, 0.0) for i in range(N_EXPERTS)]
    comb_t = jnp.concatenate(rows + [jnp.zeros((LANES - N_EXPERTS, tm), F32)], axis=0)
    comb_ref[...] = comb_t.T


def _router(x, mod, router_wt, router_bias, seq):
    t, d = x.shape
    tm = 1024
    return pl.pallas_call(
        _router_kernel,
        grid=(t // tm,),
        in_specs=[
            pl.BlockSpec((tm, d), lambda i: (i, 0)),
            pl.BlockSpec((1, 6, d), lambda i: ((i * tm) // seq, 0, 0)),
            _const_spec((N_EXPERTS, d)),
            _const_spec((N_EXPERTS, 1)),
        ],
        out_specs=pl.BlockSpec((tm, LANES), lambda i: (i, 0)),
        out_shape=jax.ShapeDtypeStruct((t, LANES), F32),
        compiler_params=_params("parallel"),
        name="moe_router",
    )(x, mod, router_wt, router_bias.reshape(N_EXPERTS, 1))


def _moe_kernel(x_ref, mod_ref, comb_ref, wgu_ref, wdn_ref, g_ref, b_ref, o_ref, u_ref, acc_ref):
    e = pl.program_id(1)
    f = wdn_ref.shape[1]

    @pl.when(e == 0)
    def _():
        u = x_ref[...] * (1.0 + mod_ref[0, 4:5, :]) + mod_ref[0, 3:4, :]
        u_ref[...] = u.astype(BF16)
        acc_ref[...] = jnp.zeros(acc_ref.shape, F32)

    h = _dot(u_ref[...], wgu_ref[0])
    gate = h[:, :f]
    a = (gate * jax.nn.sigmoid(gate) * h[:, f:]).astype(BF16)
    y = _dot(a, wdn_ref[0])
    lane = lax.broadcasted_iota(jnp.int32, comb_ref.shape, 1)
    col = jnp.sum(jnp.where(lane == e, comb_ref[...], 0.0), axis=1, keepdims=True)
    acc_ref[...] += col * y

    @pl.when(e == pl.num_programs(1) - 1)
    def _():
        r = ALPHA * x_ref[...] + mod_ref[0, 5:6, :] * acc_ref[...]
        o_ref[...] = _layer_norm(r, g_ref[...], b_ref[...])


def _moe_layer(x, mod, comb, w_gu, w_down, ln_g, ln_b, seq):
    t, d = x.shape
    n_e, _, f2 = w_gu.shape
    f = f2 // 2
    tm = 1024
    return pl.pallas_call(
        _moe_kernel,
        grid=(t // tm, n_e),
        in_specs=[
            pl.BlockSpec((tm, d), lambda i, e: (i, 0)),
            pl.BlockSpec((1, 6, d), lambda i, e: ((i * tm) // seq, 0, 0)),
            pl.BlockSpec((tm, LANES), lambda i, e: (i, 0)),
            pl.BlockSpec((1, d, f2), lambda i, e: (e, 0, 0)),
            pl.BlockSpec((1, f, d), lambda i, e: (e, 0, 0)),
            _const_spec((1, d)),
            _const_spec((1, d)),
        ],
        out_specs=pl.BlockSpec((tm, d), lambda i, e: (i, 0)),
        out_shape=jax.ShapeDtypeStruct((t, d), F32),
        scratch_shapes=[pltpu.VMEM((tm, d), BF16), pltpu.VMEM((tm, d), F32)],
        compiler_params=_params("parallel", "arbitrary"),
        name="moe_experts",
    )(x, mod, comb, w_gu, w_down, ln_g.reshape(1, d), ln_b.reshape(1, d))


def _proj_kernel(x_ref, mod_ref, wn_ref, wvt_ref, wqt_ref, bg_ref,
                 kc_ref, vc_ref, ks_ref, kw_ref, vst_ref, vwt_ref, qt_ref, gt_ref):
    tm = x_ref.shape[1]
    j = pl.program_id(1)
    x = x_ref[0]
    xb = x.astype(BF16)
    nat = _dot(xb, wn_ref[...])
    vt = _dot_nt(wvt_ref[...], xb)
    u = (x * (1.0 + mod_ref[0, 1:2, :]) + mod_ref[0, 0:1, :]).astype(BF16)
    qg = _dot_nt(wqt_ref[...], u)
    nq = N_HEADS * HEAD_DIM
    qt_ref[0] = (qg[:nq, :] * (HEAD_DIM ** -0.5 * LOG2E)).astype(BF16)
    gates = jax.nn.sigmoid(qg[nq:, :] + bg_ref[...])
    per = 3 * GQA_GROUP
    blk = (j * tm + lax.broadcasted_iota(jnp.int32, (tm, SEL_BLOCK), 0)) // SEL_BLOCK
    onehot = (blk == lax.broadcasted_iota(jnp.int32, (tm, SEL_BLOCK), 1)).astype(BF16)
    zeros = jnp.zeros((tm, SEL_BLOCK), BF16)
    hw = N_KV_HEADS * HEAD_DIM
    for h in range(N_KV_HEADS):
        c0 = h * HEAD_DIM
        kc_ref[0, h] = nat[:, c0:c0 + HEAD_DIM]
        vc_ref[0, h] = nat[:, hw + c0:hw + c0 + HEAD_DIM]
        ks = nat[:, 2 * hw + c0:2 * hw + c0 + HEAD_DIM].astype(BF16)
        ks_ref[0, h] = jnp.concatenate([ks, onehot], axis=1)
        kw = nat[:, 3 * hw + c0:3 * hw + c0 + HEAD_DIM].astype(BF16)
        kw_ref[0, h] = jnp.concatenate([kw, zeros], axis=1)
        vst_ref[0, h] = vt[c0:c0 + HEAD_DIM, :].astype(BF16)
        vwt_ref[0, h] = vt[hw + c0:hw + c0 + HEAD_DIM, :].astype(BF16)
        gt_ref[0, h] = gates[h * per:(h + 1) * per, :]


def _projections(x, mod, w_nat, w_vt, w_qgt, b_g):
    bsz, s, d = x.shape
    tm = 512
    hkv, dh = N_KV_HEADS, HEAD_DIM
    per = 3 * GQA_GROUP
    nqg = w_qgt.shape[0]
    out_shape = (
        jax.ShapeDtypeStruct((bsz, hkv, s, dh), F32),
        jax.ShapeDtypeStruct((bsz, hkv, s, dh), F32),
        jax.ShapeDtypeStruct((bsz, hkv, s, 2 * dh), BF16),
        jax.ShapeDtypeStruct((bsz, hkv, s, 2 * dh), BF16),
        jax.ShapeDtypeStruct((bsz, hkv, dh, s), BF16),
        jax.ShapeDtypeStruct((bsz, hkv, dh, s), BF16),
        jax.ShapeDtypeStruct((bsz, N_HEADS * dh, s), BF16),
        jax.ShapeDtypeStruct((bsz, hkv, per, s), F32),
    )
    out_specs = (
        pl.BlockSpec((1, hkv, tm, dh), lambda b, j: (b, 0, j, 0)),
        pl.BlockSpec((1, hkv, tm, dh), lambda b, j: (b, 0, j, 0)),
        pl.BlockSpec((1, hkv, tm, 2 * dh), lambda b, j: (b, 0, j, 0)),
        pl.BlockSpec((1, hkv, tm, 2 * dh), lambda b, j: (b, 0, j, 0)),
        pl.BlockSpec((1, hkv, dh, tm), lambda b, j: (b, 0, 0, j)),
        pl.BlockSpec((1, hkv, dh, tm), lambda b, j: (b, 0, 0, j)),
        pl.BlockSpec((1, N_HEADS * dh, tm), lambda b, j: (b, 0, j)),
        pl.BlockSpec((1, hkv, per, tm), lambda b, j: (b, 0, 0, j)),
    )
    return pl.pallas_call(
        _proj_kernel,
        grid=(bsz, s // tm),
        in_specs=[
            pl.BlockSpec((1, tm, d), lambda b, j: (b, j, 0)),
            pl.BlockSpec((1, 6, d), lambda b, j: (b, 0, 0)),
            _const_spec(w_nat.shape),
            _const_spec(w_vt.shape),
            _const_spec(w_qgt.shape),
            _const_spec((nqg - N_HEADS * dh, 1)),
        ],
        out_specs=out_specs,
        out_shape=out_shape,
        compiler_params=_params("parallel", "parallel"),
        name="nsa_projections",
    )(x, mod, w_nat, w_vt, w_qgt, b_g)


def _compress_kernel(c_ref, pe_ref, w1_ref, b1_ref, w2_ref, b2_ref, nat_ref, tr_ref):
    half = c_ref.shape[-1]
    c = c_ref[0, 0, 0]
    lo = (c + pe_ref[0, :, :half]).astype(BF16)
    hi = (c + pe_ref[0, :, half:]).astype(BF16)
    p = _dot(lo, w1_ref[0, :half, :])
    q = _dot(hi, w1_ref[0, half:, :])
    n = c.shape[0]
    pre = p + pltpu.roll(q, n - 1, 0) + b1_ref[0]
    hdn = 0.5 * pre * (1.0 + jnp.tanh(0.7978845608028654 * (pre + 0.044715 * (pre * pre * pre))))
    out = _dot(hdn.astype(BF16), w2_ref[0]) + b2_ref[0]
    nat_ref[0, 0, 0] = out.astype(BF16)
    tr_ref[0, 0, 0] = out.T.astype(BF16)


def _compress(kv_c, pe, w1, b1, w2, b2):
    two, bsz, hkv, n, width = kv_c.shape
    hid = w1.shape[-1]
    dh = w2.shape[-1]
    return pl.pallas_call(
        _compress_kernel,
        grid=(two, bsz, hkv),
        in_specs=[
            pl.BlockSpec((1, 1, 1, n, width), lambda i, b, h: (i, b, h, 0, 0)),
            pl.BlockSpec((1, 1, 2 * width), lambda i, b, h: (i, 0, 0)),
            pl.BlockSpec((1, 2 * width, hid), lambda i, b, h: (i, 0, 0)),
            pl.BlockSpec((1, 1, hid), lambda i, b, h: (i, 0, 0)),
            pl.BlockSpec((1, hid, dh), lambda i, b, h: (i, 0, 0)),
            pl.BlockSpec((1, 1, dh), lambda i, b, h: (i, 0, 0)),
        ],
        out_specs=(
            pl.BlockSpec((1, 1, 1, n, dh), lambda i, b, h: (i, b, h, 0, 0)),
            pl.BlockSpec((1, 1, 1, dh, n), lambda i, b, h: (i, b, h, 0, 0)),
        ),
        out_shape=(
            jax.ShapeDtypeStruct((two, bsz, hkv, n, dh), BF16),
            jax.ShapeDtypeStruct((two, bsz, hkv, dh, n), BF16),
        ),
        compiler_params=_params("parallel", "parallel", "parallel"),
        name="nsa_compress",
    )(kv_c, pe, w1, b1, w2, b2)


def _mask_patterns():
    keyl = np.arange(KEY_TILE)[:, None]
    ql = (np.arange(GQA_GROUP * Q_BLOCK) % Q_BLOCK)[None, :]
    true = np.ones((KEY_TILE, GQA_GROUP * Q_BLOCK), bool)
    valid = np.stack([
        true,
        keyl <= ql,
        keyl <= ql + Q_BLOCK,
        keyl > ql,
        (keyl >= Q_BLOCK) & true,
        ~true,
    ])
    return jnp.asarray(np.where(valid, 0.0, NEG), F32)


def _attn_kernel(q_ref, g_ref, kc_ref, vct_ref, kall_ref, vt_ref, pat_ref, o_ref,
                 sc_ref, qaug_ref, s0_ref, s1_ref, p0_ref, p1_ref, al_ref, m_ref, l_ref, acc_ref, *, seq):
    i = pl.program_id(2)
    nq = GQA_GROUP * Q_BLOCK
    dh = HEAD_DIM
    qa = q_ref[0]
    qt = jnp.concatenate([qa[g * dh:(g + 1) * dh, :] for g in range(GQA_GROUP)], axis=1)
    t_row = i * Q_BLOCK + lax.broadcasted_iota(jnp.int32, (1, Q_BLOCK), 1)
    t4 = jnp.concatenate([t_row] * GQA_GROUP, axis=1)

    n_cmp = kc_ref.shape[2]
    s_c = _dot(kc_ref[0, 0], qt)
    cmp_end = lax.broadcasted_iota(jnp.int32, (n_cmp, 1), 0) * CMP_STRIDE + (CMP_BLOCK - 1)
    vis = cmp_end <= t4
    s_c = jnp.where(vis, s_c, NEG)
    m_c = jnp.max(s_c, axis=0, keepdims=True)
    e_c = jnp.where(vis, jnp.exp2(s_c - m_c), 0.0)
    p_c = e_c / jnp.maximum(jnp.sum(e_c, axis=0, keepdims=True), 1e-30)
    o_c = _dot(vct_ref[0, 0], p_c.astype(BF16))

    n_sel = seq // SEL_BLOCK
    p_sum = sum(p_c[:, g * Q_BLOCK:(g + 1) * Q_BLOCK] for g in range(GQA_GROUP))
    jj = lax.broadcasted_iota(jnp.int32, (n_sel, n_cmp), 0) * SEL_BLOCK
    nn = lax.broadcasted_iota(jnp.int32, (n_sel, n_cmp), 1) * CMP_STRIDE
    ov = jnp.minimum(nn + CMP_BLOCK, jj + SEL_BLOCK) - jnp.maximum(nn, jj)
    ov_t = (jnp.maximum(ov, 0).astype(F32) * (1.0 / CMP_BLOCK)).astype(BF16)
    p_hi, p_lo = _split(p_sum)
    imp = _dot(ov_t, p_hi) + _dot(ov_t, p_lo)
    jb = lax.broadcasted_iota(jnp.int32, (n_sel, 1), 0)
    cur = t_row // SEL_BLOCK
    allowed = jb <= cur
    forced = (jb == 0) | (jb == cur) | (jb == cur - 1)
    score = jnp.where(forced & allowed, FORCE, jnp.where(allowed, imp, NEG))
    sc_ref[...] = score
    n_top = min(SEL_TOP, n_sel)
    sub = 8
    groups = [score[r:r + sub, :] for r in range(0, n_sel, sub)]
    ranks = [jnp.zeros((sub, Q_BLOCK), F32) for _ in groups]
    for k in range(n_sel):
        row = jnp.broadcast_to(sc_ref[k:k + 1, :], (sub, Q_BLOCK))
        for r, grp in enumerate(groups):
            if r * sub > k:
                ahead = row >= grp
            elif r * sub + sub - 1 < k:
                ahead = row > grp
            else:
                later = (r * sub + lax.broadcasted_iota(jnp.int32, (sub, 1), 0)) > k
                ahead = (row > grp) | ((row == grp) & later)
            ranks[r] = ranks[r] + jnp.where(ahead, 1.0, 0.0)
    rank = jnp.concatenate(ranks, axis=0)
    chosen = (rank < n_top) & (score > 0.5 * NEG)
    bias = jnp.where(chosen, 0.0, NEG).astype(BF16)
    pad = kall_ref.shape[3] - dh - n_sel
    if pad:
        bias = jnp.concatenate([bias, jnp.zeros((pad, Q_BLOCK), BF16)], axis=0)
    qaug_ref[...] = jnp.concatenate([qt, jnp.concatenate([bias] * GQA_GROUP, axis=1)], axis=0)

    n_s = i // 2 + 1
    n_jobs = n_s + 1 + (i >= 1).astype(jnp.int32) + (i >= 3).astype(jnp.int32)
    odd = i % 2
    sel_tiles = seq // KEY_TILE
    win_tiles = (seq + WIN_PAD_LO) // KEY_TILE

    def job(j):
        is_sel = j < n_s
        valid = (j >= 0) & (j < n_jobs)
        w = 2 - (j - n_s)
        r = i * Q_BLOCK + w * KEY_TILE
        krow = jnp.where(is_sel, j * KEY_TILE, seq + r)
        vtile = jnp.where(is_sel, j, sel_tiles + odd * win_tiles + r // KEY_TILE)
        pat_sel = jnp.where(j == n_s - 1, jnp.where(odd == 1, PAT_DIAG_SHIFT, PAT_DIAG), PAT_ZERO)
        pat_win = jnp.where(w == 2, PAT_DIAG,
                            jnp.where(w == 1, jnp.where(i >= 2, PAT_ZERO, PAT_UPPER),
                                      jnp.where(i >= 4, PAT_ANTI, PAT_UPPER)))
        pat = jnp.where(valid, jnp.where(is_sel, pat_sel, pat_win), PAT_NONE)
        return jnp.where(valid, krow, 0), jnp.where(valid, vtile, 0), pat, jnp.where(is_sel, 0, 1)

    def stage_a(j, s_ref):
        krow, _, pat, _ = job(j)
        k = kall_ref[0, 0, pl.ds(pl.multiple_of(krow, Q_BLOCK), KEY_TILE), :]
        s_ref[...] = _dot(k, qaug_ref[...]) + pat_ref[pat]

    def stage_b(j, s_ref, p_ref, slot):
        a = job(j)[3]
        s = s_ref[...]
        m_old = m_ref[a]
        m_new = jnp.maximum(m_old, jnp.max(s, axis=0, keepdims=True))
        alpha = jnp.exp2(m_old - m_new)
        p = jnp.exp2(s - m_new)
        l_ref[a] = alpha * l_ref[a] + jnp.sum(p, axis=0, keepdims=True)
        m_ref[a] = m_new
        al_ref[slot] = alpha
        p_ref[...] = p.astype(BF16)

    def stage_c(j, p_ref, slot):
        _, vtile, _, a = job(j)
        acc_ref[a] = al_ref[slot] * acc_ref[a] + _dot(vt_ref[0, 0, vtile], p_ref[...])

    s1_ref[...] = jnp.full(s1_ref.shape, NEG, F32)
    p0_ref[...] = jnp.zeros(p0_ref.shape, BF16)
    p1_ref[...] = jnp.zeros(p1_ref.shape, BF16)
    al_ref[...] = jnp.ones(al_ref.shape, F32)
    m_ref[...] = jnp.full(m_ref.shape, NEG, F32)
    l_ref[...] = jnp.zeros(l_ref.shape, F32)
    acc_ref[...] = jnp.zeros(acc_ref.shape, F32)

    def two_steps(jj, carry):
        j = 2 * jj
        stage_c(j - 2, p0_ref, 0)
        stage_b(j - 1, s1_ref, p1_ref, 1)
        stage_a(j, s0_ref)
        stage_c(j - 1, p1_ref, 1)
        stage_b(j, s0_ref, p0_ref, 0)
        stage_a(j + 1, s1_ref)
        return carry

    lax.fori_loop(0, (n_jobs + 3) // 2, two_steps, 0)
    o_s = acc_ref[0] / l_ref[0]
    o_w = acc_ref[1] / l_ref[1]

    gates = g_ref[0, 0]
    def gate_row(br):
        return jnp.concatenate([gates[br * GQA_GROUP + g:br * GQA_GROUP + g + 1, :] for g in range(GQA_GROUP)], axis=1)
    o = gate_row(0) * o_c + gate_row(1) * o_s + gate_row(2) * o_w
    o_ref[0] = jnp.concatenate([o[:, g * Q_BLOCK:(g + 1) * Q_BLOCK].T for g in range(GQA_GROUP)], axis=1).astype(BF16)


def _attention(q_t, gates_t, kc, vc_t, ks_aug, kw_aug, vs_t, vw_t):
    bsz, hkv, s, kdim = ks_aug.shape
    dh = vs_t.shape[2]
    n_cmp = kc.shape[2]
    gd = GQA_GROUP * dh
    nq = GQA_GROUP * Q_BLOCK

    k_all = jnp.concatenate([ks_aug, jnp.zeros((bsz, hkv, WIN_PAD_LO, kdim), BF16), kw_aug,
                             jnp.zeros((bsz, hkv, WIN_PAD_HI, kdim), BF16)], axis=2)

    def tiles(v):
        n = v.shape[-1] // KEY_TILE
        return v.reshape(bsz, hkv, dh, n, KEY_TILE).transpose(0, 1, 3, 2, 4)

    vw_pad = jnp.pad(vw_t, ((0, 0), (0, 0), (0, 0), (WIN_PAD_LO, WIN_PAD_HI)))
    vt_all = jnp.concatenate([tiles(vs_t), tiles(vw_pad[..., :s + WIN_PAD_LO]), tiles(vw_pad[..., Q_BLOCK:])], axis=2)
    patterns = _mask_patterns()
    n_rows, n_tiles = k_all.shape[2], vt_all.shape[2]

    return pl.pallas_call(
        functools.partial(_attn_kernel, seq=s),
        grid=(bsz, hkv, s // Q_BLOCK),
        in_specs=[
            pl.BlockSpec((1, gd, Q_BLOCK), lambda b, h, i: (b, h, i)),
            pl.BlockSpec((1, 1, 3 * GQA_GROUP, Q_BLOCK), lambda b, h, i: (b, h, 0, i)),
            pl.BlockSpec((1, 1, n_cmp, dh), lambda b, h, i: (b, h, 0, 0)),
            pl.BlockSpec((1, 1, dh, n_cmp), lambda b, h, i: (b, h, 0, 0)),
            pl.BlockSpec((1, 1, n_rows, kdim), lambda b, h, i: (b, h, 0, 0)),
            pl.BlockSpec((1, 1, n_tiles, dh, KEY_TILE), lambda b, h, i: (b, h, 0, 0, 0)),
            _const_spec(patterns.shape),
        ],
        out_specs=pl.BlockSpec((1, Q_BLOCK, gd), lambda b, h, i: (b, i, h)),
        out_shape=jax.ShapeDtypeStruct((bsz, s, hkv * gd), BF16),
        scratch_shapes=[
            pltpu.VMEM((s // SEL_BLOCK, Q_BLOCK), F32),
            pltpu.VMEM((kdim, nq), BF16),
            pltpu.VMEM((KEY_TILE, nq), F32),
            pltpu.VMEM((KEY_TILE, nq), F32),
            pltpu.VMEM((KEY_TILE, nq), BF16),
            pltpu.VMEM((KEY_TILE, nq), BF16),
            pltpu.VMEM((2, 1, nq), F32),
            pltpu.VMEM((2, 1, nq), F32),
            pltpu.VMEM((2, 1, nq), F32),
            pltpu.VMEM((2, dh, nq), F32),
        ],
        compiler_params=_params("parallel", "parallel", "arbitrary"),
        name="nsa_attention",
    )(q_t, gates_t, kc, vc_t, k_all, vt_all, patterns)


def _oproj_kernel(o_ref, x_ref, mod_ref, w_ref, g_ref, b_ref, out_ref):
    y = _dot(o_ref[0], w_ref[...])
    out_ref[0] = _layer_norm(ALPHA * x_ref[0] + mod_ref[0, 2:3, :] * y, g_ref[...], b_ref[...])


def _out_projection(o, x, mod, w_o, ln_g, ln_b):
    bsz, s, d = x.shape
    tm = 512
    return pl.pallas_call(
        _oproj_kernel,
        grid=(bsz, s // tm),
        in_specs=[
            pl.BlockSpec((1, tm, o.shape[-1]), lambda b, j: (b, j, 0)),
            pl.BlockSpec((1, tm, d), lambda b, j: (b, j, 0)),
            pl.BlockSpec((1, 6, d), lambda b, j: (b, 0, 0)),
            _const_spec(w_o.shape),
            _const_spec((1, d)),
            _const_spec((1, d)),
        ],
        out_specs=pl.BlockSpec((1, tm, d), lambda b, j: (b, j, 0)),
        out_shape=jax.ShapeDtypeStruct((bsz, s, d), F32),
        compiler_params=_params("parallel", "parallel"),
        name="nsa_out_proj",
    )(o, x, mod, w_o, ln_g.reshape(1, d), ln_b.reshape(1, d))


def _moe_block(x, mod, router_wt, router_bias, w_gu, w_down, ln_g, ln_b):
    bsz, s, d = x.shape
    xf = x.reshape(bsz * s, d)
    comb = _router(xf, mod, router_wt, router_bias, s)
    out = _moe_layer(xf, mod, comb, w_gu, w_down, ln_g, ln_b, s)
    return out.reshape(bsz, s, d)


def _nsa_layer(x, mod, w_kv, cmp_pe, cmp_w1, cmp_b1, cmp_w2, cmp_b2, w_qg, b_g, w_o, ln_g, ln_b):
    bsz, s, d = x.shape
    hkv, dh, grp = N_KV_HEADS, HEAD_DIM, GQA_GROUP
    hw = hkv * dh
    kvw = w_kv.reshape(d, 6, hw)
    w_nat = jnp.concatenate([kvw[:, 0], kvw[:, 1], kvw[:, 2], kvw[:, 4]], axis=1).astype(BF16)
    w_vt = jnp.concatenate([kvw[:, 3], kvw[:, 5]], axis=1).T.astype(BF16)
    nq = N_HEADS * dh
    wg = w_qg[:, nq:].reshape(d, hkv, grp, 3).transpose(0, 1, 3, 2).reshape(d, 3 * N_HEADS)
    w_qgt = jnp.concatenate([w_qg[:, :nq], wg], axis=1).T.astype(BF16)
    bg = b_g.reshape(hkv, grp, 3).transpose(0, 2, 1).reshape(3 * N_HEADS, 1)
    kc, vc, ks_aug, kw_aug, vs_t, vw_t, q_t, gates_t = _projections(x, mod, w_nat, w_vt, w_qgt, bg)
    n_chunk = s // CMP_STRIDE
    kv_c = jnp.stack([kc, vc]).reshape(2, bsz, hkv, n_chunk, CMP_STRIDE * dh)
    cmp_nat, cmp_t = _compress(kv_c, cmp_pe.reshape(2, 1, CMP_BLOCK * dh), cmp_w1.astype(BF16),
                               cmp_b1.reshape(2, 1, CMP_HIDDEN), cmp_w2.astype(BF16), cmp_b2.reshape(2, 1, dh))
    o = _attention(q_t, gates_t, cmp_nat[0], cmp_t[1], ks_aug, kw_aug, vs_t, vw_t)
    return _out_projection(o, x, mod, w_o.astype(BF16), ln_g, ln_b)


def kernel(x, c, ada_w, ada_b, ln_g, ln_b, conv_w_in, conv_w, conv_b, conv_w_out, w_kv, cmp_pe, cmp_w1, cmp_b1, cmp_w2, cmp_b2, w_qg, b_g, w_o, router_w, router_bias, w_gu, w_down):
    bsz, s, d = x.shape
    mod = _modulation(c, ada_w, ada_b).reshape(DEPTH, bsz, 6, d)
    router_wt = router_w.T
    w_gu_b = w_gu.astype(BF16)
    w_down_b = w_down.astype(BF16)

    x = _conv_layer(x, mod[0], conv_w_in[0].astype(BF16), conv_w[0], conv_b[0], conv_w_out[0].astype(BF16),
                    ln_g[0, 0], ln_b[0, 0])
    x = _moe_block(x, mod[0], router_wt, router_bias, w_gu_b[0], w_down_b[0], ln_g[0, 1], ln_b[0, 1])

    x = _nsa_layer(x, mod[1], w_kv, cmp_pe, cmp_w1, cmp_b1, cmp_w2, cmp_b2, w_qg[0], b_g[0], w_o[0],
                   ln_g[1, 0], ln_b[1, 0])
    x = _moe_block(x, mod[1], router_wt, router_bias, w_gu_b[1], w_down_b[1], ln_g[1, 1], ln_b[1, 1])
    return x
```

```python
import functools
import math

import jax
import jax.numpy as jnp
import numpy as np
from jax import lax
from jax.experimental import pallas as pl
from jax.experimental.pallas import tpu as pltpu

F32 = jnp.float32
BF16 = jnp.bfloat16

D_MODEL = 1024
DEPTH = 2
N_A_LAYERS = DEPTH // 2
CONV_WIDTH = 3
N_HEADS = 16
HEAD_DIM = D_MODEL // N_HEADS
N_KV_HEADS = 4
GQA_GROUP = N_HEADS // N_KV_HEADS
CMP_BLOCK = 32
CMP_STRIDE = 16
CMP_HIDDEN = 4 * HEAD_DIM
SEL_BLOCK = 64
SEL_TOP = 16
WINDOW = 512
Q_BLOCK = 128
N_EXPERTS = 16
N_GROUPS = 4
EXPERTS_PER_GROUP = N_EXPERTS // N_GROUPS
TOP_K = 2
D_FF_EXPERT = D_MODEL // 2
ALPHA = (2 * DEPTH) ** 0.25
LN_EPS = 1e-5
NEG = -1e30
FORCE = 1e9

LANES = 128
VMEM_LIMIT_BYTES = 56 * 1024 * 1024

LOG2E = math.log2(math.e)

KEY_TILE = 256
WIN_PAD_LO = WINDOW
WIN_PAD_HI = Q_BLOCK
PAT_ZERO, PAT_DIAG, PAT_DIAG_SHIFT, PAT_ANTI, PAT_UPPER, PAT_NONE = range(6)


def _dot(a, b):
    return jnp.dot(a, b, preferred_element_type=F32)


def _dot_nt(a, b):
    return lax.dot_general(a, b, (((1,), (1,)), ((), ())), preferred_element_type=F32)


def _split(x):
    hi = x.astype(BF16)
    lo = (x - hi.astype(F32)).astype(BF16)
    return hi, lo


def _dot3(a, b):
    ah, al = _split(a)
    bh, bl = _split(b)
    return _dot(ah, bh) + (_dot(ah, bl) + _dot(al, bh))


def _dot3_nt(a, b):
    ah, al = _split(a)
    bh, bl = _split(b)
    return _dot_nt(ah, bh) + (_dot_nt(ah, bl) + _dot_nt(al, bh))


def _layer_norm(r, g, b):
    mu = jnp.mean(r, axis=-1, keepdims=True)
    d = r - mu
    var = jnp.mean(d * d, axis=-1, keepdims=True)
    return d * lax.rsqrt(var + LN_EPS) * g + b


def _params(*sem):
    return pltpu.CompilerParams(dimension_semantics=sem, vmem_limit_bytes=VMEM_LIMIT_BYTES)


def _const_spec(shape):
    zeros = (0,) * len(shape)
    return pl.BlockSpec(shape, lambda *_: zeros)


def _mod_kernel(c_ref, w_ref, b_ref, o_ref):
    c = c_ref[...]
    s = c * jax.nn.sigmoid(c)
    o_ref[0] = _dot3(s, w_ref[0]) + b_ref[0]


def _modulation(c, ada_w, ada_b):
    depth, d, n = ada_w.shape
    bsz = c.shape[0]
    tn = 1536
    return pl.pallas_call(
        _mod_kernel,
        grid=(depth, n // tn),
        in_specs=[
            pl.BlockSpec((bsz, d), lambda l, j: (0, 0)),
            pl.BlockSpec((1, d, tn), lambda l, j: (l, 0, j)),
            pl.BlockSpec((1, 1, tn), lambda l, j: (l, 0, j)),
        ],
        out_specs=pl.BlockSpec((1, bsz, tn), lambda l, j: (l, 0, j)),
        out_shape=jax.ShapeDtypeStruct((depth, bsz, n), F32),
        compiler_params=_params("parallel", "parallel"),
        name="adaln_mod",
    )(c, ada_w, ada_b.reshape(depth, 1, n))


CONV_HALO = 8


def _conv_kernel(x_ref, mod_ref, win_ref, cw_ref, cb_ref, wout_ref, g_ref, b_ref, o_ref, z_ref):
    tm = x_ref.shape[1]
    d = x_ref.shape[2]

    @pl.when(pl.program_id(1) == 0)
    def _():
        z_ref[0:CONV_HALO, :] = jnp.zeros((CONV_HALO, d), F32)

    x = x_ref[0]
    sh = mod_ref[0, 0:1, :]
    sc = mod_ref[0, 1:2, :]
    gate = mod_ref[0, 2:3, :]
    u = (x * (1.0 + sc) + sh).astype(BF16)
    bch = _dot(u, win_ref[...])
    z = bch[:, d:2 * d] * bch[:, 2 * d:]
    z_ref[CONV_HALO:CONV_HALO + tm, :] = z
    z1 = z_ref[CONV_HALO - 1:CONV_HALO - 1 + tm, :]
    z2 = z_ref[CONV_HALO - 2:CONV_HALO - 2 + tm, :]
    conv = cw_ref[0:1, :] * z2 + cw_ref[1:2, :] * z1 + cw_ref[2:3, :] * z + cb_ref[...]
    v = (bch[:, :d] * conv).astype(BF16)
    y = _dot(v, wout_ref[...])
    o_ref[0] = _layer_norm(ALPHA * x + gate * y, g_ref[...], b_ref[...])
    z_ref[0:CONV_HALO, :] = z_ref[tm:tm + CONV_HALO, :]


def _conv_layer(x, mod, w_in, conv_w, conv_b, w_out, ln_g, ln_b):
    bsz, s, d = x.shape
    tm = 512
    return pl.pallas_call(
        _conv_kernel,
        grid=(bsz, s // tm),
        in_specs=[
            pl.BlockSpec((1, tm, d), lambda b, j: (b, j, 0)),
            pl.BlockSpec((1, 6, d), lambda b, j: (b, 0, 0)),
            _const_spec((d, 3 * d)),
            _const_spec((CONV_WIDTH, d)),
            _const_spec((1, d)),
            _const_spec((d, d)),
            _const_spec((1, d)),
            _const_spec((1, d)),
        ],
        out_specs=pl.BlockSpec((1, tm, d), lambda b, j: (b, j, 0)),
        out_shape=jax.ShapeDtypeStruct((bsz, s, d), F32),
        scratch_shapes=[pltpu.VMEM((tm + CONV_HALO, d), F32)],
        compiler_params=_params("arbitrary", "arbitrary"),
        name="conv_mixer",
    )(x, mod, w_in, conv_w, conv_b.reshape(1, d), w_out, ln_g.reshape(1, d), ln_b.reshape(1, d))


def _router_kernel(x_ref, mod_ref, rwt_ref, rb_ref, wcol_ref, route_ref, count_ref, base_ref, tri_ref):
    tm = x_ref.shape[0]
    x = x_ref[...]
    u = x * (1.0 + mod_ref[0, 4:5, :]) + mod_ref[0, 3:4, :]
    logits = _dot3_nt(rwt_ref[...], u)
    m = jnp.max(logits, axis=0, keepdims=True)
    e = jnp.exp(logits - m)
    aff = e / jnp.sum(e, axis=0, keepdims=True)
    biased = aff + rb_ref[...]
    aff_r = [aff[i:i + 1, :] for i in range(N_EXPERTS)]
    row = [biased[i:i + 1, :] for i in range(N_EXPERTS)]

    best_s, best = None, None
    for g in range(N_GROUPS):
        r = row[g * EXPERTS_PER_GROUP:(g + 1) * EXPERTS_PER_GROUP]
        gs = None
        for i in range(EXPERTS_PER_GROUP):
            for j in range(i + 1, EXPERTS_PER_GROUP):
                p = r[i] + r[j]
                gs = p if gs is None else jnp.maximum(gs, p)
        if g == 0:
            best_s, best = gs, jnp.zeros((1, tm), jnp.int32)
        else:
            upd = gs > best_s
            best = jnp.where(upd, g, best)
            best_s = jnp.where(upd, gs, best_s)

    masked = [jnp.where(best == (i // EXPERTS_PER_GROUP), row[i], NEG) for i in range(N_EXPERTS)]

    def first_argmax(vals):
        v, idx = vals[0], jnp.zeros((1, tm), jnp.int32)
        for i in range(1, N_EXPERTS):
            upd = vals[i] > v
            idx = jnp.where(upd, i, idx)
            v = jnp.where(upd, vals[i], v)
        return idx

    i0 = first_argmax(masked)
    i1 = first_argmax([jnp.where(i0 == i, -jnp.inf, masked[i]) for i in range(N_EXPERTS)])
    w0 = sum(jnp.where(i0 == i, aff_r[i], 0.0) for i in range(N_EXPERTS))
    w1 = sum(jnp.where(i1 == i, aff_r[i], 0.0) for i in range(N_EXPERTS))
    tot = w0 + w1
    w0 = w0 / tot
    w1 = w1 / tot
    w_t = jnp.concatenate([w0, w1, jnp.zeros((LANES - TOP_K, tm), F32)], axis=0)
    wcol_ref[...] = w_t.T

    @pl.when(pl.program_id(0) == 0)
    def _():
        base_ref[...] = jnp.zeros(base_ref.shape, F32)
        r_i = lax.broadcasted_iota(jnp.int32, tri_ref.shape, 0)
        c_i = lax.broadcasted_iota(jnp.int32, tri_ref.shape, 1)
        tri_ref[...] = jnp.where(r_i <= c_i, 1.0, 0.0).astype(BF16)

    hot = jnp.concatenate([jnp.where((i0 == i) | (i1 == i), 1.0, 0.0) for i in range(N_EXPERTS)], axis=0)
    incl = _dot(hot.astype(BF16), tri_ref[...])
    pos = base_ref[...] + (incl - hot)
    rank0 = sum(jnp.where(i0 == i, pos[i:i + 1, :], 0.0) for i in range(N_EXPERTS))
    rank1 = sum(jnp.where(i1 == i, pos[i:i + 1, :], 0.0) for i in range(N_EXPERTS))
    base = base_ref[...] + jnp.sum(hot, axis=1, keepdims=True)
    base_ref[...] = base
    route_ref[...] = jnp.concatenate(
        [i0, i1, rank0.astype(jnp.int32), rank1.astype(jnp.int32), jnp.zeros((4, tm), jnp.int32)], axis=0)
    count_ref[...] = jnp.broadcast_to(base, count_ref.shape).astype(jnp.int32)


def _router(x, mod, router_wt, router_bias, seq):
    t, d = x.shape
    tm = 1024
    return pl.pallas_call(
        _router_kernel,
        grid=(t // tm,),
        in_specs=[
            pl.BlockSpec((tm, d), lambda i: (i, 0)),
            pl.BlockSpec((1, 6, d), lambda i: ((i * tm) // seq, 0, 0)),
            _const_spec((N_EXPERTS, d)),
            _const_spec((N_EXPERTS, 1)),
        ],
        out_specs=(
            pl.BlockSpec((tm, LANES), lambda i: (i, 0)),
            pl.BlockSpec((8, tm), lambda i: (0, i)),
            _const_spec((N_EXPERTS, LANES)),
        ),
        out_shape=(
            jax.ShapeDtypeStruct((t, LANES), F32),
            jax.ShapeDtypeStruct((8, t), jnp.int32),
            jax.ShapeDtypeStruct((N_EXPERTS, LANES), jnp.int32),
        ),
        scratch_shapes=[pltpu.VMEM((N_EXPERTS, 1), F32), pltpu.VMEM((tm, tm), BF16)],
        compiler_params=_params("arbitrary"),
        name="moe_router",
    )(x, mod, router_wt, router_bias.reshape(N_EXPERTS, 1))


EXPERT_TILE = 512
TOKEN_TILE = 512


ROW_COPY_UNROLL = 8


def _row_copies(tm, row_copy, tile_copy):
    def start(t, c):
        for k in range(TOP_K):
            row_copy(t, k).start()
        return c

    lax.fori_loop(0, tm, start, 0, unroll=ROW_COPY_UNROLL)
    for k in range(TOP_K):
        tile_copy(k).wait()


def _dispatch_kernel(pad_ref, idx_ref, x_ref, mod_ref, xs_ref, u_ref, sem):
    tm = x_ref.shape[0]

    @pl.when(pl.program_id(0) == 0)
    def _():
        u_ref[...] = jnp.zeros(u_ref.shape, F32)
        for wait in (False, True):
            for e in range(pad_ref.shape[0]):
                @pl.when(pad_ref[e] >= 0)
                def _():
                    row = pl.multiple_of(jnp.maximum(pad_ref[e], 0), tm)
                    fill = pltpu.make_async_copy(u_ref, xs_ref.at[pl.ds(row, tm)], sem)
                    fill.wait() if wait else fill.start()

    u_ref[...] = x_ref[...] * (1.0 + mod_ref[0, 4:5, :]) + mod_ref[0, 3:4, :]

    def row_copy(t, k):
        dst = idx_ref[0, 0, k * tm + t]
        return pltpu.make_async_copy(u_ref.at[pl.ds(t, 1)], xs_ref.at[pl.ds(dst, 1)], sem)

    _row_copies(tm, row_copy, lambda k: pltpu.make_async_copy(u_ref, xs_ref.at[pl.ds(0, tm)], sem))


def _dispatch(pad_start, idx, x, mod, n_rows, seq):
    t, d = x.shape
    tm = TOKEN_TILE
    assert tm == EXPERT_TILE
    return pl.pallas_call(
        _dispatch_kernel,
        grid_spec=pltpu.PrefetchScalarGridSpec(
            num_scalar_prefetch=1,
            grid=(t // tm,),
            in_specs=[
                pl.BlockSpec((1, 1, TOP_K * tm), lambda i, pad: (i, 0, 0), memory_space=pltpu.SMEM),
                pl.BlockSpec((tm, d), lambda i, pad: (i, 0)),
                pl.BlockSpec((1, 6, d), lambda i, pad: ((i * tm) // seq, 0, 0)),
            ],
            out_specs=pl.BlockSpec(memory_space=pl.ANY),
            scratch_shapes=[pltpu.VMEM((tm, d), F32), pltpu.SemaphoreType.DMA],
        ),
        out_shape=jax.ShapeDtypeStruct((n_rows, d), F32),
        compiler_params=_params("arbitrary"),
        name="moe_dispatch",
    )(pad_start, idx, x, mod)


def _expert_kernel(te_ref, nu_ref, xs_ref, wgu_ref, wdn_ref, ys_ref):
    del te_ref
    used = pl.program_id(0) < nu_ref[0]

    @pl.when(used)
    def _():
        f = wdn_ref.shape[1]
        h = _dot(xs_ref[...].astype(BF16), wgu_ref[0])
        gate = h[:, :f]
        a = (gate * jax.nn.sigmoid(gate) * h[:, f:]).astype(BF16)
        ys_ref[...] = _dot(a, wdn_ref[0])

    @pl.when(jnp.logical_not(used))
    def _():
        ys_ref[...] = jnp.zeros(ys_ref.shape, F32)


def _experts(tile_expert, n_used, xs, w_gu, w_down):
    p, d = xs.shape
    _, _, f2 = w_gu.shape
    f = f2 // 2
    row_map = lambda s, te, nu: (s, 0)
    return pl.pallas_call(
        _expert_kernel,
        grid_spec=pltpu.PrefetchScalarGridSpec(
            num_scalar_prefetch=2,
            grid=(tile_expert.shape[0],),
            in_specs=[
                pl.BlockSpec((EXPERT_TILE, d), row_map),
                pl.BlockSpec((1, d, f2), lambda s, te, nu: (te[s], 0, 0)),
                pl.BlockSpec((1, f, d), lambda s, te, nu: (te[s], 0, 0)),
            ],
            out_specs=pl.BlockSpec((EXPERT_TILE, d), row_map),
        ),
        out_shape=jax.ShapeDtypeStruct((p, d), F32),
        compiler_params=_params("arbitrary"),
        name="moe_experts",
    )(tile_expert, n_used, xs, w_gu, w_down)


def _combine_kernel(idx_ref, x_ref, mod_ref, w_ref, ys_ref, g_ref, b_ref, o_ref, y_ref, sem):
    tm = x_ref.shape[0]

    def row_copy(t, k):
        src = idx_ref[0, 0, k * tm + t]
        return pltpu.make_async_copy(ys_ref.at[pl.ds(src, 1)], y_ref.at[k, pl.ds(t, 1)], sem)

    _row_copies(tm, row_copy, lambda k: pltpu.make_async_copy(ys_ref.at[pl.ds(0, tm)], y_ref.at[k], sem))
    w = w_ref[...]
    out = w[:, 0:1] * y_ref[0] + w[:, 1:2] * y_ref[1]
    r = ALPHA * x_ref[...] + mod_ref[0, 5:6, :] * out
    o_ref[...] = _layer_norm(r, g_ref[...], b_ref[...])


def _combine(idx, x, mod, wcol, ys, ln_g, ln_b, seq):
    t, d = x.shape
    tm = TOKEN_TILE
    return pl.pallas_call(
        _combine_kernel,
        grid=(t // tm,),
        in_specs=[
            pl.BlockSpec((1, 1, TOP_K * tm), lambda i: (i, 0, 0), memory_space=pltpu.SMEM),
            pl.BlockSpec((tm, d), lambda i: (i, 0)),
            pl.BlockSpec((1, 6, d), lambda i: ((i * tm) // seq, 0, 0)),
            pl.BlockSpec((tm, LANES), lambda i: (i, 0)),
            pl.BlockSpec(memory_space=pl.ANY),
            _const_spec((1, d)),
            _const_spec((1, d)),
        ],
        out_specs=pl.BlockSpec((tm, d), lambda i: (i, 0)),
        out_shape=jax.ShapeDtypeStruct((t, d), F32),
        scratch_shapes=[pltpu.VMEM((TOP_K, tm, d), F32), pltpu.SemaphoreType.DMA],
        compiler_params=_params("arbitrary"),
        name="moe_combine",
    )(idx, x, mod, wcol, ys, ln_g.reshape(1, d), ln_b.reshape(1, d))


def _proj_kernel(x_ref, mod_ref, wn_ref, wvt_ref, wqt_ref, bg_ref,
                 kc_ref, vc_ref, ks_ref, kw_ref, vst_ref, vwt_ref, qt_ref, gt_ref):
    tm = x_ref.shape[1]
    j = pl.program_id(1)
    x = x_ref[0]
    xb = x.astype(BF16)
    nat = _dot(xb, wn_ref[...])
    vt = _dot_nt(wvt_ref[...], xb)
    u = (x * (1.0 + mod_ref[0, 1:2, :]) + mod_ref[0, 0:1, :]).astype(BF16)
    qg = _dot_nt(wqt_ref[...], u)
    nq = N_HEADS * HEAD_DIM
    qt_ref[0] = (qg[:nq, :] * (HEAD_DIM ** -0.5 * LOG2E)).astype(BF16)
    gates = jax.nn.sigmoid(qg[nq:, :] + bg_ref[...])
    per = 3 * GQA_GROUP
    blk = (j * tm + lax.broadcasted_iota(jnp.int32, (tm, SEL_BLOCK), 0)) // SEL_BLOCK
    onehot = (blk == lax.broadcasted_iota(jnp.int32, (tm, SEL_BLOCK), 1)).astype(BF16)
    zeros = jnp.zeros((tm, SEL_BLOCK), BF16)
    hw = N_KV_HEADS * HEAD_DIM
    for h in range(N_KV_HEADS):
        c0 = h * HEAD_DIM
        kc_ref[0, h] = nat[:, c0:c0 + HEAD_DIM]
        vc_ref[0, h] = nat[:, hw + c0:hw + c0 + HEAD_DIM]
        ks = nat[:, 2 * hw + c0:2 * hw + c0 + HEAD_DIM].astype(BF16)
        ks_ref[0, h] = jnp.concatenate([ks, onehot], axis=1)
        kw = nat[:, 3 * hw + c0:3 * hw + c0 + HEAD_DIM].astype(BF16)
        kw_ref[0, h] = jnp.concatenate([kw, zeros], axis=1)
        vst_ref[0, h] = vt[c0:c0 + HEAD_DIM, :].astype(BF16)
        vwt_ref[0, h] = vt[hw + c0:hw + c0 + HEAD_DIM, :].astype(BF16)
        gt_ref[0, h] = gates[h * per:(h + 1) * per, :]


def _projections(x, mod, w_nat, w_vt, w_qgt, b_g):
    bsz, s, d = x.shape
    tm = 512
    hkv, dh = N_KV_HEADS, HEAD_DIM
    per = 3 * GQA_GROUP
    nqg = w_qgt.shape[0]
    out_shape = (
        jax.ShapeDtypeStruct((bsz, hkv, s, dh), F32),
        jax.ShapeDtypeStruct((bsz, hkv, s, dh), F32),
        jax.ShapeDtypeStruct((bsz, hkv, s, 2 * dh), BF16),
        jax.ShapeDtypeStruct((bsz, hkv, s, 2 * dh), BF16),
        jax.ShapeDtypeStruct((bsz, hkv, dh, s), BF16),
        jax.ShapeDtypeStruct((bsz, hkv, dh, s), BF16),
        jax.ShapeDtypeStruct((bsz, N_HEADS * dh, s), BF16),
        jax.ShapeDtypeStruct((bsz, hkv, per, s), F32),
    )
    out_specs = (
        pl.BlockSpec((1, hkv, tm, dh), lambda b, j: (b, 0, j, 0)),
        pl.BlockSpec((1, hkv, tm, dh), lambda b, j: (b, 0, j, 0)),
        pl.BlockSpec((1, hkv, tm, 2 * dh), lambda b, j: (b, 0, j, 0)),
        pl.BlockSpec((1, hkv, tm, 2 * dh), lambda b, j: (b, 0, j, 0)),
        pl.BlockSpec((1, hkv, dh, tm), lambda b, j: (b, 0, 0, j)),
        pl.BlockSpec((1, hkv, dh, tm), lambda b, j: (b, 0, 0, j)),
        pl.BlockSpec((1, N_HEADS * dh, tm), lambda b, j: (b, 0, j)),
        pl.BlockSpec((1, hkv, per, tm), lambda b, j: (b, 0, 0, j)),
    )
    return pl.pallas_call(
        _proj_kernel,
        grid=(bsz, s // tm),
        in_specs=[
            pl.BlockSpec((1, tm, d), lambda b, j: (b, j, 0)),
            pl.BlockSpec((1, 6, d), lambda b, j: (b, 0, 0)),
            _const_spec(w_nat.shape),
            _const_spec(w_vt.shape),
            _const_spec(w_qgt.shape),
            _const_spec((nqg - N_HEADS * dh, 1)),
        ],
        out_specs=out_specs,
        out_shape=out_shape,
        compiler_params=_params("parallel", "parallel"),
        name="nsa_projections",
    )(x, mod, w_nat, w_vt, w_qgt, b_g)


def _compress_kernel(c_ref, pe_ref, w1_ref, b1_ref, w2_ref, b2_ref, nat_ref, tr_ref):
    half = c_ref.shape[-1]
    c = c_ref[0, 0, 0]
    lo = (c + pe_ref[0, :, :half]).astype(BF16)
    hi = (c + pe_ref[0, :, half:]).astype(BF16)
    p = _dot(lo, w1_ref[0, :half, :])
    q = _dot(hi, w1_ref[0, half:, :])
    n = c.shape[0]
    pre = p + pltpu.roll(q, n - 1, 0) + b1_ref[0]
    hdn = 0.5 * pre * (1.0 + jnp.tanh(0.7978845608028654 * (pre + 0.044715 * (pre * pre * pre))))
    out = _dot(hdn.astype(BF16), w2_ref[0]) + b2_ref[0]
    nat_ref[0, 0, 0] = out.astype(BF16)
    tr_ref[0, 0, 0] = out.T.astype(BF16)


def _compress(kv_c, pe, w1, b1, w2, b2):
    two, bsz, hkv, n, width = kv_c.shape
    hid = w1.shape[-1]
    dh = w2.shape[-1]
    return pl.pallas_call(
        _compress_kernel,
        grid=(two, bsz, hkv),
        in_specs=[
            pl.BlockSpec((1, 1, 1, n, width), lambda i, b, h: (i, b, h, 0, 0)),
            pl.BlockSpec((1, 1, 2 * width), lambda i, b, h: (i, 0, 0)),
            pl.BlockSpec((1, 2 * width, hid), lambda i, b, h: (i, 0, 0)),
            pl.BlockSpec((1, 1, hid), lambda i, b, h: (i, 0, 0)),
            pl.BlockSpec((1, hid, dh), lambda i, b, h: (i, 0, 0)),
            pl.BlockSpec((1, 1, dh), lambda i, b, h: (i, 0, 0)),
        ],
        out_specs=(
            pl.BlockSpec((1, 1, 1, n, dh), lambda i, b, h: (i, b, h, 0, 0)),
            pl.BlockSpec((1, 1, 1, dh, n), lambda i, b, h: (i, b, h, 0, 0)),
        ),
        out_shape=(
            jax.ShapeDtypeStruct((two, bsz, hkv, n, dh), BF16),
            jax.ShapeDtypeStruct((two, bsz, hkv, dh, n), BF16),
        ),
        compiler_params=_params("parallel", "parallel", "parallel"),
        name="nsa_compress",
    )(kv_c, pe, w1, b1, w2, b2)


def _mask_patterns():
    keyl = np.arange(KEY_TILE)[:, None]
    ql = (np.arange(GQA_GROUP * Q_BLOCK) % Q_BLOCK)[None, :]
    true = np.ones((KEY_TILE, GQA_GROUP * Q_BLOCK), bool)
    valid = np.stack([
        true,
        keyl <= ql,
        keyl <= ql + Q_BLOCK,
        keyl > ql,
        (keyl >= Q_BLOCK) & true,
        ~true,
    ])
    return jnp.asarray(np.where(valid, 0.0, NEG), F32)


def _attn_kernel(q_ref, g_ref, kc_ref, vct_ref, kall_ref, vt_ref, pat_ref, o_ref,
                 sc_ref, qaug_ref, s0_ref, s1_ref, p0_ref, p1_ref, al_ref, m_ref, l_ref, acc_ref, *, seq):
    i = pl.program_id(2)
    nq = GQA_GROUP * Q_BLOCK
    dh = HEAD_DIM
    qa = q_ref[0]
    qt = jnp.concatenate([qa[g * dh:(g + 1) * dh, :] for g in range(GQA_GROUP)], axis=1)
    t_row = i * Q_BLOCK + lax.broadcasted_iota(jnp.int32, (1, Q_BLOCK), 1)
    t4 = jnp.concatenate([t_row] * GQA_GROUP, axis=1)

    n_cmp = kc_ref.shape[2]
    s_c = _dot(kc_ref[0, 0], qt)
    cmp_end = lax.broadcasted_iota(jnp.int32, (n_cmp, 1), 0) * CMP_STRIDE + (CMP_BLOCK - 1)
    vis = cmp_end <= t4
    s_c = jnp.where(vis, s_c, NEG)
    m_c = jnp.max(s_c, axis=0, keepdims=True)
    e_c = jnp.where(vis, jnp.exp2(s_c - m_c), 0.0)
    p_c = e_c / jnp.maximum(jnp.sum(e_c, axis=0, keepdims=True), 1e-30)
    o_c = _dot(vct_ref[0, 0], p_c.astype(BF16))

    n_sel = seq // SEL_BLOCK
    p_sum = sum(p_c[:, g * Q_BLOCK:(g + 1) * Q_BLOCK] for g in range(GQA_GROUP))
    jj = lax.broadcasted_iota(jnp.int32, (n_sel, n_cmp), 0) * SEL_BLOCK
    nn = lax.broadcasted_iota(jnp.int32, (n_sel, n_cmp), 1) * CMP_STRIDE
    ov = jnp.minimum(nn + CMP_BLOCK, jj + SEL_BLOCK) - jnp.maximum(nn, jj)
    ov_t = (jnp.maximum(ov, 0).astype(F32) * (1.0 / CMP_BLOCK)).astype(BF16)
    p_hi, p_lo = _split(p_sum)
    imp = _dot(ov_t, p_hi) + _dot(ov_t, p_lo)
    jb = lax.broadcasted_iota(jnp.int32, (n_sel, 1), 0)
    cur = t_row // SEL_BLOCK
    allowed = jb <= cur
    forced = (jb == 0) | (jb == cur) | (jb == cur - 1)
    score = jnp.where(forced & allowed, FORCE, jnp.where(allowed, imp, NEG))
    sc_ref[...] = score
    n_top = min(SEL_TOP, n_sel)
    sub = 8
    groups = [score[r:r + sub, :] for r in range(0, n_sel, sub)]
    ranks = [jnp.zeros((sub, Q_BLOCK), F32) for _ in groups]
    for k in range(n_sel):
        row = jnp.broadcast_to(sc_ref[k:k + 1, :], (sub, Q_BLOCK))
        for r, grp in enumerate(groups):
            if r * sub > k:
                ahead = row >= grp
            elif r * sub + sub - 1 < k:
                ahead = row > grp
            else:
                later = (r * sub + lax.broadcasted_iota(jnp.int32, (sub, 1), 0)) > k
                ahead = (row > grp) | ((row == grp) & later)
            ranks[r] = ranks[r] + jnp.where(ahead, 1.0, 0.0)
    rank = jnp.concatenate(ranks, axis=0)
    chosen = (rank < n_top) & (score > 0.5 * NEG)
    bias = jnp.where(chosen, 0.0, NEG).astype(BF16)
    pad = kall_ref.shape[3] - dh - n_sel
    if pad:
        bias = jnp.concatenate([bias, jnp.zeros((pad, Q_BLOCK), BF16)], axis=0)
    qaug_ref[...] = jnp.concatenate([qt, jnp.concatenate([bias] * GQA_GROUP, axis=1)], axis=0)

    n_s = i // 2 + 1
    n_jobs = n_s + 1 + (i >= 1).astype(jnp.int32) + (i >= 3).astype(jnp.int32)
    odd = i % 2
    sel_tiles = seq // KEY_TILE
    win_tiles = (seq + WIN_PAD_LO) // KEY_TILE

    def job(j):
        is_sel = j < n_s
        valid = (j >= 0) & (j < n_jobs)
        w = 2 - (j - n_s)
        r = i * Q_BLOCK + w * KEY_TILE
        krow = jnp.where(is_sel, j * KEY_TILE, seq + r)
        vtile = jnp.where(is_sel, j, sel_tiles + odd * win_tiles + r // KEY_TILE)
        pat_sel = jnp.where(j == n_s - 1, jnp.where(odd == 1, PAT_DIAG_SHIFT, PAT_DIAG), PAT_ZERO)
        pat_win = jnp.where(w == 2, PAT_DIAG,
                            jnp.where(w == 1, jnp.where(i >= 2, PAT_ZERO, PAT_UPPER),
                                      jnp.where(i >= 4, PAT_ANTI, PAT_UPPER)))
        pat = jnp.where(valid, jnp.where(is_sel, pat_sel, pat_win), PAT_NONE)
        return jnp.where(valid, krow, 0), jnp.where(valid, vtile, 0), pat, jnp.where(is_sel, 0, 1)

    def stage_a(j, s_ref):
        krow, _, pat, _ = job(j)
        k = kall_ref[0, 0, pl.ds(pl.multiple_of(krow, Q_BLOCK), KEY_TILE), :]
        s_ref[...] = _dot(k, qaug_ref[...]) + pat_ref[pat]

    def stage_b(j, s_ref, p_ref, slot):
        a = job(j)[3]
        s = s_ref[...]
        m_old = m_ref[a]
        m_new = jnp.maximum(m_old, jnp.max(s, axis=0, keepdims=True))
        alpha = jnp.exp2(m_old - m_new)
        p = jnp.exp2(s - m_new)
        l_ref[a] = alpha * l_ref[a] + jnp.sum(p, axis=0, keepdims=True)
        m_ref[a] = m_new
        al_ref[slot] = alpha
        p_ref[...] = p.astype(BF16)

    def stage_c(j, p_ref, slot):
        _, vtile, _, a = job(j)
        acc_ref[a] = al_ref[slot] * acc_ref[a] + _dot(vt_ref[0, 0, vtile], p_ref[...])

    s1_ref[...] = jnp.full(s1_ref.shape, NEG, F32)
    p0_ref[...] = jnp.zeros(p0_ref.shape, BF16)
    p1_ref[...] = jnp.zeros(p1_ref.shape, BF16)
    al_ref[...] = jnp.ones(al_ref.shape, F32)
    m_ref[...] = jnp.full(m_ref.shape, NEG, F32)
    l_ref[...] = jnp.zeros(l_ref.shape, F32)
    acc_ref[...] = jnp.zeros(acc_ref.shape, F32)

    def two_steps(jj, carry):
        j = 2 * jj
        stage_c(j - 2, p0_ref, 0)
        stage_b(j - 1, s1_ref, p1_ref, 1)
        stage_a(j, s0_ref)
        stage_c(j - 1, p1_ref, 1)
        stage_b(j, s0_ref, p0_ref, 0)
        stage_a(j + 1, s1_ref)
        return carry

    lax.fori_loop(0, (n_jobs + 3) // 2, two_steps, 0)
    o_s = acc_ref[0] / l_ref[0]
    o_w = acc_ref[1] / l_ref[1]

    gates = g_ref[0, 0]
    def gate_row(br):
        return jnp.concatenate([gates[br * GQA_GROUP + g:br * GQA_GROUP + g + 1, :] for g in range(GQA_GROUP)], axis=1)
    o = gate_row(0) * o_c + gate_row(1) * o_s + gate_row(2) * o_w
    o_ref[0] = jnp.concatenate([o[:, g * Q_BLOCK:(g + 1) * Q_BLOCK].T for g in range(GQA_GROUP)], axis=1).astype(BF16)


def _attention(q_t, gates_t, kc, vc_t, ks_aug, kw_aug, vs_t, vw_t):
    bsz, hkv, s, kdim = ks_aug.shape
    dh = vs_t.shape[2]
    n_cmp = kc.shape[2]
    gd = GQA_GROUP * dh
    nq = GQA_GROUP * Q_BLOCK

    k_all = jnp.concatenate([ks_aug, jnp.zeros((bsz, hkv, WIN_PAD_LO, kdim), BF16), kw_aug,
                             jnp.zeros((bsz, hkv, WIN_PAD_HI, kdim), BF16)], axis=2)

    def tiles(v):
        n = v.shape[-1] // KEY_TILE
        return v.reshape(bsz, hkv, dh, n, KEY_TILE).transpose(0, 1, 3, 2, 4)

    vw_pad = jnp.pad(vw_t, ((0, 0), (0, 0), (0, 0), (WIN_PAD_LO, WIN_PAD_HI)))
    vt_all = jnp.concatenate([tiles(vs_t), tiles(vw_pad[..., :s + WIN_PAD_LO]), tiles(vw_pad[..., Q_BLOCK:])], axis=2)
    patterns = _mask_patterns()
    n_rows, n_tiles = k_all.shape[2], vt_all.shape[2]

    return pl.pallas_call(
        functools.partial(_attn_kernel, seq=s),
        grid=(bsz, hkv, s // Q_BLOCK),
        in_specs=[
            pl.BlockSpec((1, gd, Q_BLOCK), lambda b, h, i: (b, h, i)),
            pl.BlockSpec((1, 1, 3 * GQA_GROUP, Q_BLOCK), lambda b, h, i: (b, h, 0, i)),
            pl.BlockSpec((1, 1, n_cmp, dh), lambda b, h, i: (b, h, 0, 0)),
            pl.BlockSpec((1, 1, dh, n_cmp), lambda b, h, i: (b, h, 0, 0)),
            pl.BlockSpec((1, 1, n_rows, kdim), lambda b, h, i: (b, h, 0, 0)),
            pl.BlockSpec((1, 1, n_tiles, dh, KEY_TILE), lambda b, h, i: (b, h, 0, 0, 0)),
            _const_spec(patterns.shape),
        ],
        out_specs=pl.BlockSpec((1, Q_BLOCK, gd), lambda b, h, i: (b, i, h)),
        out_shape=jax.ShapeDtypeStruct((bsz, s, hkv * gd), BF16),
        scratch_shapes=[
            pltpu.VMEM((s // SEL_BLOCK, Q_BLOCK), F32),
            pltpu.VMEM((kdim, nq), BF16),
            pltpu.VMEM((KEY_TILE, nq), F32),
            pltpu.VMEM((KEY_TILE, nq), F32),
            pltpu.VMEM((KEY_TILE, nq), BF16),
            pltpu.VMEM((KEY_TILE, nq), BF16),
            pltpu.VMEM((2, 1, nq), F32),
            pltpu.VMEM((2, 1, nq), F32),
            pltpu.VMEM((2, 1, nq), F32),
            pltpu.VMEM((2, dh, nq), F32),
        ],
        compiler_params=_params("parallel", "parallel", "arbitrary"),
        name="nsa_attention",
    )(q_t, gates_t, kc, vc_t, k_all, vt_all, patterns)


def _oproj_kernel(o_ref, x_ref, mod_ref, w_ref, g_ref, b_ref, out_ref):
    y = _dot(o_ref[0], w_ref[...])
    out_ref[0] = _layer_norm(ALPHA * x_ref[0] + mod_ref[0, 2:3, :] * y, g_ref[...], b_ref[...])


def _out_projection(o, x, mod, w_o, ln_g, ln_b):
    bsz, s, d = x.shape
    tm = 512
    return pl.pallas_call(
        _oproj_kernel,
        grid=(bsz, s // tm),
        in_specs=[
            pl.BlockSpec((1, tm, o.shape[-1]), lambda b, j: (b, j, 0)),
            pl.BlockSpec((1, tm, d), lambda b, j: (b, j, 0)),
            pl.BlockSpec((1, 6, d), lambda b, j: (b, 0, 0)),
            _const_spec(w_o.shape),
            _const_spec((1, d)),
            _const_spec((1, d)),
        ],
        out_specs=pl.BlockSpec((1, tm, d), lambda b, j: (b, j, 0)),
        out_shape=jax.ShapeDtypeStruct((bsz, s, d), F32),
        compiler_params=_params("parallel", "parallel"),
        name="nsa_out_proj",
    )(o, x, mod, w_o, ln_g.reshape(1, d), ln_b.reshape(1, d))


def _moe_block(x, mod, router_wt, router_bias, w_gu, w_down, ln_g, ln_b):
    bsz, s, d = x.shape
    t = bsz * s
    xf = x.reshape(t, d)
    wcol, route, counts = _router(xf, mod, router_wt, router_bias, s)

    cnt = counts[:, 0]
    padded = (cnt + EXPERT_TILE - 1) // EXPERT_TILE * EXPERT_TILE
    ends = jnp.cumsum(padded)
    offs = ends - padded
    n_tiles = (TOP_K * t) // EXPERT_TILE + N_EXPERTS
    tile_start = jnp.arange(n_tiles, dtype=jnp.int32) * EXPERT_TILE
    tile_expert = jnp.minimum(jnp.sum(tile_start[:, None] >= ends[None, :], axis=1), N_EXPERTS - 1).astype(jnp.int32)
    n_used = (ends[-1:] // EXPERT_TILE).astype(jnp.int32)
    experts, ranks = route[:TOP_K], route[TOP_K:2 * TOP_K]
    dst = ranks + sum(jnp.where(experts == e, offs[e], 0) for e in range(N_EXPERTS))
    idx = dst.reshape(TOP_K, t // TOKEN_TILE, TOKEN_TILE).transpose(1, 0, 2).reshape(t // TOKEN_TILE, 1, TOP_K * TOKEN_TILE)

    last_tile = jnp.where(cnt > 0, ends - EXPERT_TILE, -1)
    spare = n_used + jnp.arange(N_EXPERTS)
    spare = jnp.where(spare < n_tiles, spare * EXPERT_TILE, -1)
    xs = _dispatch(jnp.concatenate([last_tile, spare]).astype(jnp.int32), idx, xf, mod, n_tiles * EXPERT_TILE, s)
    ys = _experts(tile_expert, n_used, xs, w_gu, w_down)
    out = _combine(idx, xf, mod, wcol, ys, ln_g, ln_b, s)
    return out.reshape(bsz, s, d)


def _nsa_layer(x, mod, w_kv, cmp_pe, cmp_w1, cmp_b1, cmp_w2, cmp_b2, w_qg, b_g, w_o, ln_g, ln_b):
    bsz, s, d = x.shape
    hkv, dh, grp = N_KV_HEADS, HEAD_DIM, GQA_GROUP
    hw = hkv * dh
    kvw = w_kv.reshape(d, 6, hw)
    w_nat = jnp.concatenate([kvw[:, 0], kvw[:, 1], kvw[:, 2], kvw[:, 4]], axis=1).astype(BF16)
    w_vt = jnp.concatenate([kvw[:, 3], kvw[:, 5]], axis=1).T.astype(BF16)
    nq = N_HEADS * dh
    wg = w_qg[:, nq:].reshape(d, hkv, grp, 3).transpose(0, 1, 3, 2).reshape(d, 3 * N_HEADS)
    w_qgt = jnp.concatenate([w_qg[:, :nq], wg], axis=1).T.astype(BF16)
    bg = b_g.reshape(hkv, grp, 3).transpose(0, 2, 1).reshape(3 * N_HEADS, 1)
    kc, vc, ks_aug, kw_aug, vs_t, vw_t, q_t, gates_t = _projections(x, mod, w_nat, w_vt, w_qgt, bg)
    n_chunk = s // CMP_STRIDE
    kv_c = jnp.stack([kc, vc]).reshape(2, bsz, hkv, n_chunk, CMP_STRIDE * dh)
    cmp_nat, cmp_t = _compress(kv_c, cmp_pe.reshape(2, 1, CMP_BLOCK * dh), cmp_w1.astype(BF16),
                               cmp_b1.reshape(2, 1, CMP_HIDDEN), cmp_w2.astype(BF16), cmp_b2.reshape(2, 1, dh))
    o = _attention(q_t, gates_t, cmp_nat[0], cmp_t[1], ks_aug, kw_aug, vs_t, vw_t)
    return _out_projection(o, x, mod, w_o.astype(BF16), ln_g, ln_b)


def kernel(x, c, ada_w, ada_b, ln_g, ln_b, conv_w_in, conv_w, conv_b, conv_w_out, w_kv, cmp_pe, cmp_w1, cmp_b1, cmp_w2, cmp_b2, w_qg, b_g, w_o, router_w, router_bias, w_gu, w_down):
    bsz, s, d = x.shape
    mod = _modulation(c, ada_w, ada_b).reshape(DEPTH, bsz, 6, d)
    router_wt = router_w.T
    w_gu_b = w_gu.astype(BF16)
    w_down_b = w_down.astype(BF16)

    x = _conv_layer(x, mod[0], conv_w_in[0].astype(BF16), conv_w[0], conv_b[0], conv_w_out[0].astype(BF16),
                    ln_g[0, 0], ln_b[0, 0])
    x = _moe_block(x, mod[0], router_wt, router_bias, w_gu_b[0], w_down_b[0], ln_g[0, 1], ln_b[0, 1])

    x = _nsa_layer(x, mod[1], w_kv, cmp_pe, cmp_w1, cmp_b1, cmp_w2, cmp_b2, w_qg[0], b_g[0], w_o[0],
                   ln_g[1, 0], ln_b[1, 0])
    x = _moe_block(x, mod[1], router_wt, router_bias, w_gu_b[1], w_down_b[1], ln_g[1, 1], ln_b[1, 1])
    return x
```

```python
import functools
import math

import jax
import jax.numpy as jnp
import numpy as np
from jax import lax
from jax.experimental import pallas as pl
from jax.experimental.pallas import tpu as pltpu

F32 = jnp.float32
BF16 = jnp.bfloat16

D_MODEL = 1024
DEPTH = 2
N_A_LAYERS = DEPTH // 2
CONV_WIDTH = 3
N_HEADS = 16
HEAD_DIM = D_MODEL // N_HEADS
N_KV_HEADS = 4
GQA_GROUP = N_HEADS // N_KV_HEADS
CMP_BLOCK = 32
CMP_STRIDE = 16
CMP_HIDDEN = 4 * HEAD_DIM
SEL_BLOCK = 64
SEL_TOP = 16
WINDOW = 512
N_EXPERTS = 16
N_GROUPS = 4
EXPERTS_PER_GROUP = N_EXPERTS // N_GROUPS
TOP_K = 2
D_FF_EXPERT = D_MODEL // 2
ALPHA = (2 * DEPTH) ** 0.25
LN_EPS = 1e-5
NEG = -1e30
FORCE = 1e9

LANES = 128
VMEM_LIMIT_BYTES = 56 * 1024 * 1024

LOG2E = math.log2(math.e)

ATT_Q = 256
KEY_TILE = 256
PAT_ZERO, PAT_DIAG, PAT_ANTI, PAT_NONE = range(4)


def _dot(a, b):
    return jnp.dot(a, b, preferred_element_type=F32)


def _dot_nt(a, b):
    return lax.dot_general(a, b, (((1,), (1,)), ((), ())), preferred_element_type=F32)


def _split(x):
    hi = x.astype(BF16)
    lo = (x - hi.astype(F32)).astype(BF16)
    return hi, lo


def _dot3(a, b):
    ah, al = _split(a)
    bh, bl = _split(b)
    return _dot(ah, bh) + (_dot(ah, bl) + _dot(al, bh))


def _dot3_nt(a, b):
    ah, al = _split(a)
    bh, bl = _split(b)
    return _dot_nt(ah, bh) + (_dot_nt(ah, bl) + _dot_nt(al, bh))


def _layer_norm(r, g, b):
    mu = jnp.mean(r, axis=-1, keepdims=True)
    d = r - mu
    var = jnp.mean(d * d, axis=-1, keepdims=True)
    return d * lax.rsqrt(var + LN_EPS) * g + b


def _params(*sem):
    return pltpu.CompilerParams(dimension_semantics=sem, vmem_limit_bytes=VMEM_LIMIT_BYTES)


def _const_spec(shape):
    zeros = (0,) * len(shape)
    return pl.BlockSpec(shape, lambda *_: zeros)


def _mod_kernel(c_ref, w_ref, b_ref, o_ref):
    c = c_ref[...]
    s = c * jax.nn.sigmoid(c)
    o_ref[0] = _dot3(s, w_ref[0]) + b_ref[0]


def _modulation(c, ada_w, ada_b):
    depth, d, n = ada_w.shape
    bsz = c.shape[0]
    tn = 1536
    return pl.pallas_call(
        _mod_kernel,
        grid=(depth, n // tn),
        in_specs=[
            pl.BlockSpec((bsz, d), lambda l, j: (0, 0)),
            pl.BlockSpec((1, d, tn), lambda l, j: (l, 0, j)),
            pl.BlockSpec((1, 1, tn), lambda l, j: (l, 0, j)),
        ],
        out_specs=pl.BlockSpec((1, bsz, tn), lambda l, j: (l, 0, j)),
        out_shape=jax.ShapeDtypeStruct((depth, bsz, n), F32),
        compiler_params=_params("parallel", "parallel"),
        name="adaln_mod",
    )(c, ada_w, ada_b.reshape(depth, 1, n))


CONV_HALO = 8


def _conv_kernel(x_ref, mod_ref, win_ref, cw_ref, cb_ref, wout_ref, g_ref, b_ref, o_ref, z_ref):
    tm = x_ref.shape[1]
    d = x_ref.shape[2]

    @pl.when(pl.program_id(1) == 0)
    def _():
        z_ref[0:CONV_HALO, :] = jnp.zeros((CONV_HALO, d), F32)

    x = x_ref[0]
    sh = mod_ref[0, 0:1, :]
    sc = mod_ref[0, 1:2, :]
    gate = mod_ref[0, 2:3, :]
    u = (x * (1.0 + sc) + sh).astype(BF16)
    bch = _dot(u, win_ref[...])
    z = bch[:, d:2 * d] * bch[:, 2 * d:]
    z_ref[CONV_HALO:CONV_HALO + tm, :] = z
    z1 = z_ref[CONV_HALO - 1:CONV_HALO - 1 + tm, :]
    z2 = z_ref[CONV_HALO - 2:CONV_HALO - 2 + tm, :]
    conv = cw_ref[0:1, :] * z2 + cw_ref[1:2, :] * z1 + cw_ref[2:3, :] * z + cb_ref[...]
    v = (bch[:, :d] * conv).astype(BF16)
    y = _dot(v, wout_ref[...])
    o_ref[0] = _layer_norm(ALPHA * x + gate * y, g_ref[...], b_ref[...])
    z_ref[0:CONV_HALO, :] = z_ref[tm:tm + CONV_HALO, :]


def _conv_layer(x, mod, w_in, conv_w, conv_b, w_out, ln_g, ln_b):
    bsz, s, d = x.shape
    tm = 512
    return pl.pallas_call(
        _conv_kernel,
        grid=(bsz, s // tm),
        in_specs=[
            pl.BlockSpec((1, tm, d), lambda b, j: (b, j, 0)),
            pl.BlockSpec((1, 6, d), lambda b, j: (b, 0, 0)),
            _const_spec((d, 3 * d)),
            _const_spec((CONV_WIDTH, d)),
            _const_spec((1, d)),
            _const_spec((d, d)),
            _const_spec((1, d)),
            _const_spec((1, d)),
        ],
        out_specs=pl.BlockSpec((1, tm, d), lambda b, j: (b, j, 0)),
        out_shape=jax.ShapeDtypeStruct((bsz, s, d), F32),
        scratch_shapes=[pltpu.VMEM((tm + CONV_HALO, d), F32)],
        compiler_params=_params("arbitrary", "arbitrary"),
        name="conv_mixer",
    )(x, mod, w_in, conv_w, conv_b.reshape(1, d), w_out, ln_g.reshape(1, d), ln_b.reshape(1, d))


def _router_kernel(x_ref, mod_ref, rwt_ref, rb_ref, wcol_ref, route_ref, count_ref, base_ref, tri_ref):
    tm = x_ref.shape[0]
    x = x_ref[...]
    u = x * (1.0 + mod_ref[0, 4:5, :]) + mod_ref[0, 3:4, :]
    logits = _dot3_nt(rwt_ref[...], u)
    m = jnp.max(logits, axis=0, keepdims=True)
    e = jnp.exp(logits - m)
    aff = e / jnp.sum(e, axis=0, keepdims=True)
    biased = aff + rb_ref[...]
    aff_r = [aff[i:i + 1, :] for i in range(N_EXPERTS)]
    row = [biased[i:i + 1, :] for i in range(N_EXPERTS)]

    best_s, best = None, None
    for g in range(N_GROUPS):
        r = row[g * EXPERTS_PER_GROUP:(g + 1) * EXPERTS_PER_GROUP]
        gs = None
        for i in range(EXPERTS_PER_GROUP):
            for j in range(i + 1, EXPERTS_PER_GROUP):
                p = r[i] + r[j]
                gs = p if gs is None else jnp.maximum(gs, p)
        if g == 0:
            best_s, best = gs, jnp.zeros((1, tm), jnp.int32)
        else:
            upd = gs > best_s
            best = jnp.where(upd, g, best)
            best_s = jnp.where(upd, gs, best_s)

    masked = [jnp.where(best == (i // EXPERTS_PER_GROUP), row[i], NEG) for i in range(N_EXPERTS)]

    def first_argmax(vals):
        v, idx = vals[0], jnp.zeros((1, tm), jnp.int32)
        for i in range(1, N_EXPERTS):
            upd = vals[i] > v
            idx = jnp.where(upd, i, idx)
            v = jnp.where(upd, vals[i], v)
        return idx

    i0 = first_argmax(masked)
    i1 = first_argmax([jnp.where(i0 == i, -jnp.inf, masked[i]) for i in range(N_EXPERTS)])
    w0 = sum(jnp.where(i0 == i, aff_r[i], 0.0) for i in range(N_EXPERTS))
    w1 = sum(jnp.where(i1 == i, aff_r[i], 0.0) for i in range(N_EXPERTS))
    tot = w0 + w1
    w0 = w0 / tot
    w1 = w1 / tot
    w_t = jnp.concatenate([w0, w1, jnp.zeros((LANES - TOP_K, tm), F32)], axis=0)
    wcol_ref[...] = w_t.T

    @pl.when(pl.program_id(0) == 0)
    def _():
        base_ref[...] = jnp.zeros(base_ref.shape, F32)
        r_i = lax.broadcasted_iota(jnp.int32, tri_ref.shape, 0)
        c_i = lax.broadcasted_iota(jnp.int32, tri_ref.shape, 1)
        tri_ref[...] = jnp.where(r_i <= c_i, 1.0, 0.0).astype(BF16)

    hot = jnp.concatenate([jnp.where((i0 == i) | (i1 == i), 1.0, 0.0) for i in range(N_EXPERTS)], axis=0)
    incl = _dot(hot.astype(BF16), tri_ref[...])
    pos = base_ref[...] + (incl - hot)
    rank0 = sum(jnp.where(i0 == i, pos[i:i + 1, :], 0.0) for i in range(N_EXPERTS))
    rank1 = sum(jnp.where(i1 == i, pos[i:i + 1, :], 0.0) for i in range(N_EXPERTS))
    base = base_ref[...] + jnp.sum(hot, axis=1, keepdims=True)
    base_ref[...] = base
    route_ref[...] = jnp.concatenate(
        [i0, i1, rank0.astype(jnp.int32), rank1.astype(jnp.int32), jnp.zeros((4, tm), jnp.int32)], axis=0)
    count_ref[...] = jnp.broadcast_to(base, count_ref.shape).astype(jnp.int32)


def _router(x, mod, router_wt, router_bias, seq):
    t, d = x.shape
    tm = 1024
    return pl.pallas_call(
        _router_kernel,
        grid=(t // tm,),
        in_specs=[
            pl.BlockSpec((tm, d), lambda i: (i, 0)),
            pl.BlockSpec((1, 6, d), lambda i: ((i * tm) // seq, 0, 0)),
            _const_spec((N_EXPERTS, d)),
            _const_spec((N_EXPERTS, 1)),
        ],
        out_specs=(
            pl.BlockSpec((tm, LANES), lambda i: (i, 0)),
            pl.BlockSpec((8, tm), lambda i: (0, i)),
            _const_spec((N_EXPERTS, LANES)),
        ),
        out_shape=(
            jax.ShapeDtypeStruct((t, LANES), F32),
            jax.ShapeDtypeStruct((8, t), jnp.int32),
            jax.ShapeDtypeStruct((N_EXPERTS, LANES), jnp.int32),
        ),
        scratch_shapes=[pltpu.VMEM((N_EXPERTS, 1), F32), pltpu.VMEM((tm, tm), BF16)],
        compiler_params=_params("arbitrary"),
        name="moe_router",
    )(x, mod, router_wt, router_bias.reshape(N_EXPERTS, 1))


EXPERT_TILE = 512
TOKEN_TILE = 512
ROW_COPY_UNROLL = 8
ROW_TILES = D_MODEL // LANES


def _store_rows(ref, x):
    rows = x.shape[0]
    for c in range(ROW_TILES):
        ref[pl.ds(c, rows, stride=ROW_TILES), :] = x[:, c * LANES:(c + 1) * LANES]


def _load_rows(ref, rows):
    return jnp.concatenate([ref[pl.ds(c, rows, stride=ROW_TILES), :] for c in range(ROW_TILES)], axis=1)


def _row(ref, r):
    return ref.at[pl.ds(pl.multiple_of(r * ROW_TILES, ROW_TILES), ROW_TILES)]


def _row_copies(tm, row_copy, tile_copy):
    def start(t, c):
        for k in range(TOP_K):
            row_copy(t, k).start()
        return c

    lax.fori_loop(0, tm, start, 0, unroll=ROW_COPY_UNROLL)
    for k in range(TOP_K):
        tile_copy(k).wait()


def _dispatch_kernel(pad_ref, idx_ref, x_ref, mod_ref, xs_ref, u_ref, sem):
    tm = x_ref.shape[0]

    @pl.when(pl.program_id(0) == 0)
    def _():
        u_ref[...] = jnp.zeros(u_ref.shape, F32)
        for wait in (False, True):
            for e in range(pad_ref.shape[0]):
                @pl.when(pad_ref[e] >= 0)
                def _():
                    row = pl.multiple_of(jnp.maximum(pad_ref[e], 0) * ROW_TILES, tm * ROW_TILES)
                    fill = pltpu.make_async_copy(u_ref, xs_ref.at[pl.ds(row, tm * ROW_TILES)], sem)
                    fill.wait() if wait else fill.start()

    _store_rows(u_ref, x_ref[...] * (1.0 + mod_ref[0, 4:5, :]) + mod_ref[0, 3:4, :])

    def row_copy(t, k):
        dst = idx_ref[0, 0, k * tm + t]
        return pltpu.make_async_copy(_row(u_ref, t), _row(xs_ref, dst), sem)

    _row_copies(tm, row_copy, lambda k: pltpu.make_async_copy(u_ref, xs_ref.at[pl.ds(0, tm * ROW_TILES)], sem))


def _dispatch(pad_start, idx, x, mod, n_rows, seq):
    t, d = x.shape
    tm = TOKEN_TILE
    assert tm == EXPERT_TILE
    return pl.pallas_call(
        _dispatch_kernel,
        grid_spec=pltpu.PrefetchScalarGridSpec(
            num_scalar_prefetch=1,
            grid=(t // tm,),
            in_specs=[
                pl.BlockSpec((1, 1, TOP_K * tm), lambda i, pad: (i, 0, 0), memory_space=pltpu.SMEM),
                pl.BlockSpec((tm, d), lambda i, pad: (i, 0)),
                pl.BlockSpec((1, 6, d), lambda i, pad: ((i * tm) // seq, 0, 0)),
            ],
            out_specs=pl.BlockSpec(memory_space=pl.ANY),
            scratch_shapes=[pltpu.VMEM((tm * ROW_TILES, LANES), F32), pltpu.SemaphoreType.DMA],
        ),
        out_shape=jax.ShapeDtypeStruct((n_rows * ROW_TILES, LANES), F32),
        compiler_params=_params("arbitrary"),
        name="moe_dispatch",
    )(pad_start, idx, x, mod)


def _expert_kernel(te_ref, nu_ref, xs_ref, wgu_ref, wdn_ref, ys_ref):
    del te_ref
    used = pl.program_id(0) < nu_ref[0]

    @pl.when(used)
    def _():
        f = wdn_ref.shape[1]
        h = _dot(_load_rows(xs_ref, EXPERT_TILE).astype(BF16), wgu_ref[0].astype(BF16))
        gate = h[:, :f]
        a = (gate * jax.nn.sigmoid(gate) * h[:, f:]).astype(BF16)
        _store_rows(ys_ref, _dot(a, wdn_ref[0].astype(BF16)))

    @pl.when(jnp.logical_not(used))
    def _():
        ys_ref[...] = jnp.zeros(ys_ref.shape, F32)


def _experts(tile_expert, n_used, xs, w_gu, w_down):
    _, d, f2 = w_gu.shape
    f = f2 // 2
    row_map = lambda s, te, nu: (s, 0)
    return pl.pallas_call(
        _expert_kernel,
        grid_spec=pltpu.PrefetchScalarGridSpec(
            num_scalar_prefetch=2,
            grid=(tile_expert.shape[0],),
            in_specs=[
                pl.BlockSpec((EXPERT_TILE * ROW_TILES, LANES), row_map),
                pl.BlockSpec((1, d, f2), lambda s, te, nu: (te[s], 0, 0)),
                pl.BlockSpec((1, f, d), lambda s, te, nu: (te[s], 0, 0)),
            ],
            out_specs=pl.BlockSpec((EXPERT_TILE * ROW_TILES, LANES), row_map),
        ),
        out_shape=jax.ShapeDtypeStruct(xs.shape, F32),
        compiler_params=_params("arbitrary"),
        name="moe_experts",
    )(tile_expert, n_used, xs, w_gu, w_down)


def _combine_kernel(idx_ref, x_ref, mod_ref, w_ref, ys_ref, g_ref, b_ref, o_ref, y_ref, sem):
    tm = x_ref.shape[0]

    def row_copy(t, k):
        src = idx_ref[0, 0, k * tm + t]
        return pltpu.make_async_copy(_row(ys_ref, src), _row(y_ref.at[k], t), sem)

    _row_copies(tm, row_copy,
                lambda k: pltpu.make_async_copy(ys_ref.at[pl.ds(0, tm * ROW_TILES)], y_ref.at[k], sem))
    w = w_ref[...]
    out = w[:, 0:1] * _load_rows(y_ref.at[0], tm) + w[:, 1:2] * _load_rows(y_ref.at[1], tm)
    r = ALPHA * x_ref[...] + mod_ref[0, 5:6, :] * out
    o_ref[...] = _layer_norm(r, g_ref[...], b_ref[...])


def _combine(idx, x, mod, wcol, ys, ln_g, ln_b, seq):
    t, d = x.shape
    tm = TOKEN_TILE
    return pl.pallas_call(
        _combine_kernel,
        grid=(t // tm,),
        in_specs=[
            pl.BlockSpec((1, 1, TOP_K * tm), lambda i: (i, 0, 0), memory_space=pltpu.SMEM),
            pl.BlockSpec((tm, d), lambda i: (i, 0)),
            pl.BlockSpec((1, 6, d), lambda i: ((i * tm) // seq, 0, 0)),
            pl.BlockSpec((tm, LANES), lambda i: (i, 0)),
            pl.BlockSpec(memory_space=pl.ANY),
            _const_spec((1, d)),
            _const_spec((1, d)),
        ],
        out_specs=pl.BlockSpec((tm, d), lambda i: (i, 0)),
        out_shape=jax.ShapeDtypeStruct((t, d), F32),
        scratch_shapes=[pltpu.VMEM((TOP_K, tm * ROW_TILES, LANES), F32), pltpu.SemaphoreType.DMA],
        compiler_params=_params("arbitrary"),
        name="moe_combine",
    )(idx, x, mod, wcol, ys, ln_g.reshape(1, d), ln_b.reshape(1, d))


def _proj_kernel(x_ref, mod_ref, wn_ref, wvt_ref, wqt_ref, bg_ref,
                 kvc_ref, ks_ref, kw_ref, vst_ref, vwt_ref, qt_ref, gt_ref):
    tm = x_ref.shape[1]
    j = pl.program_id(1)
    x = x_ref[0]
    xb = x.astype(BF16)
    nat = _dot(xb, wn_ref[...])
    vt = _dot_nt(wvt_ref[...], xb)
    u = (x * (1.0 + mod_ref[0, 1:2, :]) + mod_ref[0, 0:1, :]).astype(BF16)
    qg = _dot_nt(wqt_ref[...], u)
    nq = N_HEADS * HEAD_DIM
    qt_ref[0] = (qg[:nq, :] * (HEAD_DIM ** -0.5 * LOG2E)).astype(BF16)
    gates = jax.nn.sigmoid(qg[nq:, :] + bg_ref[...])
    per = 3 * GQA_GROUP
    blk = (j * tm + lax.broadcasted_iota(jnp.int32, (tm, SEL_BLOCK), 0)) // SEL_BLOCK
    onehot = (blk == lax.broadcasted_iota(jnp.int32, (tm, SEL_BLOCK), 1)).astype(BF16)
    zeros = jnp.zeros((tm, SEL_BLOCK), BF16)
    hw = N_KV_HEADS * HEAD_DIM
    for h in range(N_KV_HEADS):
        c0 = h * HEAD_DIM
        kvc_ref[0, h] = nat[:, 2 * c0:2 * c0 + 2 * HEAD_DIM]
        ks = nat[:, 2 * hw + c0:2 * hw + c0 + HEAD_DIM].astype(BF16)
        ks_ref[0, h] = jnp.concatenate([ks, onehot], axis=1)
        kw = nat[:, 3 * hw + c0:3 * hw + c0 + HEAD_DIM].astype(BF16)
        kw_ref[0, h] = jnp.concatenate([kw, zeros], axis=1)
        vst_ref[0, h] = vt[c0:c0 + HEAD_DIM, :].astype(BF16)
        vwt_ref[0, h] = vt[hw + c0:hw + c0 + HEAD_DIM, :].astype(BF16)
        gt_ref[0, h] = gates[h * per:(h + 1) * per, :]


def _projections(x, mod, w_nat, w_vt, w_qgt, b_g):
    bsz, s, d = x.shape
    tm = 512
    hkv, dh = N_KV_HEADS, HEAD_DIM
    per = 3 * GQA_GROUP
    nqg = w_qgt.shape[0]
    out_shape = (
        jax.ShapeDtypeStruct((bsz, hkv, s, 2 * dh), F32),
        jax.ShapeDtypeStruct((bsz, hkv, s, 2 * dh), BF16),
        jax.ShapeDtypeStruct((bsz, hkv, s, 2 * dh), BF16),
        jax.ShapeDtypeStruct((bsz, hkv, dh, s), BF16),
        jax.ShapeDtypeStruct((bsz, hkv, dh, s), BF16),
        jax.ShapeDtypeStruct((bsz, N_HEADS * dh, s), BF16),
        jax.ShapeDtypeStruct((bsz, hkv, per, s), F32),
    )
    out_specs = (
        pl.BlockSpec((1, hkv, tm, 2 * dh), lambda b, j: (b, 0, j, 0)),
        pl.BlockSpec((1, hkv, tm, 2 * dh), lambda b, j: (b, 0, j, 0)),
        pl.BlockSpec((1, hkv, tm, 2 * dh), lambda b, j: (b, 0, j, 0)),
        pl.BlockSpec((1, hkv, dh, tm), lambda b, j: (b, 0, 0, j)),
        pl.BlockSpec((1, hkv, dh, tm), lambda b, j: (b, 0, 0, j)),
        pl.BlockSpec((1, N_HEADS * dh, tm), lambda b, j: (b, 0, j)),
        pl.BlockSpec((1, hkv, per, tm), lambda b, j: (b, 0, 0, j)),
    )
    return pl.pallas_call(
        _proj_kernel,
        grid=(bsz, s // tm),
        in_specs=[
            pl.BlockSpec((1, tm, d), lambda b, j: (b, j, 0)),
            pl.BlockSpec((1, 6, d), lambda b, j: (b, 0, 0)),
            _const_spec(w_nat.shape),
            _const_spec(w_vt.shape),
            _const_spec(w_qgt.shape),
            _const_spec((nqg - N_HEADS * dh, 1)),
        ],
        out_specs=out_specs,
        out_shape=out_shape,
        compiler_params=_params("parallel", "parallel"),
        name="nsa_projections",
    )(x, mod, w_nat, w_vt, w_qgt, b_g)


def _compress_kernel(kvc_ref, pe_ref, w1_ref, b1_ref, w2_ref, b2_ref, nat_ref, tr_ref):
    n = kvc_ref.shape[2] // CMP_STRIDE
    hid2 = w1_ref.shape[-1]
    p = jnp.zeros((n, hid2), F32)
    q = jnp.zeros((n, hid2), F32)
    for l in range(CMP_STRIDE):
        x = kvc_ref[0, 0, pl.ds(l, n, stride=CMP_STRIDE), :]
        p = p + _dot((x + pe_ref[l:l + 1, :]).astype(BF16), w1_ref[0, l])
        q = q + _dot((x + pe_ref[CMP_STRIDE + l:CMP_STRIDE + l + 1, :]).astype(BF16), w1_ref[1, l])
    pre = p + pltpu.roll(q, n - 1, 0) + b1_ref[...]
    hdn = 0.5 * pre * (1.0 + jnp.tanh(0.7978845608028654 * (pre + 0.044715 * (pre * pre * pre))))
    out = _dot(hdn.astype(BF16), w2_ref[...]) + b2_ref[...]
    nat_ref[0, 0] = out.astype(BF16)
    tr_ref[0, 0] = out[:, HEAD_DIM:].T.astype(BF16)


def _compress(kvc, cmp_pe, cmp_w1, cmp_b1, cmp_w2, cmp_b2):
    bsz, hkv, s, two_dh = kvc.shape
    dh = two_dh // 2
    n = s // CMP_STRIDE
    hid = cmp_w1.shape[-1]
    zw = jnp.zeros((2, CMP_STRIDE, dh, hid), F32)
    w1 = cmp_w1.reshape(2, 2, CMP_STRIDE, dh, hid)
    w1 = jnp.concatenate([jnp.concatenate([w1[0], zw], axis=-1), jnp.concatenate([zw, w1[1]], axis=-1)], axis=-2)
    zd = jnp.zeros((hid, dh), F32)
    w2 = jnp.concatenate([jnp.concatenate([cmp_w2[0], zd], axis=1), jnp.concatenate([zd, cmp_w2[1]], axis=1)], axis=0)
    pe = jnp.concatenate([cmp_pe[0], cmp_pe[1]], axis=1)
    return pl.pallas_call(
        _compress_kernel,
        grid=(bsz, hkv),
        in_specs=[
            pl.BlockSpec((1, 1, s, two_dh), lambda b, h: (b, h, 0, 0)),
            _const_spec(pe.shape),
            _const_spec(w1.shape),
            _const_spec((1, 2 * hid)),
            _const_spec(w2.shape),
            _const_spec((1, two_dh)),
        ],
        out_specs=(
            pl.BlockSpec((1, 1, n, two_dh), lambda b, h: (b, h, 0, 0)),
            pl.BlockSpec((1, 1, dh, n), lambda b, h: (b, h, 0, 0)),
        ),
        out_shape=(
            jax.ShapeDtypeStruct((bsz, hkv, n, two_dh), BF16),
            jax.ShapeDtypeStruct((bsz, hkv, dh, n), BF16),
        ),
        compiler_params=_params("parallel", "parallel"),
        name="nsa_compress",
    )(kvc, pe, w1.astype(BF16), cmp_b1.reshape(1, 2 * hid), w2.astype(BF16), cmp_b2.reshape(1, two_dh))


def _mask_patterns():
    assert WINDOW == 2 * KEY_TILE and ATT_Q == KEY_TILE
    keyl = np.arange(KEY_TILE)[:, None]
    ql = (np.arange(GQA_GROUP * ATT_Q) % ATT_Q)[None, :]
    true = np.ones((KEY_TILE, GQA_GROUP * ATT_Q), bool)
    valid = np.stack([
        true,
        keyl <= ql,
        keyl > ql,
        ~true,
    ])
    return jnp.asarray(np.where(valid, 0.0, NEG), F32)


def _attn_kernel(q_ref, g_ref, kcv_ref, vct_ref, ks_ref, kw_ref, vs_ref, vw_ref, pat_ref, o_ref,
                 kall_ref, vt_ref, sc_ref, qaug_ref, s0_ref, s1_ref, p0_ref, p1_ref, al_ref, mt_ref,
                 m_ref, l_ref, acc_ref):
    i = pl.program_id(2)
    seq = ks_ref.shape[2]
    nq = GQA_GROUP * ATT_Q
    dh = HEAD_DIM
    n_key_tiles = seq // KEY_TILE

    @pl.when(i == 0)
    def _():
        kall_ref[0:seq, :] = ks_ref[0, 0]
        kall_ref[seq:2 * seq, :] = kw_ref[0, 0]
        for n in range(n_key_tiles):
            vt_ref[n] = vs_ref[0, 0, :, n * KEY_TILE:(n + 1) * KEY_TILE]
            vt_ref[n_key_tiles + n] = vw_ref[0, 0, :, n * KEY_TILE:(n + 1) * KEY_TILE]

    qa = q_ref[0]
    qt = jnp.concatenate([qa[g * dh:(g + 1) * dh, :] for g in range(GQA_GROUP)], axis=1)
    t_row = i * ATT_Q + lax.broadcasted_iota(jnp.int32, (1, ATT_Q), 1)
    t4 = jnp.concatenate([t_row] * GQA_GROUP, axis=1)

    n_cmp = kcv_ref.shape[2]
    q_c = jnp.concatenate([qt, jnp.zeros((kcv_ref.shape[3] - dh, nq), BF16)], axis=0)
    s_c = _dot(kcv_ref[0, 0], q_c)
    cmp_end = lax.broadcasted_iota(jnp.int32, (n_cmp, 1), 0) * CMP_STRIDE + (CMP_BLOCK - 1)
    vis = cmp_end <= t4
    s_c = jnp.where(vis, s_c, NEG)
    m_c = jnp.max(s_c, axis=0, keepdims=True)
    e_c = jnp.where(vis, jnp.exp2(s_c - m_c), 0.0)
    p_c = e_c / jnp.maximum(jnp.sum(e_c, axis=0, keepdims=True), 1e-30)
    o_c = _dot(vct_ref[0, 0], p_c.astype(BF16))

    n_sel = seq // SEL_BLOCK
    p_sum = sum(p_c[:, g * ATT_Q:(g + 1) * ATT_Q] for g in range(GQA_GROUP))
    jj = lax.broadcasted_iota(jnp.int32, (n_sel, n_cmp), 0) * SEL_BLOCK
    nn = lax.broadcasted_iota(jnp.int32, (n_sel, n_cmp), 1) * CMP_STRIDE
    ov = jnp.minimum(nn + CMP_BLOCK, jj + SEL_BLOCK) - jnp.maximum(nn, jj)
    ov_t = (jnp.maximum(ov, 0).astype(F32) * (1.0 / CMP_BLOCK)).astype(BF16)
    p_hi, p_lo = _split(p_sum)
    imp = _dot(ov_t, p_hi) + _dot(ov_t, p_lo)
    jb = lax.broadcasted_iota(jnp.int32, (n_sel, 1), 0)
    cur = t_row // SEL_BLOCK
    allowed = jb <= cur
    forced = (jb == 0) | (jb == cur) | (jb == cur - 1)
    score = jnp.where(forced & allowed, FORCE, jnp.where(allowed, imp, NEG))
    sc_ref[...] = score
    n_top = min(SEL_TOP, n_sel)
    sub = 8
    groups = [score[r:r + sub, :] for r in range(0, n_sel, sub)]
    ranks = [jnp.zeros((sub, ATT_Q), F32) for _ in groups]
    for k in range(n_sel):
        row = jnp.broadcast_to(sc_ref[k:k + 1, :], (sub, ATT_Q))
        for r, grp in enumerate(groups):
            if r * sub > k:
                ahead = row >= grp
            elif r * sub + sub - 1 < k:
                ahead = row > grp
            else:
                later = (r * sub + lax.broadcasted_iota(jnp.int32, (sub, 1), 0)) > k
                ahead = (row > grp) | ((row == grp) & later)
            ranks[r] = ranks[r] + jnp.where(ahead, 1.0, 0.0)
    rank = jnp.concatenate(ranks, axis=0)
    chosen = (rank < n_top) & (score > 0.5 * NEG)
    bias = jnp.where(chosen, 0.0, NEG).astype(BF16)
    pad = kall_ref.shape[1] - dh - n_sel
    if pad:
        bias = jnp.concatenate([bias, jnp.zeros((pad, ATT_Q), BF16)], axis=0)
    qaug_ref[...] = jnp.concatenate([qt, jnp.concatenate([bias] * GQA_GROUP, axis=1)], axis=0)

    n_s = i + 1
    n_jobs = n_s + 1 + (i >= 1).astype(jnp.int32) + (i >= 2).astype(jnp.int32)

    def job(j):
        is_sel = j < n_s
        valid = (j >= 0) & (j < n_jobs)
        w = j - n_s
        tile = jnp.where(is_sel, j, n_key_tiles + i - w)
        pat_sel = jnp.where(j == i, PAT_DIAG, PAT_ZERO)
        pat_win = jnp.where(w == 0, PAT_DIAG, jnp.where(w == 1, PAT_ZERO, PAT_ANTI))
        pat = jnp.where(valid, jnp.where(is_sel, pat_sel, pat_win), PAT_NONE)
        return jnp.where(valid, tile, 0), pat, jnp.where(is_sel, 0, 1)

    def stage_a(j, s_ref, slot, masked):
        tile, pat, _ = job(j)
        k = kall_ref[pl.ds(pl.multiple_of(tile * KEY_TILE, KEY_TILE), KEY_TILE), :]
        s = _dot(k, qaug_ref[...])
        if masked:
            s = s + pat_ref[pat]
        s_ref[...] = s
        mt_ref[slot] = jnp.max(s, axis=0, keepdims=True)

    def stage_b(j, s_ref, p_ref, slot):
        a = job(j)[2]
        m_old = m_ref[a]
        m_new = jnp.maximum(m_old, mt_ref[slot])
        alpha = jnp.exp2(m_old - m_new)
        p = jnp.exp2(s_ref[...] - m_new)
        l_ref[a] = alpha * l_ref[a] + jnp.sum(p, axis=0, keepdims=True)
        m_ref[a] = m_new
        al_ref[slot] = alpha
        p_ref[...] = p.astype(BF16)

    def stage_c(j, p_ref, slot):
        tile, _, a = job(j)
        acc_ref[a] = al_ref[slot] * acc_ref[a] + _dot(vt_ref[tile], p_ref[...])

    s0_ref[...] = jnp.full(s0_ref.shape, NEG, F32)
    s1_ref[...] = jnp.full(s1_ref.shape, NEG, F32)
    mt_ref[...] = jnp.full(mt_ref.shape, NEG, F32)
    p0_ref[...] = jnp.zeros(p0_ref.shape, BF16)
    p1_ref[...] = jnp.zeros(p1_ref.shape, BF16)
    al_ref[...] = jnp.ones(al_ref.shape, F32)
    m_ref[...] = jnp.full(m_ref.shape, NEG, F32)
    l_ref[...] = jnp.zeros(l_ref.shape, F32)
    acc_ref[...] = jnp.zeros(acc_ref.shape, F32)

    def two_steps(jj, carry, masked):
        j = 2 * jj
        stage_c(j - 4, p0_ref, 0)
        stage_c(j - 3, p1_ref, 1)
        stage_b(j - 2, s0_ref, p0_ref, 0)
        stage_b(j - 1, s1_ref, p1_ref, 1)
        stage_a(j, s0_ref, 0, masked)
        stage_a(j + 1, s1_ref, 1, masked)
        return carry

    n_plain = i // 2
    lax.fori_loop(0, n_plain, functools.partial(two_steps, masked=False), 0)
    lax.fori_loop(n_plain, (n_jobs + 5) // 2, functools.partial(two_steps, masked=True), 0)
    o_s = acc_ref[0] / l_ref[0]
    o_w = acc_ref[1] / l_ref[1]

    gates = g_ref[0, 0]
    def gate_row(br):
        return jnp.concatenate([gates[br * GQA_GROUP + g:br * GQA_GROUP + g + 1, :] for g in range(GQA_GROUP)], axis=1)
    o = gate_row(0) * o_c + gate_row(1) * o_s + gate_row(2) * o_w
    o_ref[0] = jnp.concatenate([o[:, g * ATT_Q:(g + 1) * ATT_Q].T for g in range(GQA_GROUP)], axis=1).astype(BF16)


def _attention(q_t, gates_t, kcv, vc_t, ks_aug, kw_aug, vs_t, vw_t):
    bsz, hkv, s, kdim = ks_aug.shape
    dh = vs_t.shape[2]
    n_cmp = kcv.shape[2]
    gd = GQA_GROUP * dh
    nq = GQA_GROUP * ATT_Q
    patterns = _mask_patterns()
    per_head = lambda shape: pl.BlockSpec((1, 1) + shape, lambda b, h, i: (b, h, 0, 0))
    return pl.pallas_call(
        _attn_kernel,
        grid=(bsz, hkv, s // ATT_Q),
        in_specs=[
            pl.BlockSpec((1, gd, ATT_Q), lambda b, h, i: (b, h, i)),
            pl.BlockSpec((1, 1, 3 * GQA_GROUP, ATT_Q), lambda b, h, i: (b, h, 0, i)),
            per_head((n_cmp, kcv.shape[3])),
            per_head((dh, n_cmp)),
            per_head((s, kdim)),
            per_head((s, kdim)),
            per_head((dh, s)),
            per_head((dh, s)),
            _const_spec(patterns.shape),
        ],
        out_specs=pl.BlockSpec((1, ATT_Q, gd), lambda b, h, i: (b, i, h)),
        out_shape=jax.ShapeDtypeStruct((bsz, s, hkv * gd), BF16),
        scratch_shapes=[
            pltpu.VMEM((2 * s, kdim), BF16),
            pltpu.VMEM((2 * (s // KEY_TILE), dh, KEY_TILE), BF16),
            pltpu.VMEM((s // SEL_BLOCK, ATT_Q), F32),
            pltpu.VMEM((kdim, nq), BF16),
            pltpu.VMEM((KEY_TILE, nq), F32),
            pltpu.VMEM((KEY_TILE, nq), F32),
            pltpu.VMEM((KEY_TILE, nq), BF16),
            pltpu.VMEM((KEY_TILE, nq), BF16),
            pltpu.VMEM((2, 1, nq), F32),
            pltpu.VMEM((2, 1, nq), F32),
            pltpu.VMEM((2, 1, nq), F32),
            pltpu.VMEM((2, 1, nq), F32),
            pltpu.VMEM((2, dh, nq), F32),
        ],
        compiler_params=_params("parallel", "parallel", "arbitrary"),
        name="nsa_attention",
    )(q_t, gates_t, kcv, vc_t, ks_aug, kw_aug, vs_t, vw_t, patterns)


def _oproj_kernel(o_ref, x_ref, mod_ref, w_ref, g_ref, b_ref, out_ref):
    y = _dot(o_ref[0], w_ref[...])
    out_ref[0] = _layer_norm(ALPHA * x_ref[0] + mod_ref[0, 2:3, :] * y, g_ref[...], b_ref[...])


def _out_projection(o, x, mod, w_o, ln_g, ln_b):
    bsz, s, d = x.shape
    tm = 512
    return pl.pallas_call(
        _oproj_kernel,
        grid=(bsz, s // tm),
        in_specs=[
            pl.BlockSpec((1, tm, o.shape[-1]), lambda b, j: (b, j, 0)),
            pl.BlockSpec((1, tm, d), lambda b, j: (b, j, 0)),
            pl.BlockSpec((1, 6, d), lambda b, j: (b, 0, 0)),
            _const_spec(w_o.shape),
            _const_spec((1, d)),
            _const_spec((1, d)),
        ],
        out_specs=pl.BlockSpec((1, tm, d), lambda b, j: (b, j, 0)),
        out_shape=jax.ShapeDtypeStruct((bsz, s, d), F32),
        compiler_params=_params("parallel", "parallel"),
        name="nsa_out_proj",
    )(o, x, mod, w_o, ln_g.reshape(1, d), ln_b.reshape(1, d))


def _moe_block(x, mod, router_wt, router_bias, w_gu, w_down, ln_g, ln_b):
    bsz, s, d = x.shape
    t = bsz * s
    xf = x.reshape(t, d)
    wcol, route, counts = _router(xf, mod, router_wt, router_bias, s)

    cnt = counts[:, 0]
    padded = (cnt + EXPERT_TILE - 1) // EXPERT_TILE * EXPERT_TILE
    ends = jnp.cumsum(padded)
    offs = ends - padded
    n_tiles = (TOP_K * t) // EXPERT_TILE + N_EXPERTS
    tile_start = jnp.arange(n_tiles, dtype=jnp.int32) * EXPERT_TILE
    tile_expert = jnp.minimum(jnp.sum(tile_start[:, None] >= ends[None, :], axis=1), N_EXPERTS - 1).astype(jnp.int32)
    n_used = (ends[-1:] // EXPERT_TILE).astype(jnp.int32)
    experts, ranks = route[:TOP_K], route[TOP_K:2 * TOP_K]
    dst = ranks + sum(jnp.where(experts == e, offs[e], 0) for e in range(N_EXPERTS))
    idx = dst.reshape(TOP_K, t // TOKEN_TILE, TOKEN_TILE).transpose(1, 0, 2).reshape(t // TOKEN_TILE, 1, TOP_K * TOKEN_TILE)

    last_tile = jnp.where(cnt > 0, ends - EXPERT_TILE, -1)
    spare = n_used + jnp.arange(N_EXPERTS)
    spare = jnp.where(spare < n_tiles, spare * EXPERT_TILE, -1)
    xs = _dispatch(jnp.concatenate([last_tile, spare]).astype(jnp.int32), idx, xf, mod, n_tiles * EXPERT_TILE, s)
    ys = _experts(tile_expert, n_used, xs, w_gu, w_down)
    out = _combine(idx, xf, mod, wcol, ys, ln_g, ln_b, s)
    return out.reshape(bsz, s, d)


def _nsa_layer(x, mod, w_kv, cmp_pe, cmp_w1, cmp_b1, cmp_w2, cmp_b2, w_qg, b_g, w_o, ln_g, ln_b):
    bsz, s, d = x.shape
    hkv, dh, grp = N_KV_HEADS, HEAD_DIM, GQA_GROUP
    hw = hkv * dh
    kvw = w_kv.reshape(d, 6, hw)
    kvc_w = jnp.stack([kvw[:, 0].reshape(d, hkv, dh), kvw[:, 1].reshape(d, hkv, dh)], axis=2).reshape(d, 2 * hw)
    w_nat = jnp.concatenate([kvc_w, kvw[:, 2], kvw[:, 4]], axis=1).astype(BF16)
    w_vt = jnp.concatenate([kvw[:, 3], kvw[:, 5]], axis=1).T.astype(BF16)
    nq = N_HEADS * dh
    wg = w_qg[:, nq:].reshape(d, hkv, grp, 3).transpose(0, 1, 3, 2).reshape(d, 3 * N_HEADS)
    w_qgt = jnp.concatenate([w_qg[:, :nq], wg], axis=1).T.astype(BF16)
    bg = b_g.reshape(hkv, grp, 3).transpose(0, 2, 1).reshape(3 * N_HEADS, 1)
    kvc, ks_aug, kw_aug, vs_t, vw_t, q_t, gates_t = _projections(x, mod, w_nat, w_vt, w_qgt, bg)
    cmp_kv, cmp_vt = _compress(kvc, cmp_pe, cmp_w1, cmp_b1, cmp_w2, cmp_b2)
    o = _attention(q_t, gates_t, cmp_kv, cmp_vt, ks_aug, kw_aug, vs_t, vw_t)
    return _out_projection(o, x, mod, w_o.astype(BF16), ln_g, ln_b)


def kernel(x, c, ada_w, ada_b, ln_g, ln_b, conv_w_in, conv_w, conv_b, conv_w_out, w_kv, cmp_pe, cmp_w1, cmp_b1, cmp_w2, cmp_b2, w_qg, b_g, w_o, router_w, router_bias, w_gu, w_down):
    bsz, s, d = x.shape
    mod = _modulation(c, ada_w, ada_b).reshape(DEPTH, bsz, 6, d)
    router_wt = router_w.T

    x = _conv_layer(x, mod[0], conv_w_in[0].astype(BF16), conv_w[0], conv_b[0], conv_w_out[0].astype(BF16),
                    ln_g[0, 0], ln_b[0, 0])
    x = _moe_block(x, mod[0], router_wt, router_bias, w_gu[0], w_down[0], ln_g[0, 1], ln_b[0, 1])

    x = _nsa_layer(x, mod[1], w_kv, cmp_pe, cmp_w1, cmp_b1, cmp_w2, cmp_b2, w_qg[0], b_g[0], w_o[0],
                   ln_g[1, 0], ln_b[1, 0])
    x = _moe_block(x, mod[1], router_wt, router_bias, w_gu[1], w_down[1], ln_g[1, 1], ln_b[1, 1])
    return x
```

```python
import functools
import math

import jax
import jax.numpy as jnp
import numpy as np
from jax import lax
from jax.experimental import pallas as pl
from jax.experimental.pallas import tpu as pltpu

F32 = jnp.float32
BF16 = jnp.bfloat16

D_MODEL = 1024
DEPTH = 2
N_A_LAYERS = DEPTH // 2
CONV_WIDTH = 3
N_HEADS = 16
HEAD_DIM = D_MODEL // N_HEADS
N_KV_HEADS = 4
GQA_GROUP = N_HEADS // N_KV_HEADS
CMP_BLOCK = 32
CMP_STRIDE = 16
CMP_HIDDEN = 4 * HEAD_DIM
SEL_BLOCK = 64
SEL_TOP = 16
WINDOW = 512
N_EXPERTS = 16
N_GROUPS = 4
EXPERTS_PER_GROUP = N_EXPERTS // N_GROUPS
TOP_K = 2
D_FF_EXPERT = D_MODEL // 2
ALPHA = (2 * DEPTH) ** 0.25
LN_EPS = 1e-5
NEG = -1e30
FORCE = 1e9

LANES = 128
VMEM_LIMIT_BYTES = 56 * 1024 * 1024

LOG2E = math.log2(math.e)

ATT_Q = 256
KEY_TILE = 256
PAT_ZERO, PAT_DIAG, PAT_ANTI, PAT_NONE = range(4)


def _dot(a, b):
    return jnp.dot(a, b, preferred_element_type=F32)


def _dot_nt(a, b):
    return lax.dot_general(a, b, (((1,), (1,)), ((), ())), preferred_element_type=F32)


def _split(x):
    hi = x.astype(BF16)
    lo = (x - hi.astype(F32)).astype(BF16)
    return hi, lo


def _dot3(a, b):
    ah, al = _split(a)
    bh, bl = _split(b)
    return _dot(ah, bh) + (_dot(ah, bl) + _dot(al, bh))


def _dot3_nt(a, b):
    ah, al = _split(a)
    bh, bl = _split(b)
    return _dot_nt(ah, bh) + (_dot_nt(ah, bl) + _dot_nt(al, bh))


def _layer_norm(r, g, b):
    mu = jnp.mean(r, axis=-1, keepdims=True)
    d = r - mu
    var = jnp.mean(d * d, axis=-1, keepdims=True)
    return d * lax.rsqrt(var + LN_EPS) * g + b


def _params(*sem):
    return pltpu.CompilerParams(dimension_semantics=sem, vmem_limit_bytes=VMEM_LIMIT_BYTES)


def _const_spec(shape):
    zeros = (0,) * len(shape)
    return pl.BlockSpec(shape, lambda *_: zeros)


def _mod_kernel(c_ref, w_ref, b_ref, o_ref):
    c = c_ref[...]
    s = c * jax.nn.sigmoid(c)
    o_ref[0] = _dot3(s, w_ref[0]) + b_ref[0]


def _modulation(c, ada_w, ada_b):
    depth, d, n = ada_w.shape
    bsz = c.shape[0]
    tn = 1536
    return pl.pallas_call(
        _mod_kernel,
        grid=(depth, n // tn),
        in_specs=[
            pl.BlockSpec((bsz, d), lambda l, j: (0, 0)),
            pl.BlockSpec((1, d, tn), lambda l, j: (l, 0, j)),
            pl.BlockSpec((1, 1, tn), lambda l, j: (l, 0, j)),
        ],
        out_specs=pl.BlockSpec((1, bsz, tn), lambda l, j: (l, 0, j)),
        out_shape=jax.ShapeDtypeStruct((depth, bsz, n), F32),
        compiler_params=_params("parallel", "parallel"),
        name="adaln_mod",
    )(c, ada_w, ada_b.reshape(depth, 1, n))


CONV_HALO = 8


def _conv_kernel(x_ref, mod_ref, win_ref, cw_ref, cb_ref, wout_ref, g_ref, b_ref, o_ref, z_ref):
    tm = x_ref.shape[1]
    d = x_ref.shape[2]

    @pl.when(pl.program_id(1) == 0)
    def _():
        z_ref[0:CONV_HALO, :] = jnp.zeros((CONV_HALO, d), F32)

    x = x_ref[0]
    sh = mod_ref[0, 0:1, :]
    sc = mod_ref[0, 1:2, :]
    gate = mod_ref[0, 2:3, :]
    u = (x * (1.0 + sc) + sh).astype(BF16)
    bch = _dot(u, win_ref[...])
    z = bch[:, d:2 * d] * bch[:, 2 * d:]
    z_ref[CONV_HALO:CONV_HALO + tm, :] = z
    z1 = z_ref[CONV_HALO - 1:CONV_HALO - 1 + tm, :]
    z2 = z_ref[CONV_HALO - 2:CONV_HALO - 2 + tm, :]
    conv = cw_ref[0:1, :] * z2 + cw_ref[1:2, :] * z1 + cw_ref[2:3, :] * z + cb_ref[...]
    v = (bch[:, :d] * conv).astype(BF16)
    y = _dot(v, wout_ref[...])
    o_ref[0] = _layer_norm(ALPHA * x + gate * y, g_ref[...], b_ref[...])
    z_ref[0:CONV_HALO, :] = z_ref[tm:tm + CONV_HALO, :]


def _conv_layer(x, mod, w_in, conv_w, conv_b, w_out, ln_g, ln_b):
    bsz, s, d = x.shape
    tm = 512
    return pl.pallas_call(
        _conv_kernel,
        grid=(bsz, s // tm),
        in_specs=[
            pl.BlockSpec((1, tm, d), lambda b, j: (b, j, 0)),
            pl.BlockSpec((1, 6, d), lambda b, j: (b, 0, 0)),
            _const_spec((d, 3 * d)),
            _const_spec((CONV_WIDTH, d)),
            _const_spec((1, d)),
            _const_spec((d, d)),
            _const_spec((1, d)),
            _const_spec((1, d)),
        ],
        out_specs=pl.BlockSpec((1, tm, d), lambda b, j: (b, j, 0)),
        out_shape=jax.ShapeDtypeStruct((bsz, s, d), F32),
        scratch_shapes=[pltpu.VMEM((tm + CONV_HALO, d), F32)],
        compiler_params=_params("arbitrary", "arbitrary"),
        name="conv_mixer",
    )(x, mod, w_in, conv_w, conv_b.reshape(1, d), w_out, ln_g.reshape(1, d), ln_b.reshape(1, d))


def _router_kernel(x_ref, mod_ref, rwt_ref, rb_ref, wcol_ref, route_ref, count_ref, base_ref, tri_ref):
    tm = x_ref.shape[0]
    x = x_ref[...]
    u = x * (1.0 + mod_ref[0, 4:5, :]) + mod_ref[0, 3:4, :]
    logits = _dot3_nt(rwt_ref[...], u)
    m = jnp.max(logits, axis=0, keepdims=True)
    e = jnp.exp(logits - m)
    aff = e / jnp.sum(e, axis=0, keepdims=True)
    biased = aff + rb_ref[...]
    aff_r = [aff[i:i + 1, :] for i in range(N_EXPERTS)]
    row = [biased[i:i + 1, :] for i in range(N_EXPERTS)]

    best_s, best = None, None
    for g in range(N_GROUPS):
        r = row[g * EXPERTS_PER_GROUP:(g + 1) * EXPERTS_PER_GROUP]
        gs = None
        for i in range(EXPERTS_PER_GROUP):
            for j in range(i + 1, EXPERTS_PER_GROUP):
                p = r[i] + r[j]
                gs = p if gs is None else jnp.maximum(gs, p)
        if g == 0:
            best_s, best = gs, jnp.zeros((1, tm), jnp.int32)
        else:
            upd = gs > best_s
            best = jnp.where(upd, g, best)
            best_s = jnp.where(upd, gs, best_s)

    masked = [jnp.where(best == (i // EXPERTS_PER_GROUP), row[i], NEG) for i in range(N_EXPERTS)]

    def first_argmax(vals):
        v, idx = vals[0], jnp.zeros((1, tm), jnp.int32)
        for i in range(1, N_EXPERTS):
            upd = vals[i] > v
            idx = jnp.where(upd, i, idx)
            v = jnp.where(upd, vals[i], v)
        return idx

    i0 = first_argmax(masked)
    i1 = first_argmax([jnp.where(i0 == i, -jnp.inf, masked[i]) for i in range(N_EXPERTS)])
    w0 = sum(jnp.where(i0 == i, aff_r[i], 0.0) for i in range(N_EXPERTS))
    w1 = sum(jnp.where(i1 == i, aff_r[i], 0.0) for i in range(N_EXPERTS))
    tot = w0 + w1
    w0 = w0 / tot
    w1 = w1 / tot
    w_t = jnp.concatenate([w0, w1, jnp.zeros((LANES - TOP_K, tm), F32)], axis=0)
    wcol_ref[...] = w_t.T

    @pl.when(pl.program_id(0) == 0)
    def _():
        base_ref[...] = jnp.zeros(base_ref.shape, F32)
        r_i = lax.broadcasted_iota(jnp.int32, tri_ref.shape, 0)
        c_i = lax.broadcasted_iota(jnp.int32, tri_ref.shape, 1)
        tri_ref[...] = jnp.where(r_i <= c_i, 1.0, 0.0).astype(BF16)

    hot = jnp.concatenate([jnp.where((i0 == i) | (i1 == i), 1.0, 0.0) for i in range(N_EXPERTS)], axis=0)
    incl = _dot(hot.astype(BF16), tri_ref[...])
    pos = base_ref[...] + (incl - hot)
    rank0 = sum(jnp.where(i0 == i, pos[i:i + 1, :], 0.0) for i in range(N_EXPERTS))
    rank1 = sum(jnp.where(i1 == i, pos[i:i + 1, :], 0.0) for i in range(N_EXPERTS))
    base = base_ref[...] + jnp.sum(hot, axis=1, keepdims=True)
    base_ref[...] = base
    route_ref[...] = jnp.concatenate(
        [i0, i1, rank0.astype(jnp.int32), rank1.astype(jnp.int32), jnp.zeros((4, tm), jnp.int32)], axis=0)
    count_ref[...] = jnp.broadcast_to(base, count_ref.shape).astype(jnp.int32)


def _router(x, mod, router_wt, router_bias, seq):
    t, d = x.shape
    tm = 1024
    return pl.pallas_call(
        _router_kernel,
        grid=(t // tm,),
        in_specs=[
            pl.BlockSpec((tm, d), lambda i: (i, 0)),
            pl.BlockSpec((1, 6, d), lambda i: ((i * tm) // seq, 0, 0)),
            _const_spec((N_EXPERTS, d)),
            _const_spec((N_EXPERTS, 1)),
        ],
        out_specs=(
            pl.BlockSpec((tm, LANES), lambda i: (i, 0)),
            pl.BlockSpec((8, tm), lambda i: (0, i)),
            _const_spec((N_EXPERTS, LANES)),
        ),
        out_shape=(
            jax.ShapeDtypeStruct((t, LANES), F32),
            jax.ShapeDtypeStruct((8, t), jnp.int32),
            jax.ShapeDtypeStruct((N_EXPERTS, LANES), jnp.int32),
        ),
        scratch_shapes=[pltpu.VMEM((N_EXPERTS, 1), F32), pltpu.VMEM((tm, tm), BF16)],
        compiler_params=_params("arbitrary"),
        name="moe_router",
    )(x, mod, router_wt, router_bias.reshape(N_EXPERTS, 1))


EXPERT_TILE = 512
TOKEN_TILE = 512
ROW_COPY_UNROLL = 8
ROW_TILES = D_MODEL // LANES


def _store_rows(ref, x):
    rows = x.shape[0]
    for c in range(ROW_TILES):
        ref[pl.ds(c, rows, stride=ROW_TILES), :] = x[:, c * LANES:(c + 1) * LANES]


def _load_rows(ref, rows):
    return jnp.concatenate([ref[pl.ds(c, rows, stride=ROW_TILES), :] for c in range(ROW_TILES)], axis=1)


def _row(ref, r):
    return ref.at[pl.ds(pl.multiple_of(r * ROW_TILES, ROW_TILES), ROW_TILES)]


def _row_copies(tm, row_copy, tile_copy):
    def start(t, c):
        for k in range(TOP_K):
            row_copy(t, k).start(priority=k % 2)
        return c

    lax.fori_loop(0, tm, start, 0, unroll=ROW_COPY_UNROLL)
    for k in range(TOP_K):
        tile_copy(k).wait()


def _dispatch_kernel(pad_ref, idx_ref, x_ref, mod_ref, xs_ref, u_ref, sem):
    tm = x_ref.shape[0]

    @pl.when(pl.program_id(0) == 0)
    def _():
        u_ref[...] = jnp.zeros(u_ref.shape, F32)
        for wait in (False, True):
            for e in range(pad_ref.shape[0]):
                @pl.when(pad_ref[e] >= 0)
                def _():
                    row = pl.multiple_of(jnp.maximum(pad_ref[e], 0) * ROW_TILES, tm * ROW_TILES)
                    fill = pltpu.make_async_copy(u_ref, xs_ref.at[pl.ds(row, tm * ROW_TILES)], sem)
                    fill.wait() if wait else fill.start()

    _store_rows(u_ref, x_ref[...] * (1.0 + mod_ref[0, 4:5, :]) + mod_ref[0, 3:4, :])

    def row_copy(t, k):
        dst = idx_ref[0, 0, k * tm + t]
        return pltpu.make_async_copy(_row(u_ref, t), _row(xs_ref, dst), sem)

    _row_copies(tm, row_copy, lambda k: pltpu.make_async_copy(u_ref, xs_ref.at[pl.ds(0, tm * ROW_TILES)], sem))


def _dispatch(pad_start, idx, x, mod, n_rows, seq):
    t, d = x.shape
    tm = TOKEN_TILE
    assert tm == EXPERT_TILE
    return pl.pallas_call(
        _dispatch_kernel,
        grid_spec=pltpu.PrefetchScalarGridSpec(
            num_scalar_prefetch=1,
            grid=(t // tm,),
            in_specs=[
                pl.BlockSpec((1, 1, TOP_K * tm), lambda i, pad: (i, 0, 0), memory_space=pltpu.SMEM),
                pl.BlockSpec((tm, d), lambda i, pad: (i, 0)),
                pl.BlockSpec((1, 6, d), lambda i, pad: ((i * tm) // seq, 0, 0)),
            ],
            out_specs=pl.BlockSpec(memory_space=pl.ANY),
            scratch_shapes=[pltpu.VMEM((tm * ROW_TILES, LANES), F32), pltpu.SemaphoreType.DMA],
        ),
        out_shape=jax.ShapeDtypeStruct((n_rows * ROW_TILES, LANES), F32),
        compiler_params=_params("arbitrary"),
        name="moe_dispatch",
    )(pad_start, idx, x, mod)


def _expert_kernel(te_ref, nu_ref, xs_ref, wgu_ref, wdn_ref, ys_ref):
    del te_ref
    used = pl.program_id(0) < nu_ref[0]

    @pl.when(used)
    def _():
        f = wdn_ref.shape[2]
        h = _dot(_load_rows(xs_ref, EXPERT_TILE).astype(BF16), wgu_ref[0, 0].astype(BF16))
        gate = h[:, :f]
        a = (gate * jax.nn.sigmoid(gate) * h[:, f:]).astype(BF16)
        _store_rows(ys_ref, _dot(a, wdn_ref[0, 0].astype(BF16)))

    @pl.when(jnp.logical_not(used))
    def _():
        ys_ref[...] = jnp.zeros(ys_ref.shape, F32)


def _experts(tile_expert, n_used, xs, w_gu, w_down, layer):
    _, _, d, f2 = w_gu.shape
    f = f2 // 2
    row_map = lambda s, te, nu: (s, 0)
    in_row_map = lambda s, te, nu: (jnp.minimum(s, nu[0]), 0)
    w_map = lambda s, te, nu: (layer, te[s], 0, 0)
    return pl.pallas_call(
        _expert_kernel,
        grid_spec=pltpu.PrefetchScalarGridSpec(
            num_scalar_prefetch=2,
            grid=(tile_expert.shape[0],),
            in_specs=[
                pl.BlockSpec((EXPERT_TILE * ROW_TILES, LANES), in_row_map),
                pl.BlockSpec((1, 1, d, f2), w_map),
                pl.BlockSpec((1, 1, f, d), w_map),
            ],
            out_specs=pl.BlockSpec((EXPERT_TILE * ROW_TILES, LANES), row_map),
        ),
        out_shape=jax.ShapeDtypeStruct(xs.shape, F32),
        compiler_params=_params("arbitrary"),
        name="moe_experts",
    )(tile_expert, n_used, xs, w_gu, w_down)


def _combine_kernel(idx_ref, x_ref, mod_ref, w_ref, ys_ref, g_ref, b_ref, o_ref, y_ref, sem):
    tm = x_ref.shape[0]

    def row_copy(t, k):
        src = idx_ref[0, 0, k * tm + t]
        return pltpu.make_async_copy(_row(ys_ref, src), _row(y_ref.at[k], t), sem)

    _row_copies(tm, row_copy,
                lambda k: pltpu.make_async_copy(ys_ref.at[pl.ds(0, tm * ROW_TILES)], y_ref.at[k], sem))
    w = w_ref[...]
    out = w[:, 0:1] * _load_rows(y_ref.at[0], tm) + w[:, 1:2] * _load_rows(y_ref.at[1], tm)
    r = ALPHA * x_ref[...] + mod_ref[0, 5:6, :] * out
    o_ref[...] = _layer_norm(r, g_ref[...], b_ref[...])


def _combine(idx, x, mod, wcol, ys, ln_g, ln_b, seq):
    t, d = x.shape
    tm = TOKEN_TILE
    return pl.pallas_call(
        _combine_kernel,
        grid=(t // tm,),
        in_specs=[
            pl.BlockSpec((1, 1, TOP_K * tm), lambda i: (i, 0, 0), memory_space=pltpu.SMEM),
            pl.BlockSpec((tm, d), lambda i: (i, 0)),
            pl.BlockSpec((1, 6, d), lambda i: ((i * tm) // seq, 0, 0)),
            pl.BlockSpec((tm, LANES), lambda i: (i, 0)),
            pl.BlockSpec(memory_space=pl.ANY),
            _const_spec((1, d)),
            _const_spec((1, d)),
        ],
        out_specs=pl.BlockSpec((tm, d), lambda i: (i, 0)),
        out_shape=jax.ShapeDtypeStruct((t, d), F32),
        scratch_shapes=[pltpu.VMEM((TOP_K, tm * ROW_TILES, LANES), F32), pltpu.SemaphoreType.DMA],
        compiler_params=_params("arbitrary"),
        name="moe_combine",
    )(idx, x, mod, wcol, ys, ln_g.reshape(1, d), ln_b.reshape(1, d))


def _proj_kernel(x_ref, mod_ref, wn_ref, wvt_ref, wqt_ref, bg_ref,
                 kvc_ref, ks_ref, kw_ref, vst_ref, vwt_ref, qt_ref, gt_ref):
    tm = x_ref.shape[1]
    j = pl.program_id(1)
    x = x_ref[0]
    xb = x.astype(BF16)
    nat = _dot(xb, wn_ref[...])
    vt = _dot_nt(wvt_ref[...], xb)
    u = (x * (1.0 + mod_ref[0, 1:2, :]) + mod_ref[0, 0:1, :]).astype(BF16)
    qg = _dot_nt(wqt_ref[...], u)
    nq = N_HEADS * HEAD_DIM
    qt_ref[0] = (qg[:nq, :] * (HEAD_DIM ** -0.5 * LOG2E)).astype(BF16)
    gates = jax.nn.sigmoid(qg[nq:, :] + bg_ref[...])
    per = 3 * GQA_GROUP
    blk = (j * tm + lax.broadcasted_iota(jnp.int32, (tm, SEL_BLOCK), 0)) // SEL_BLOCK
    onehot = (blk == lax.broadcasted_iota(jnp.int32, (tm, SEL_BLOCK), 1)).astype(BF16)
    zeros = jnp.zeros((tm, SEL_BLOCK), BF16)
    hw = N_KV_HEADS * HEAD_DIM
    for h in range(N_KV_HEADS):
        c0 = h * HEAD_DIM
        kvc_ref[0, h] = nat[:, 2 * c0:2 * c0 + 2 * HEAD_DIM]
        ks = nat[:, 2 * hw + c0:2 * hw + c0 + HEAD_DIM].astype(BF16)
        ks_ref[0, h] = jnp.concatenate([ks, onehot], axis=1)
        kw = nat[:, 3 * hw + c0:3 * hw + c0 + HEAD_DIM].astype(BF16)
        kw_ref[0, h] = jnp.concatenate([kw, zeros], axis=1)
        vst_ref[0, h] = vt[c0:c0 + HEAD_DIM, :].astype(BF16)
        vwt_ref[0, h] = vt[hw + c0:hw + c0 + HEAD_DIM, :].astype(BF16)
        gt_ref[0, h] = gates[h * per:(h + 1) * per, :]


def _projections(x, mod, w_nat, w_vt, w_qgt, b_g):
    bsz, s, d = x.shape
    tm = 512
    hkv, dh = N_KV_HEADS, HEAD_DIM
    per = 3 * GQA_GROUP
    nqg = w_qgt.shape[0]
    out_shape = (
        jax.ShapeDtypeStruct((bsz, hkv, s, 2 * dh), F32),
        jax.ShapeDtypeStruct((bsz, hkv, s, 2 * dh), BF16),
        jax.ShapeDtypeStruct((bsz, hkv, s, 2 * dh), BF16),
        jax.ShapeDtypeStruct((bsz, hkv, dh, s), BF16),
        jax.ShapeDtypeStruct((bsz, hkv, dh, s), BF16),
        jax.ShapeDtypeStruct((bsz, N_HEADS * dh, s), BF16),
        jax.ShapeDtypeStruct((bsz, hkv, per, s), F32),
    )
    out_specs = (
        pl.BlockSpec((1, hkv, tm, 2 * dh), lambda b, j: (b, 0, j, 0)),
        pl.BlockSpec((1, hkv, tm, 2 * dh), lambda b, j: (b, 0, j, 0)),
        pl.BlockSpec((1, hkv, tm, 2 * dh), lambda b, j: (b, 0, j, 0)),
        pl.BlockSpec((1, hkv, dh, tm), lambda b, j: (b, 0, 0, j)),
        pl.BlockSpec((1, hkv, dh, tm), lambda b, j: (b, 0, 0, j)),
        pl.BlockSpec((1, N_HEADS * dh, tm), lambda b, j: (b, 0, j)),
        pl.BlockSpec((1, hkv, per, tm), lambda b, j: (b, 0, 0, j)),
    )
    return pl.pallas_call(
        _proj_kernel,
        grid=(bsz, s // tm),
        in_specs=[
            pl.BlockSpec((1, tm, d), lambda b, j: (b, j, 0)),
            pl.BlockSpec((1, 6, d), lambda b, j: (b, 0, 0)),
            _const_spec(w_nat.shape),
            _const_spec(w_vt.shape),
            _const_spec(w_qgt.shape),
            _const_spec((nqg - N_HEADS * dh, 1)),
        ],
        out_specs=out_specs,
        out_shape=out_shape,
        compiler_params=_params("parallel", "parallel"),
        name="nsa_projections",
    )(x, mod, w_nat, w_vt, w_qgt, b_g)


def _compress_kernel(kvc_ref, pe_ref, w1_ref, b1_ref, w2_ref, b2_ref, nat_ref, tr_ref):
    n = kvc_ref.shape[2] // CMP_STRIDE
    hid2 = w1_ref.shape[-1]
    p = jnp.zeros((n, hid2), F32)
    q = jnp.zeros((n, hid2), F32)
    for l in range(CMP_STRIDE):
        x = kvc_ref[0, 0, pl.ds(l, n, stride=CMP_STRIDE), :]
        p = p + _dot((x + pe_ref[l:l + 1, :]).astype(BF16), w1_ref[0, l])
        q = q + _dot((x + pe_ref[CMP_STRIDE + l:CMP_STRIDE + l + 1, :]).astype(BF16), w1_ref[1, l])
    pre = p + pltpu.roll(q, n - 1, 0) + b1_ref[...]
    hdn = 0.5 * pre * (1.0 + jnp.tanh(0.7978845608028654 * (pre + 0.044715 * (pre * pre * pre))))
    out = _dot(hdn.astype(BF16), w2_ref[...]) + b2_ref[...]
    nat_ref[0, 0] = out.astype(BF16)
    tr_ref[0, 0] = out[:, HEAD_DIM:].T.astype(BF16)


def _compress(kvc, cmp_pe, cmp_w1, cmp_b1, cmp_w2, cmp_b2):
    bsz, hkv, s, two_dh = kvc.shape
    dh = two_dh // 2
    n = s // CMP_STRIDE
    hid = cmp_w1.shape[-1]
    zw = jnp.zeros((2, CMP_STRIDE, dh, hid), F32)
    w1 = cmp_w1.reshape(2, 2, CMP_STRIDE, dh, hid)
    w1 = jnp.concatenate([jnp.concatenate([w1[0], zw], axis=-1), jnp.concatenate([zw, w1[1]], axis=-1)], axis=-2)
    zd = jnp.zeros((hid, dh), F32)
    w2 = jnp.concatenate([jnp.concatenate([cmp_w2[0], zd], axis=1), jnp.concatenate([zd, cmp_w2[1]], axis=1)], axis=0)
    pe = jnp.concatenate([cmp_pe[0], cmp_pe[1]], axis=1)
    return pl.pallas_call(
        _compress_kernel,
        grid=(bsz, hkv),
        in_specs=[
            pl.BlockSpec((1, 1, s, two_dh), lambda b, h: (b, h, 0, 0)),
            _const_spec(pe.shape),
            _const_spec(w1.shape),
            _const_spec((1, 2 * hid)),
            _const_spec(w2.shape),
            _const_spec((1, two_dh)),
        ],
        out_specs=(
            pl.BlockSpec((1, 1, n, two_dh), lambda b, h: (b, h, 0, 0)),
            pl.BlockSpec((1, 1, dh, n), lambda b, h: (b, h, 0, 0)),
        ),
        out_shape=(
            jax.ShapeDtypeStruct((bsz, hkv, n, two_dh), BF16),
            jax.ShapeDtypeStruct((bsz, hkv, dh, n), BF16),
        ),
        compiler_params=_params("parallel", "parallel"),
        name="nsa_compress",
    )(kvc, pe, w1.astype(BF16), cmp_b1.reshape(1, 2 * hid), w2.astype(BF16), cmp_b2.reshape(1, two_dh))


def _mask_patterns():
    assert WINDOW == 2 * KEY_TILE and ATT_Q == KEY_TILE
    keyl = np.arange(KEY_TILE)[:, None]
    ql = (np.arange(GQA_GROUP * ATT_Q) % ATT_Q)[None, :]
    true = np.ones((KEY_TILE, GQA_GROUP * ATT_Q), bool)
    valid = np.stack([
        true,
        keyl <= ql,
        keyl > ql,
        ~true,
    ])
    return jnp.asarray(np.where(valid, 0.0, NEG), F32)


def _attn_kernel(q_ref, g_ref, kcv_ref, vct_ref, ks_ref, kw_ref, vs_ref, vw_ref, pat_ref, o_ref,
                 kall_ref, vt_ref, sc_ref, qaug_ref, s0_ref, s1_ref, p0_ref, p1_ref, al_ref, mt_ref,
                 m_ref, l_ref, acc_ref):
    i = pl.program_id(2)
    seq = ks_ref.shape[2]
    nq = GQA_GROUP * ATT_Q
    dh = HEAD_DIM
    n_key_tiles = seq // KEY_TILE

    @pl.when(i == 0)
    def _():
        kall_ref[0:seq, :] = ks_ref[0, 0]
        kall_ref[seq:2 * seq, :] = kw_ref[0, 0]
        for n in range(n_key_tiles):
            vt_ref[n] = vs_ref[0, 0, :, n * KEY_TILE:(n + 1) * KEY_TILE]
            vt_ref[n_key_tiles + n] = vw_ref[0, 0, :, n * KEY_TILE:(n + 1) * KEY_TILE]

    qa = q_ref[0]
    qt = jnp.concatenate([qa[g * dh:(g + 1) * dh, :] for g in range(GQA_GROUP)], axis=1)
    t_row = i * ATT_Q + lax.broadcasted_iota(jnp.int32, (1, ATT_Q), 1)
    t4 = jnp.concatenate([t_row] * GQA_GROUP, axis=1)

    n_cmp = kcv_ref.shape[2]
    q_c = jnp.concatenate([qt, jnp.zeros((kcv_ref.shape[3] - dh, nq), BF16)], axis=0)
    s_c = _dot(kcv_ref[0, 0], q_c)
    cmp_end = lax.broadcasted_iota(jnp.int32, (n_cmp, 1), 0) * CMP_STRIDE + (CMP_BLOCK - 1)
    vis = cmp_end <= t4
    s_c = jnp.where(vis, s_c, NEG)
    m_c = jnp.max(s_c, axis=0, keepdims=True)
    e_c = jnp.where(vis, jnp.exp2(s_c - m_c), 0.0)
    p_c = e_c / jnp.maximum(jnp.sum(e_c, axis=0, keepdims=True), 1e-30)
    o_c = _dot(vct_ref[0, 0], p_c.astype(BF16))

    n_sel = seq // SEL_BLOCK
    p_sum = sum(p_c[:, g * ATT_Q:(g + 1) * ATT_Q] for g in range(GQA_GROUP))
    jj = lax.broadcasted_iota(jnp.int32, (n_sel, n_cmp), 0) * SEL_BLOCK
    nn = lax.broadcasted_iota(jnp.int32, (n_sel, n_cmp), 1) * CMP_STRIDE
    ov = jnp.minimum(nn + CMP_BLOCK, jj + SEL_BLOCK) - jnp.maximum(nn, jj)
    ov_t = (jnp.maximum(ov, 0).astype(F32) * (1.0 / CMP_BLOCK)).astype(BF16)
    p_hi, p_lo = _split(p_sum)
    imp = _dot(ov_t, p_hi) + _dot(ov_t, p_lo)
    jb = lax.broadcasted_iota(jnp.int32, (n_sel, 1), 0)
    cur = t_row // SEL_BLOCK
    allowed = jb <= cur
    forced = (jb == 0) | (jb == cur) | (jb == cur - 1)
    score = jnp.where(forced & allowed, FORCE, jnp.where(allowed, imp, NEG))
    sc_ref[...] = score
    n_top = min(SEL_TOP, n_sel)
    sub = 8
    groups = [score[r:r + sub, :] for r in range(0, n_sel, sub)]
    ranks = [jnp.zeros((sub, ATT_Q), F32) for _ in groups]
    for k in range(n_sel):
        row = jnp.broadcast_to(sc_ref[k:k + 1, :], (sub, ATT_Q))
        for r, grp in enumerate(groups):
            if r * sub > k:
                ahead = row >= grp
            elif r * sub + sub - 1 < k:
                ahead = row > grp
            else:
                later = (r * sub + lax.broadcasted_iota(jnp.int32, (sub, 1), 0)) > k
                ahead = (row > grp) | ((row == grp) & later)
            ranks[r] = ranks[r] + jnp.where(ahead, 1.0, 0.0)
    rank = jnp.concatenate(ranks, axis=0)
    chosen = (rank < n_top) & (score > 0.5 * NEG)
    bias = jnp.where(chosen, 0.0, NEG).astype(BF16)
    pad = kall_ref.shape[1] - dh - n_sel
    if pad:
        bias = jnp.concatenate([bias, jnp.zeros((pad, ATT_Q), BF16)], axis=0)
    qaug_ref[...] = jnp.concatenate([qt, jnp.concatenate([bias] * GQA_GROUP, axis=1)], axis=0)

    n_s = i + 1
    n_jobs = n_s + 1 + (i >= 1).astype(jnp.int32) + (i >= 2).astype(jnp.int32)

    def job(j):
        is_sel = j < n_s
        valid = (j >= 0) & (j < n_jobs)
        w = j - n_s
        tile = jnp.where(is_sel, j, n_key_tiles + i - w)
        pat_sel = jnp.where(j == i, PAT_DIAG, PAT_ZERO)
        pat_win = jnp.where(w == 0, PAT_DIAG, jnp.where(w == 1, PAT_ZERO, PAT_ANTI))
        pat = jnp.where(valid, jnp.where(is_sel, pat_sel, pat_win), PAT_NONE)
        return jnp.where(valid, tile, 0), pat, jnp.where(is_sel, 0, 1)

    def stage_a(j, s_ref, slot, masked):
        tile, pat, _ = job(j)
        k = kall_ref[pl.ds(pl.multiple_of(tile * KEY_TILE, KEY_TILE), KEY_TILE), :]
        s = _dot(k, qaug_ref[...])
        if masked:
            s = s + pat_ref[pat]
        s_ref[...] = s
        mt_ref[slot] = jnp.max(s, axis=0, keepdims=True)

    def stage_b(j, s_ref, p_ref, slot):
        a = job(j)[2]
        m_old = m_ref[a]
        m_new = jnp.maximum(m_old, mt_ref[slot])
        alpha = jnp.exp2(m_old - m_new)
        p = jnp.exp2(s_ref[...] - m_new)
        l_ref[a] = alpha * l_ref[a] + jnp.sum(p, axis=0, keepdims=True)
        m_ref[a] = m_new
        al_ref[slot] = alpha
        p_ref[...] = p.astype(BF16)

    def stage_c(j, p_ref, slot):
        tile, _, a = job(j)
        acc_ref[a] = al_ref[slot] * acc_ref[a] + _dot(vt_ref[tile], p_ref[...])

    s0_ref[...] = jnp.full(s0_ref.shape, NEG, F32)
    s1_ref[...] = jnp.full(s1_ref.shape, NEG, F32)
    mt_ref[...] = jnp.full(mt_ref.shape, NEG, F32)
    p0_ref[...] = jnp.zeros(p0_ref.shape, BF16)
    p1_ref[...] = jnp.zeros(p1_ref.shape, BF16)
    al_ref[...] = jnp.ones(al_ref.shape, F32)
    m_ref[...] = jnp.full(m_ref.shape, NEG, F32)
    l_ref[...] = jnp.zeros(l_ref.shape, F32)
    acc_ref[...] = jnp.zeros(acc_ref.shape, F32)

    def two_steps(jj, carry, masked):
        j = 2 * jj
        stage_c(j - 4, p0_ref, 0)
        stage_c(j - 3, p1_ref, 1)
        stage_b(j - 2, s0_ref, p0_ref, 0)
        stage_b(j - 1, s1_ref, p1_ref, 1)
        stage_a(j, s0_ref, 0, masked)
        stage_a(j + 1, s1_ref, 1, masked)
        return carry

    n_plain = i // 2
    lax.fori_loop(0, n_plain, functools.partial(two_steps, masked=False), 0)
    lax.fori_loop(n_plain, (n_jobs + 5) // 2, functools.partial(two_steps, masked=True), 0)
    o_s = acc_ref[0] / l_ref[0]
    o_w = acc_ref[1] / l_ref[1]

    gates = g_ref[0, 0]
    def gate_row(br):
        return jnp.concatenate([gates[br * GQA_GROUP + g:br * GQA_GROUP + g + 1, :] for g in range(GQA_GROUP)], axis=1)
    o = gate_row(0) * o_c + gate_row(1) * o_s + gate_row(2) * o_w
    o_ref[0] = jnp.concatenate([o[:, g * ATT_Q:(g + 1) * ATT_Q].T for g in range(GQA_GROUP)], axis=1).astype(BF16)


def _attention(q_t, gates_t, kcv, vc_t, ks_aug, kw_aug, vs_t, vw_t):
    bsz, hkv, s, kdim = ks_aug.shape
    dh = vs_t.shape[2]
    n_cmp = kcv.shape[2]
    gd = GQA_GROUP * dh
    nq = GQA_GROUP * ATT_Q
    patterns = _mask_patterns()
    per_head = lambda shape: pl.BlockSpec((1, 1) + shape, lambda b, h, i: (b, h, 0, 0))
    return pl.pallas_call(
        _attn_kernel,
        grid=(bsz, hkv, s // ATT_Q),
        in_specs=[
            pl.BlockSpec((1, gd, ATT_Q), lambda b, h, i: (b, h, i)),
            pl.BlockSpec((1, 1, 3 * GQA_GROUP, ATT_Q), lambda b, h, i: (b, h, 0, i)),
            per_head((n_cmp, kcv.shape[3])),
            per_head((dh, n_cmp)),
            per_head((s, kdim)),
            per_head((s, kdim)),
            per_head((dh, s)),
            per_head((dh, s)),
            _const_spec(patterns.shape),
        ],
        out_specs=pl.BlockSpec((1, ATT_Q, gd), lambda b, h, i: (b, i, h)),
        out_shape=jax.ShapeDtypeStruct((bsz, s, hkv * gd), BF16),
        scratch_shapes=[
            pltpu.VMEM((2 * s, kdim), BF16),
            pltpu.VMEM((2 * (s // KEY_TILE), dh, KEY_TILE), BF16),
            pltpu.VMEM((s // SEL_BLOCK, ATT_Q), F32),
            pltpu.VMEM((kdim, nq), BF16),
            pltpu.VMEM((KEY_TILE, nq), F32),
            pltpu.VMEM((KEY_TILE, nq), F32),
            pltpu.VMEM((KEY_TILE, nq), BF16),
            pltpu.VMEM((KEY_TILE, nq), BF16),
            pltpu.VMEM((2, 1, nq), F32),
            pltpu.VMEM((2, 1, nq), F32),
            pltpu.VMEM((2, 1, nq), F32),
            pltpu.VMEM((2, 1, nq), F32),
            pltpu.VMEM((2, dh, nq), F32),
        ],
        compiler_params=_params("parallel", "parallel", "arbitrary"),
        name="nsa_attention",
    )(q_t, gates_t, kcv, vc_t, ks_aug, kw_aug, vs_t, vw_t, patterns)


def _oproj_kernel(o_ref, x_ref, mod_ref, w_ref, g_ref, b_ref, out_ref):
    y = _dot(o_ref[0], w_ref[...])
    out_ref[0] = _layer_norm(ALPHA * x_ref[0] + mod_ref[0, 2:3, :] * y, g_ref[...], b_ref[...])


def _out_projection(o, x, mod, w_o, ln_g, ln_b):
    bsz, s, d = x.shape
    tm = 512
    return pl.pallas_call(
        _oproj_kernel,
        grid=(bsz, s // tm),
        in_specs=[
            pl.BlockSpec((1, tm, o.shape[-1]), lambda b, j: (b, j, 0)),
            pl.BlockSpec((1, tm, d), lambda b, j: (b, j, 0)),
            pl.BlockSpec((1, 6, d), lambda b, j: (b, 0, 0)),
            _const_spec(w_o.shape),
            _const_spec((1, d)),
            _const_spec((1, d)),
        ],
        out_specs=pl.BlockSpec((1, tm, d), lambda b, j: (b, j, 0)),
        out_shape=jax.ShapeDtypeStruct((bsz, s, d), F32),
        compiler_params=_params("parallel", "parallel"),
        name="nsa_out_proj",
    )(o, x, mod, w_o, ln_g.reshape(1, d), ln_b.reshape(1, d))


def _moe_block(x, mod, router_wt, router_bias, w_gu, w_down, layer, ln_g, ln_b):
    bsz, s, d = x.shape
    t = bsz * s
    xf = x.reshape(t, d)
    wcol, route, counts = _router(xf, mod, router_wt, router_bias, s)

    cnt = counts[:, 0]
    padded = (cnt + EXPERT_TILE - 1) // EXPERT_TILE * EXPERT_TILE
    ends = jnp.cumsum(padded)
    offs = ends - padded
    n_tiles = (TOP_K * t) // EXPERT_TILE + N_EXPERTS
    tile_start = jnp.arange(n_tiles, dtype=jnp.int32) * EXPERT_TILE
    tile_expert = jnp.minimum(jnp.sum(tile_start[:, None] >= ends[None, :], axis=1), N_EXPERTS - 1).astype(jnp.int32)
    n_used = (ends[-1:] // EXPERT_TILE).astype(jnp.int32)
    experts, ranks = route[:TOP_K], route[TOP_K:2 * TOP_K]
    dst = ranks + sum(jnp.where(experts == e, offs[e], 0) for e in range(N_EXPERTS))
    idx = dst.reshape(TOP_K, t // TOKEN_TILE, TOKEN_TILE).transpose(1, 0, 2).reshape(t // TOKEN_TILE, 1, TOP_K * TOKEN_TILE)

    last_tile = jnp.where(cnt > 0, ends - EXPERT_TILE, -1)
    spare = n_used + jnp.arange(N_EXPERTS)
    spare = jnp.where(spare < n_tiles, spare * EXPERT_TILE, -1)
    xs = _dispatch(jnp.concatenate([last_tile, spare]).astype(jnp.int32), idx, xf, mod, n_tiles * EXPERT_TILE, s)
    ys = _experts(tile_expert, n_used, xs, w_gu, w_down, layer)
    out = _combine(idx, xf, mod, wcol, ys, ln_g, ln_b, s)
    return out.reshape(bsz, s, d)


def _nsa_layer(x, mod, w_kv, cmp_pe, cmp_w1, cmp_b1, cmp_w2, cmp_b2, w_qg, b_g, w_o, ln_g, ln_b):
    bsz, s, d = x.shape
    hkv, dh, grp = N_KV_HEADS, HEAD_DIM, GQA_GROUP
    hw = hkv * dh
    kvw = w_kv.reshape(d, 6, hw)
    kvc_w = jnp.stack([kvw[:, 0].reshape(d, hkv, dh), kvw[:, 1].reshape(d, hkv, dh)], axis=2).reshape(d, 2 * hw)
    w_nat = jnp.concatenate([kvc_w, kvw[:, 2], kvw[:, 4]], axis=1).astype(BF16)
    w_vt = jnp.concatenate([kvw[:, 3], kvw[:, 5]], axis=1).T.astype(BF16)
    nq = N_HEADS * dh
    wg = w_qg[:, nq:].reshape(d, hkv, grp, 3).transpose(0, 1, 3, 2).reshape(d, 3 * N_HEADS)
    w_qgt = jnp.concatenate([w_qg[:, :nq], wg], axis=1).T.astype(BF16)
    bg = b_g.reshape(hkv, grp, 3).transpose(0, 2, 1).reshape(3 * N_HEADS, 1)
    kvc, ks_aug, kw_aug, vs_t, vw_t, q_t, gates_t = _projections(x, mod, w_nat, w_vt, w_qgt, bg)
    cmp_kv, cmp_vt = _compress(kvc, cmp_pe, cmp_w1, cmp_b1, cmp_w2, cmp_b2)
    o = _attention(q_t, gates_t, cmp_kv, cmp_vt, ks_aug, kw_aug, vs_t, vw_t)
    return _out_projection(o, x, mod, w_o.astype(BF16), ln_g, ln_b)


def kernel(x, c, ada_w, ada_b, ln_g, ln_b, conv_w_in, conv_w, conv_b, conv_w_out, w_kv, cmp_pe, cmp_w1, cmp_b1, cmp_w2, cmp_b2, w_qg, b_g, w_o, router_w, router_bias, w_gu, w_down):
    bsz, s, d = x.shape
    mod = _modulation(c, ada_w, ada_b).reshape(DEPTH, bsz, 6, d)
    router_wt = router_w.T

    x = _conv_layer(x, mod[0], conv_w_in[0].astype(BF16), conv_w[0], conv_b[0], conv_w_out[0].astype(BF16),
                    ln_g[0, 0], ln_b[0, 0])
    x = _moe_block(x, mod[0], router_wt, router_bias, w_gu, w_down, 0, ln_g[0, 1], ln_b[0, 1])

    x = _nsa_layer(x, mod[1], w_kv, cmp_pe, cmp_w1, cmp_b1, cmp_w2, cmp_b2, w_qg[0], b_g[0], w_o[0],
                   ln_g[1, 0], ln_b[1, 0])
    x = _moe_block(x, mod[1], router_wt, router_bias, w_gu, w_down, 1, ln_g[1, 1], ln_b[1, 1])
    return x
```

```python
import functools
import math

import jax
import jax.numpy as jnp
import numpy as np
from jax import lax
from jax.experimental import pallas as pl
from jax.experimental.pallas import tpu as pltpu

F32 = jnp.float32
BF16 = jnp.bfloat16

D_MODEL = 1024
DEPTH = 2
N_A_LAYERS = DEPTH // 2
CONV_WIDTH = 3
N_HEADS = 16
HEAD_DIM = D_MODEL // N_HEADS
N_KV_HEADS = 4
GQA_GROUP = N_HEADS // N_KV_HEADS
CMP_BLOCK = 32
CMP_STRIDE = 16
CMP_HIDDEN = 4 * HEAD_DIM
SEL_BLOCK = 64
SEL_TOP = 16
WINDOW = 512
N_EXPERTS = 16
N_GROUPS = 4
EXPERTS_PER_GROUP = N_EXPERTS // N_GROUPS
TOP_K = 2
D_FF_EXPERT = D_MODEL // 2
ALPHA = (2 * DEPTH) ** 0.25
LN_EPS = 1e-5
NEG = -1e30
FORCE = 1e9

LANES = 128
VMEM_LIMIT_BYTES = 56 * 1024 * 1024

LOG2E = math.log2(math.e)

ATT_Q = 256
KEY_TILE = 256
PAT_ZERO, PAT_DIAG, PAT_ANTI, PAT_NONE = range(4)
ATT_UNROLL = 4
ACC_RING = 8
EPILOGUE_LAG = 3
RANK_BUCKET = 16
VT_ROWS = HEAD_DIM + 16


def _dot(a, b):
    return jnp.dot(a, b, preferred_element_type=F32)


def _dot_nt(a, b):
    return lax.dot_general(a, b, (((1,), (1,)), ((), ())), preferred_element_type=F32)


def _split(x):
    hi = x.astype(BF16)
    lo = (x - hi.astype(F32)).astype(BF16)
    return hi, lo


def _dot3(a, b):
    ah, al = _split(a)
    bh, bl = _split(b)
    return _dot(ah, bh) + (_dot(ah, bl) + _dot(al, bh))


def _dot3_nt(a, b):
    ah, al = _split(a)
    bh, bl = _split(b)
    return _dot_nt(ah, bh) + (_dot_nt(ah, bl) + _dot_nt(al, bh))


def _layer_norm(r, g, b):
    mu = jnp.mean(r, axis=-1, keepdims=True)
    d = r - mu
    var = jnp.mean(d * d, axis=-1, keepdims=True)
    return d * lax.rsqrt(var + LN_EPS) * g + b


def _params(*sem):
    return pltpu.CompilerParams(dimension_semantics=sem, vmem_limit_bytes=VMEM_LIMIT_BYTES)


def _const_spec(shape):
    zeros = (0,) * len(shape)
    return pl.BlockSpec(shape, lambda *_: zeros)


def _mod_kernel(c_ref, w_ref, b_ref, o_ref):
    c = c_ref[...]
    s = c * jax.nn.sigmoid(c)
    o_ref[0] = _dot3(s, w_ref[0]) + b_ref[0]


def _modulation(c, ada_w, ada_b):
    depth, d, n = ada_w.shape
    bsz = c.shape[0]
    tn = 1536
    return pl.pallas_call(
        _mod_kernel,
        grid=(depth, n // tn),
        in_specs=[
            pl.BlockSpec((bsz, d), lambda l, j: (0, 0)),
            pl.BlockSpec((1, d, tn), lambda l, j: (l, 0, j)),
            pl.BlockSpec((1, 1, tn), lambda l, j: (l, 0, j)),
        ],
        out_specs=pl.BlockSpec((1, bsz, tn), lambda l, j: (l, 0, j)),
        out_shape=jax.ShapeDtypeStruct((depth, bsz, n), F32),
        compiler_params=_params("parallel", "parallel"),
        name="adaln_mod",
    )(c, ada_w, ada_b.reshape(depth, 1, n))


CONV_HALO = 8


def _conv_kernel(x_ref, mod_ref, win_ref, cw_ref, cb_ref, wout_ref, g_ref, b_ref, o_ref, z_ref):
    tm = x_ref.shape[1]
    d = x_ref.shape[2]

    @pl.when(pl.program_id(1) == 0)
    def _():
        z_ref[0:CONV_HALO, :] = jnp.zeros((CONV_HALO, d), F32)

    x = x_ref[0]
    sh = mod_ref[0, 0:1, :]
    sc = mod_ref[0, 1:2, :]
    gate = mod_ref[0, 2:3, :]
    u = (x * (1.0 + sc) + sh).astype(BF16)
    bch = _dot(u, win_ref[...])
    z = bch[:, d:2 * d] * bch[:, 2 * d:]
    z_ref[CONV_HALO:CONV_HALO + tm, :] = z
    z1 = z_ref[CONV_HALO - 1:CONV_HALO - 1 + tm, :]
    z2 = z_ref[CONV_HALO - 2:CONV_HALO - 2 + tm, :]
    conv = cw_ref[0:1, :] * z2 + cw_ref[1:2, :] * z1 + cw_ref[2:3, :] * z + cb_ref[...]
    v = (bch[:, :d] * conv).astype(BF16)
    y = _dot(v, wout_ref[...])
    o_ref[0] = _layer_norm(ALPHA * x + gate * y, g_ref[...], b_ref[...])
    z_ref[0:CONV_HALO, :] = z_ref[tm:tm + CONV_HALO, :]


def _conv_layer(x, mod, w_in, conv_w, conv_b, w_out, ln_g, ln_b):
    bsz, s, d = x.shape
    tm = 512
    return pl.pallas_call(
        _conv_kernel,
        grid=(bsz, s // tm),
        in_specs=[
            pl.BlockSpec((1, tm, d), lambda b, j: (b, j, 0)),
            pl.BlockSpec((1, 6, d), lambda b, j: (b, 0, 0)),
            _const_spec((d, 3 * d)),
            _const_spec((CONV_WIDTH, d)),
            _const_spec((1, d)),
            _const_spec((d, d)),
            _const_spec((1, d)),
            _const_spec((1, d)),
        ],
        out_specs=pl.BlockSpec((1, tm, d), lambda b, j: (b, j, 0)),
        out_shape=jax.ShapeDtypeStruct((bsz, s, d), F32),
        scratch_shapes=[pltpu.VMEM((tm + CONV_HALO, d), F32)],
        compiler_params=_params("arbitrary", "arbitrary"),
        name="conv_mixer",
    )(x, mod, w_in, conv_w, conv_b.reshape(1, d), w_out, ln_g.reshape(1, d), ln_b.reshape(1, d))


def _router_kernel(x_ref, mod_ref, rwt_ref, rb_ref, wcol_ref, route_ref, count_ref, base_ref, tri_ref):
    tm = x_ref.shape[0]
    x = x_ref[...]
    u = x * (1.0 + mod_ref[0, 4:5, :]) + mod_ref[0, 3:4, :]
    logits = _dot3_nt(rwt_ref[...], u)
    m = jnp.max(logits, axis=0, keepdims=True)
    e = jnp.exp(logits - m)
    aff = e / jnp.sum(e, axis=0, keepdims=True)
    biased = aff + rb_ref[...]
    aff_r = [aff[i:i + 1, :] for i in range(N_EXPERTS)]
    row = [biased[i:i + 1, :] for i in range(N_EXPERTS)]

    best_s, best = None, None
    for g in range(N_GROUPS):
        r = row[g * EXPERTS_PER_GROUP:(g + 1) * EXPERTS_PER_GROUP]
        gs = None
        for i in range(EXPERTS_PER_GROUP):
            for j in range(i + 1, EXPERTS_PER_GROUP):
                p = r[i] + r[j]
                gs = p if gs is None else jnp.maximum(gs, p)
        if g == 0:
            best_s, best = gs, jnp.zeros((1, tm), jnp.int32)
        else:
            upd = gs > best_s
            best = jnp.where(upd, g, best)
            best_s = jnp.where(upd, gs, best_s)

    masked = [jnp.where(best == (i // EXPERTS_PER_GROUP), row[i], NEG) for i in range(N_EXPERTS)]

    def first_argmax(vals):
        v, idx = vals[0], jnp.zeros((1, tm), jnp.int32)
        for i in range(1, N_EXPERTS):
            upd = vals[i] > v
            idx = jnp.where(upd, i, idx)
            v = jnp.where(upd, vals[i], v)
        return idx

    i0 = first_argmax(masked)
    i1 = first_argmax([jnp.where(i0 == i, -jnp.inf, masked[i]) for i in range(N_EXPERTS)])
    w0 = sum(jnp.where(i0 == i, aff_r[i], 0.0) for i in range(N_EXPERTS))
    w1 = sum(jnp.where(i1 == i, aff_r[i], 0.0) for i in range(N_EXPERTS))
    tot = w0 + w1
    w0 = w0 / tot
    w1 = w1 / tot
    w_t = jnp.concatenate([w0, w1, jnp.zeros((LANES - TOP_K, tm), F32)], axis=0)
    wcol_ref[...] = w_t.T

    @pl.when(pl.program_id(0) == 0)
    def _():
        base_ref[...] = jnp.zeros(base_ref.shape, F32)
        r_i = lax.broadcasted_iota(jnp.int32, tri_ref.shape, 0)
        c_i = lax.broadcasted_iota(jnp.int32, tri_ref.shape, 1)
        tri_ref[...] = jnp.where(r_i <= c_i, 1.0, 0.0).astype(BF16)

    hot = jnp.concatenate([jnp.where((i0 == i) | (i1 == i), 1.0, 0.0) for i in range(N_EXPERTS)], axis=0)
    incl = _dot(hot.astype(BF16), tri_ref[...])
    pos = base_ref[...] + (incl - hot)
    rank0 = sum(jnp.where(i0 == i, pos[i:i + 1, :], 0.0) for i in range(N_EXPERTS))
    rank1 = sum(jnp.where(i1 == i, pos[i:i + 1, :], 0.0) for i in range(N_EXPERTS))
    base = base_ref[...] + jnp.sum(hot, axis=1, keepdims=True)
    base_ref[...] = base
    route_ref[...] = jnp.concatenate(
        [i0, i1, rank0.astype(jnp.int32), rank1.astype(jnp.int32), jnp.zeros((4, tm), jnp.int32)], axis=0)
    count_ref[...] = jnp.broadcast_to(base, count_ref.shape).astype(jnp.int32)


def _router(x, mod, router_wt, router_bias, seq):
    t, d = x.shape
    tm = 1024
    return pl.pallas_call(
        _router_kernel,
        grid=(t // tm,),
        in_specs=[
            pl.BlockSpec((tm, d), lambda i: (i, 0)),
            pl.BlockSpec((1, 6, d), lambda i: ((i * tm) // seq, 0, 0)),
            _const_spec((N_EXPERTS, d)),
            _const_spec((N_EXPERTS, 1)),
        ],
        out_specs=(
            pl.BlockSpec((tm, LANES), lambda i: (i, 0)),
            pl.BlockSpec((8, tm), lambda i: (0, i)),
            _const_spec((N_EXPERTS, LANES)),
        ),
        out_shape=(
            jax.ShapeDtypeStruct((t, LANES), F32),
            jax.ShapeDtypeStruct((8, t), jnp.int32),
            jax.ShapeDtypeStruct((N_EXPERTS, LANES), jnp.int32),
        ),
        scratch_shapes=[pltpu.VMEM((N_EXPERTS, 1), F32), pltpu.VMEM((tm, tm), BF16)],
        compiler_params=_params("arbitrary"),
        name="moe_router",
    )(x, mod, router_wt, router_bias.reshape(N_EXPERTS, 1))


EXPERT_TILE = 512
TOKEN_TILE = 512
ROW_COPY_UNROLL = 8
ROW_TILES = D_MODEL // LANES


def _store_rows(ref, x):
    rows = x.shape[0]
    for c in range(ROW_TILES):
        ref[pl.ds(c, rows, stride=ROW_TILES), :] = x[:, c * LANES:(c + 1) * LANES]


def _load_rows(ref, rows):
    return jnp.concatenate([ref[pl.ds(c, rows, stride=ROW_TILES), :] for c in range(ROW_TILES)], axis=1)


def _row(ref, r):
    return ref.at[pl.ds(pl.multiple_of(r * ROW_TILES, ROW_TILES), ROW_TILES)]


def _row_copies(tm, row_copy, tile_copy):
    def start(t, c):
        for k in range(TOP_K):
            row_copy(t, k).start(priority=k % 2)
        return c

    lax.fori_loop(0, tm, start, 0, unroll=ROW_COPY_UNROLL)
    for k in range(TOP_K):
        tile_copy(k).wait()


def _dispatch_kernel(pad_ref, idx_ref, x_ref, mod_ref, xs_ref, u_ref, sem):
    tm = x_ref.shape[0]

    @pl.when(pl.program_id(0) == 0)
    def _():
        u_ref[...] = jnp.zeros(u_ref.shape, F32)
        for wait in (False, True):
            for e in range(pad_ref.shape[0]):
                @pl.when(pad_ref[e] >= 0)
                def _():
                    row = pl.multiple_of(jnp.maximum(pad_ref[e], 0) * ROW_TILES, tm * ROW_TILES)
                    fill = pltpu.make_async_copy(u_ref, xs_ref.at[pl.ds(row, tm * ROW_TILES)], sem)
                    fill.wait() if wait else fill.start()

    _store_rows(u_ref, x_ref[...] * (1.0 + mod_ref[0, 4:5, :]) + mod_ref[0, 3:4, :])

    def row_copy(t, k):
        dst = idx_ref[0, 0, k * tm + t]
        return pltpu.make_async_copy(_row(u_ref, t), _row(xs_ref, dst), sem)

    _row_copies(tm, row_copy, lambda k: pltpu.make_async_copy(u_ref, xs_ref.at[pl.ds(0, tm * ROW_TILES)], sem))


def _dispatch(pad_start, idx, x, mod, n_rows, seq):
    t, d = x.shape
    tm = TOKEN_TILE
    assert tm == EXPERT_TILE
    return pl.pallas_call(
        _dispatch_kernel,
        grid_spec=pltpu.PrefetchScalarGridSpec(
            num_scalar_prefetch=1,
            grid=(t // tm,),
            in_specs=[
                pl.BlockSpec((1, 1, TOP_K * tm), lambda i, pad: (i, 0, 0), memory_space=pltpu.SMEM),
                pl.BlockSpec((tm, d), lambda i, pad: (i, 0)),
                pl.BlockSpec((1, 6, d), lambda i, pad: ((i * tm) // seq, 0, 0)),
            ],
            out_specs=pl.BlockSpec(memory_space=pl.ANY),
            scratch_shapes=[pltpu.VMEM((tm * ROW_TILES, LANES), F32), pltpu.SemaphoreType.DMA],
        ),
        out_shape=jax.ShapeDtypeStruct((n_rows * ROW_TILES, LANES), F32),
        compiler_params=_params("arbitrary"),
        name="moe_dispatch",
    )(pad_start, idx, x, mod)


def _expert_kernel(te_ref, nu_ref, xs_ref, wgu_ref, wdn_ref, ys_ref):
    del te_ref
    used = pl.program_id(0) < nu_ref[0]

    @pl.when(used)
    def _():
        f = wdn_ref.shape[2]
        h = _dot(_load_rows(xs_ref, EXPERT_TILE).astype(BF16), wgu_ref[0, 0].astype(BF16))
        gate = h[:, :f]
        a = (gate * jax.nn.sigmoid(gate) * h[:, f:]).astype(BF16)
        _store_rows(ys_ref, _dot(a, wdn_ref[0, 0].astype(BF16)))

    @pl.when(jnp.logical_not(used))
    def _():
        ys_ref[...] = jnp.zeros(ys_ref.shape, F32)


def _experts(tile_expert, n_used, xs, w_gu, w_down, layer):
    _, _, d, f2 = w_gu.shape
    f = f2 // 2
    row_map = lambda s, te, nu: (s, 0)
    in_row_map = lambda s, te, nu: (jnp.minimum(s, nu[0]), 0)
    w_map = lambda s, te, nu: (layer, te[s], 0, 0)
    return pl.pallas_call(
        _expert_kernel,
        grid_spec=pltpu.PrefetchScalarGridSpec(
            num_scalar_prefetch=2,
            grid=(tile_expert.shape[0],),
            in_specs=[
                pl.BlockSpec((EXPERT_TILE * ROW_TILES, LANES), in_row_map),
                pl.BlockSpec((1, 1, d, f2), w_map),
                pl.BlockSpec((1, 1, f, d), w_map),
            ],
            out_specs=pl.BlockSpec((EXPERT_TILE * ROW_TILES, LANES), row_map),
        ),
        out_shape=jax.ShapeDtypeStruct(xs.shape, F32),
        compiler_params=_params("arbitrary"),
        name="moe_experts",
    )(tile_expert, n_used, xs, w_gu, w_down)


def _combine_kernel(idx_ref, x_ref, mod_ref, w_ref, ys_ref, g_ref, b_ref, o_ref, y_ref, sem):
    tm = x_ref.shape[0]

    def row_copy(t, k):
        src = idx_ref[0, 0, k * tm + t]
        return pltpu.make_async_copy(_row(ys_ref, src), _row(y_ref.at[k], t), sem)

    _row_copies(tm, row_copy,
                lambda k: pltpu.make_async_copy(ys_ref.at[pl.ds(0, tm * ROW_TILES)], y_ref.at[k], sem))
    w = w_ref[...]
    out = w[:, 0:1] * _load_rows(y_ref.at[0], tm) + w[:, 1:2] * _load_rows(y_ref.at[1], tm)
    r = ALPHA * x_ref[...] + mod_ref[0, 5:6, :] * out
    o_ref[...] = _layer_norm(r, g_ref[...], b_ref[...])


def _combine(idx, x, mod, wcol, ys, ln_g, ln_b, seq):
    t, d = x.shape
    tm = TOKEN_TILE
    return pl.pallas_call(
        _combine_kernel,
        grid=(t // tm,),
        in_specs=[
            pl.BlockSpec((1, 1, TOP_K * tm), lambda i: (i, 0, 0), memory_space=pltpu.SMEM),
            pl.BlockSpec((tm, d), lambda i: (i, 0)),
            pl.BlockSpec((1, 6, d), lambda i: ((i * tm) // seq, 0, 0)),
            pl.BlockSpec((tm, LANES), lambda i: (i, 0)),
            pl.BlockSpec(memory_space=pl.ANY),
            _const_spec((1, d)),
            _const_spec((1, d)),
        ],
        out_specs=pl.BlockSpec((tm, d), lambda i: (i, 0)),
        out_shape=jax.ShapeDtypeStruct((t, d), F32),
        scratch_shapes=[pltpu.VMEM((TOP_K, tm * ROW_TILES, LANES), F32), pltpu.SemaphoreType.DMA],
        compiler_params=_params("arbitrary"),
        name="moe_combine",
    )(idx, x, mod, wcol, ys, ln_g.reshape(1, d), ln_b.reshape(1, d))


def _proj_kernel(x_ref, mod_ref, wn_ref, wvt_ref, wqt_ref, bg_ref,
                 kvc_ref, ks_ref, kw_ref, vst_ref, vwt_ref, qt_ref, gt_ref):
    tm = x_ref.shape[1]
    j = pl.program_id(1)
    x = x_ref[0]
    xb = x.astype(BF16)
    nat = _dot(xb, wn_ref[...])
    vt = _dot_nt(wvt_ref[...], xb)
    u = (x * (1.0 + mod_ref[0, 1:2, :]) + mod_ref[0, 0:1, :]).astype(BF16)
    qg = _dot_nt(wqt_ref[...], u)
    nq = N_HEADS * HEAD_DIM
    q = (qg[:nq, :] * (HEAD_DIM ** -0.5 * LOG2E)).astype(BF16)
    gates = jax.nn.sigmoid(qg[nq:, :] + bg_ref[...])
    per = 3 * GQA_GROUP
    blk = (j * tm + lax.broadcasted_iota(jnp.int32, (tm, SEL_BLOCK), 0)) // SEL_BLOCK
    onehot = (blk == lax.broadcasted_iota(jnp.int32, (tm, SEL_BLOCK), 1)).astype(BF16)
    zeros = jnp.zeros((tm, SEL_BLOCK), BF16)
    hw = N_KV_HEADS * HEAD_DIM
    for h in range(N_KV_HEADS):
        c0 = h * HEAD_DIM
        kvc_ref[0, h] = nat[:, 2 * c0:2 * c0 + 2 * HEAD_DIM]
        ks = nat[:, 2 * hw + c0:2 * hw + c0 + HEAD_DIM].astype(BF16)
        ks_ref[0, h] = jnp.concatenate([ks, onehot], axis=1)
        kw = nat[:, 3 * hw + c0:3 * hw + c0 + HEAD_DIM].astype(BF16)
        kw_ref[0, h] = jnp.concatenate([kw, zeros], axis=1)
        vst_ref[0, h] = vt[c0:c0 + HEAD_DIM, :].astype(BF16)
        vwt_ref[0, h] = vt[hw + c0:hw + c0 + HEAD_DIM, :].astype(BF16)
        for p in range(tm // ATT_Q):
            cols = slice(p * ATT_Q, (p + 1) * ATT_Q)
            qt_ref[0, h, p] = q[h * GQA_GROUP * HEAD_DIM:(h + 1) * GQA_GROUP * HEAD_DIM, cols]
            gt_ref[0, h, p] = gates[h * per:(h + 1) * per, cols]


def _projections(x, mod, w_nat, w_vt, w_qgt, b_g):
    bsz, s, d = x.shape
    tm = 512
    hkv, dh = N_KV_HEADS, HEAD_DIM
    per = 3 * GQA_GROUP
    nqg = w_qgt.shape[0]
    out_shape = (
        jax.ShapeDtypeStruct((bsz, hkv, s, 2 * dh), F32),
        jax.ShapeDtypeStruct((bsz, hkv, s, 2 * dh), BF16),
        jax.ShapeDtypeStruct((bsz, hkv, s, 2 * dh), BF16),
        jax.ShapeDtypeStruct((bsz, hkv, dh, s), BF16),
        jax.ShapeDtypeStruct((bsz, hkv, dh, s), BF16),
        jax.ShapeDtypeStruct((bsz, hkv, s // ATT_Q, GQA_GROUP * dh, ATT_Q), BF16),
        jax.ShapeDtypeStruct((bsz, hkv, s // ATT_Q, per, ATT_Q), F32),
    )
    out_specs = (
        pl.BlockSpec((1, hkv, tm, 2 * dh), lambda b, j: (b, 0, j, 0)),
        pl.BlockSpec((1, hkv, tm, 2 * dh), lambda b, j: (b, 0, j, 0)),
        pl.BlockSpec((1, hkv, tm, 2 * dh), lambda b, j: (b, 0, j, 0)),
        pl.BlockSpec((1, hkv, dh, tm), lambda b, j: (b, 0, 0, j)),
        pl.BlockSpec((1, hkv, dh, tm), lambda b, j: (b, 0, 0, j)),
        pl.BlockSpec((1, hkv, tm // ATT_Q, GQA_GROUP * dh, ATT_Q), lambda b, j: (b, 0, j, 0, 0)),
        pl.BlockSpec((1, hkv, tm // ATT_Q, per, ATT_Q), lambda b, j: (b, 0, j, 0, 0)),
    )
    return pl.pallas_call(
        _proj_kernel,
        grid=(bsz, s // tm),
        in_specs=[
            pl.BlockSpec((1, tm, d), lambda b, j: (b, j, 0)),
            pl.BlockSpec((1, 6, d), lambda b, j: (b, 0, 0)),
            _const_spec(w_nat.shape),
            _const_spec(w_vt.shape),
            _const_spec(w_qgt.shape),
            _const_spec((nqg - N_HEADS * dh, 1)),
        ],
        out_specs=out_specs,
        out_shape=out_shape,
        compiler_params=_params("parallel", "parallel"),
        name="nsa_projections",
    )(x, mod, w_nat, w_vt, w_qgt, b_g)


def _compress_kernel(kvc_ref, pe_ref, w1_ref, b1_ref, w2_ref, b2_ref, nat_ref, tr_ref):
    n = kvc_ref.shape[2] // CMP_STRIDE
    hid2 = w1_ref.shape[-1]
    p = jnp.zeros((n, hid2), F32)
    q = jnp.zeros((n, hid2), F32)
    for l in range(CMP_STRIDE):
        x = kvc_ref[0, 0, pl.ds(l, n, stride=CMP_STRIDE), :]
        p = p + _dot((x + pe_ref[l:l + 1, :]).astype(BF16), w1_ref[0, l])
        q = q + _dot((x + pe_ref[CMP_STRIDE + l:CMP_STRIDE + l + 1, :]).astype(BF16), w1_ref[1, l])
    pre = p + pltpu.roll(q, n - 1, 0) + b1_ref[...]
    hdn = 0.5 * pre * (1.0 + jnp.tanh(0.7978845608028654 * (pre + 0.044715 * (pre * pre * pre))))
    out = _dot(hdn.astype(BF16), w2_ref[...]) + b2_ref[...]
    nat_ref[0, 0] = out.astype(BF16)
    tr_ref[0, 0] = out[:, HEAD_DIM:].T.astype(BF16)


def _compress(kvc, cmp_pe, cmp_w1, cmp_b1, cmp_w2, cmp_b2):
    bsz, hkv, s, two_dh = kvc.shape
    dh = two_dh // 2
    n = s // CMP_STRIDE
    hid = cmp_w1.shape[-1]
    zw = jnp.zeros((2, CMP_STRIDE, dh, hid), F32)
    w1 = cmp_w1.reshape(2, 2, CMP_STRIDE, dh, hid)
    w1 = jnp.concatenate([jnp.concatenate([w1[0], zw], axis=-1), jnp.concatenate([zw, w1[1]], axis=-1)], axis=-2)
    zd = jnp.zeros((hid, dh), F32)
    w2 = jnp.concatenate([jnp.concatenate([cmp_w2[0], zd], axis=1), jnp.concatenate([zd, cmp_w2[1]], axis=1)], axis=0)
    pe = jnp.concatenate([cmp_pe[0], cmp_pe[1]], axis=1)
    return pl.pallas_call(
        _compress_kernel,
        grid=(bsz, hkv),
        in_specs=[
            pl.BlockSpec((1, 1, s, two_dh), lambda b, h: (b, h, 0, 0)),
            _const_spec(pe.shape),
            _const_spec(w1.shape),
            _const_spec((1, 2 * hid)),
            _const_spec(w2.shape),
            _const_spec((1, two_dh)),
        ],
        out_specs=(
            pl.BlockSpec((1, 1, n, two_dh), lambda b, h: (b, h, 0, 0)),
            pl.BlockSpec((1, 1, dh, n), lambda b, h: (b, h, 0, 0)),
        ),
        out_shape=(
            jax.ShapeDtypeStruct((bsz, hkv, n, two_dh), BF16),
            jax.ShapeDtypeStruct((bsz, hkv, dh, n), BF16),
        ),
        compiler_params=_params("parallel", "parallel"),
        name="nsa_compress",
    )(kvc, pe, w1.astype(BF16), cmp_b1.reshape(1, 2 * hid), w2.astype(BF16), cmp_b2.reshape(1, two_dh))


def _mask_patterns():
    assert WINDOW == 2 * KEY_TILE and ATT_Q == KEY_TILE
    keyl = np.arange(KEY_TILE)[:, None]
    ql = (np.arange(GQA_GROUP * ATT_Q) % ATT_Q)[None, :]
    true = np.ones((KEY_TILE, GQA_GROUP * ATT_Q), bool)
    valid = np.stack([
        true,
        keyl <= ql,
        keyl > ql,
        ~true,
    ])
    return jnp.asarray(np.where(valid, 0.0, NEG), F32)


def _job_tables(seq):
    nqb, nkt, u = seq // ATT_Q, seq // KEY_TILE, ATT_UNROLL
    tile, pat, acc, qs, base = [], [], [], [], []
    for i in range(nqb):
        base.append(len(tile))
        for j in range(i + 1):
            tile.append(j)
            pat.append(PAT_DIAG if j == i else PAT_ZERO)
            acc.append(2 * (i % ACC_RING))
            qs.append(i % 2)
        for w in range(min(i, WINDOW // KEY_TILE) + 1):
            tile.append(nkt + i - w)
            pat.append((PAT_DIAG, PAT_ZERO, PAT_ANTI)[w])
            acc.append(2 * (i % ACC_RING) + 1)
            qs.append(i % 2)
    n_real = len(tile)
    n_steps = (n_real - 1 + 2 * u) // u + 1
    pad = lambda x, fill: np.array([fill] * (2 * u) + x + [fill] * (n_steps * u - n_real), np.int32)
    step_lo = np.array([-(-b // u) for b in base] + [n_steps], np.int32)
    assert all(base[i + 2] >= step_lo[i + 1] * u for i in range(nqb - 2))
    return pad(tile, 0), pad(pat, PAT_NONE), pad(acc, 2 * ACC_RING), pad(qs, 0), step_lo


def _attn_kernel(jt_ref, jp_ref, ja_ref, jq_ref, lo_ref,
                 q_ref, g_ref, kcv_ref, vct_ref, ks_ref, kw_ref, vs_ref, vw_ref, pat_ref, o_ref,
                 kall_ref, vt_ref, sc_ref, bias_ref, qaug_ref, s_ref, p_ref, al_ref, mt_ref, m_ref, acc_ref, oc_ref):
    seq = ks_ref.shape[2]
    nq = GQA_GROUP * ATT_Q
    dh = HEAD_DIM
    n_key_tiles = seq // KEY_TILE
    n_qb = seq // ATT_Q
    u_jobs = ATT_UNROLL

    kall_ref[0:seq, :] = ks_ref[0, 0]
    kall_ref[seq:2 * seq, :] = kw_ref[0, 0]
    ones_rows = (lax.broadcasted_iota(jnp.int32, (VT_ROWS - dh, KEY_TILE), 0) == 0).astype(BF16)
    for n in range(n_key_tiles):
        vt_ref[n, 0:dh, :] = vs_ref[0, 0, :, n * KEY_TILE:(n + 1) * KEY_TILE]
        vt_ref[n_key_tiles + n, 0:dh, :] = vw_ref[0, 0, :, n * KEY_TILE:(n + 1) * KEY_TILE]
        vt_ref[n, dh:VT_ROWS, :] = ones_rows
        vt_ref[n_key_tiles + n, dh:VT_ROWS, :] = ones_rows

    s_ref[...] = jnp.full(s_ref.shape, NEG, F32)
    mt_ref[...] = jnp.full(mt_ref.shape, NEG, F32)
    p_ref[...] = jnp.zeros(p_ref.shape, BF16)
    al_ref[...] = jnp.ones(al_ref.shape, F32)
    m_ref[...] = jnp.full(m_ref.shape, NEG, F32)
    acc_ref[...] = jnp.ones(acc_ref.shape, F32)
    oc_ref[...] = jnp.zeros(oc_ref.shape, F32)

    def prologue(qb):
        slot = qb % ACC_RING
        qa = q_ref[0, 0, qb]
        qt = jnp.concatenate([qa[g * dh:(g + 1) * dh, :] for g in range(GQA_GROUP)], axis=1)
        t_row = qb * ATT_Q + lax.broadcasted_iota(jnp.int32, (1, ATT_Q), 1)
        t4 = jnp.concatenate([t_row] * GQA_GROUP, axis=1)

        n_cmp = kcv_ref.shape[2]
        q_c = jnp.concatenate([qt, jnp.zeros((kcv_ref.shape[3] - dh, nq), BF16)], axis=0)
        s_c = _dot(kcv_ref[0, 0], q_c)
        cmp_end = lax.broadcasted_iota(jnp.int32, (n_cmp, 1), 0) * CMP_STRIDE + (CMP_BLOCK - 1)
        vis = cmp_end <= t4
        s_c = jnp.where(vis, s_c, NEG)
        m_c = jnp.max(s_c, axis=0, keepdims=True)
        e_c = jnp.where(vis, jnp.exp2(s_c - m_c), 0.0)
        p_c = e_c / jnp.maximum(jnp.sum(e_c, axis=0, keepdims=True), 1e-30)
        oc_ref[slot] = _dot(vct_ref[0, 0], p_c.astype(BF16))

        n_sel = seq // SEL_BLOCK
        p_sum = sum(p_c[:, g * ATT_Q:(g + 1) * ATT_Q] for g in range(GQA_GROUP))
        jj = lax.broadcasted_iota(jnp.int32, (n_sel, n_cmp), 0) * SEL_BLOCK
        nn = lax.broadcasted_iota(jnp.int32, (n_sel, n_cmp), 1) * CMP_STRIDE
        ov = jnp.minimum(nn + CMP_BLOCK, jj + SEL_BLOCK) - jnp.maximum(nn, jj)
        ov_t = (jnp.maximum(ov, 0).astype(F32) * (1.0 / CMP_BLOCK)).astype(BF16)
        p_hi, p_lo = _split(p_sum)
        imp = _dot(ov_t, p_hi) + _dot(ov_t, p_lo)
        jb = lax.broadcasted_iota(jnp.int32, (n_sel, 1), 0)
        cur = t_row // SEL_BLOCK
        allowed = jb <= cur
        forced = (jb == 0) | (jb == cur) | (jb == cur - 1)
        score = jnp.where(forced & allowed, FORCE, jnp.where(allowed, imp, NEG))
        sc_ref[...] = score
        n_top = min(SEL_TOP, n_sel)

        def selection_bias(n_blk):
            sub = 8
            groups = [sc_ref[r:r + sub, :] for r in range(0, n_blk, sub)]
            ranks = [jnp.zeros((sub, ATT_Q), F32) for _ in groups]
            for k in range(n_blk):
                row = jnp.broadcast_to(sc_ref[k:k + 1, :], (sub, ATT_Q))
                for r, grp in enumerate(groups):
                    if r * sub > k:
                        ahead = row >= grp
                    elif r * sub + sub - 1 < k:
                        ahead = row > grp
                    else:
                        later = (r * sub + lax.broadcasted_iota(jnp.int32, (sub, 1), 0)) > k
                        ahead = (row > grp) | ((row == grp) & later)
                    ranks[r] = ranks[r] + jnp.where(ahead, 1.0, 0.0)
            chosen = (jnp.concatenate(ranks, axis=0) < n_top) & (jnp.concatenate(groups, axis=0) > 0.5 * NEG)
            bias = jnp.where(chosen, 0.0, NEG).astype(BF16)
            rest = bias_ref.shape[0] - n_blk
            return jnp.concatenate([bias, jnp.full((rest, ATT_Q), NEG, BF16)], axis=0) if rest else bias

        n_buckets = -(-n_sel // RANK_BUCKET)
        bucket = jnp.minimum((qb + 1) * (ATT_Q // SEL_BLOCK) - 1, n_sel - 1) // RANK_BUCKET
        for b in range(n_buckets):
            @pl.when(bucket == b)
            def _():
                bias_ref[...] = selection_bias(min((b + 1) * RANK_BUCKET, n_sel))
        qaug_ref[qb % 2] = jnp.concatenate([qt, jnp.concatenate([bias_ref[...]] * GQA_GROUP, axis=1)], axis=0)

        for br in range(2):
            m_ref[2 * slot + br] = jnp.full((1, nq), NEG, F32)
            acc_ref[2 * slot + br] = jnp.zeros((VT_ROWS, nq), F32)

    def epilogue(qb, slot):
        a_s, a_w = acc_ref[2 * slot], acc_ref[2 * slot + 1]
        o_s = a_s[:dh] / a_s[dh:dh + 1]
        o_w = a_w[:dh] / a_w[dh:dh + 1]
        gates = g_ref[0, 0, qb]

        def gate_row(br):
            return jnp.concatenate([gates[br * GQA_GROUP + g:br * GQA_GROUP + g + 1, :] for g in range(GQA_GROUP)],
                                   axis=1)
        o = gate_row(0) * oc_ref[slot] + gate_row(1) * o_s + gate_row(2) * o_w
        rows = pl.ds(pl.multiple_of(qb * ATT_Q, ATT_Q), ATT_Q)
        o_ref[0, rows, :] = jnp.concatenate(
            [o[:, g * ATT_Q:(g + 1) * ATT_Q].T for g in range(GQA_GROUP)], axis=1).astype(BF16)

    def stage_a(t, u):
        tile, pat, qs = jt_ref[t], jp_ref[t], jq_ref[t]
        k = kall_ref[pl.ds(pl.multiple_of(tile * KEY_TILE, KEY_TILE), KEY_TILE), :]
        s = _dot(k, qaug_ref[qs]) + pat_ref[pat]
        s_ref[u] = s
        mt_ref[u] = jnp.max(s, axis=0, keepdims=True)

    def stage_b(t, u):
        a = ja_ref[t]
        m_old = m_ref[a]
        m_new = jnp.maximum(m_old, mt_ref[u])
        alpha = jnp.exp2(m_old - m_new)
        m_ref[a] = m_new
        al_ref[u] = alpha
        p_ref[u] = jnp.exp2(s_ref[u] - m_new).astype(BF16)

    def stage_c(t, u):
        tile, a = jt_ref[t], ja_ref[t]
        acc_ref[a] = al_ref[u] * acc_ref[a] + _dot(vt_ref[tile], p_ref[u])

    def step(g, carry):
        t0 = g * u_jobs
        for u in range(u_jobs):
            stage_c(t0 + u, u)
        for u in range(u_jobs):
            stage_b(t0 + u_jobs + u, u)
        for u in range(u_jobs):
            stage_a(t0 + 2 * u_jobs + u, u)
        return carry

    prologue(0)

    def block(i, carry):
        prologue(jnp.minimum(i + 1, n_qb - 1))
        epilogue(jnp.maximum(i - EPILOGUE_LAG, 0), (i - EPILOGUE_LAG) % ACC_RING)
        lax.fori_loop(lo_ref[i], lo_ref[i + 1], step, 0)
        return carry

    lax.fori_loop(0, n_qb, block, 0)
    for qb in range(max(n_qb - EPILOGUE_LAG, 0), n_qb):
        epilogue(qb, qb % ACC_RING)


def _attention(q_t, gates_t, kcv, vc_t, ks_aug, kw_aug, vs_t, vw_t):
    bsz, hkv, s, kdim = ks_aug.shape
    dh = vs_t.shape[2]
    n_cmp = kcv.shape[2]
    gd = GQA_GROUP * dh
    nq = GQA_GROUP * ATT_Q
    n_qb = s // ATT_Q
    patterns = _mask_patterns()
    tables = _job_tables(s)
    per_head = lambda shape: pl.BlockSpec((1, 1) + shape, lambda b, h, *_: (b, h) + (0,) * len(shape))
    n_acc = 2 * ACC_RING + 1
    return pl.pallas_call(
        _attn_kernel,
        grid_spec=pltpu.PrefetchScalarGridSpec(
            num_scalar_prefetch=len(tables),
            grid=(bsz, hkv),
            in_specs=[
                per_head((n_qb, gd, ATT_Q)),
                per_head((n_qb, 3 * GQA_GROUP, ATT_Q)),
                per_head((n_cmp, kcv.shape[3])),
                per_head((dh, n_cmp)),
                per_head((s, kdim)),
                per_head((s, kdim)),
                per_head((dh, s)),
                per_head((dh, s)),
                pl.BlockSpec(patterns.shape, lambda b, h, *_: (0, 0, 0)),
            ],
            out_specs=pl.BlockSpec((1, s, gd), lambda b, h, *_: (b, 0, h)),
            scratch_shapes=[
                pltpu.VMEM((2 * s, kdim), BF16),
                pltpu.VMEM((2 * (s // KEY_TILE), VT_ROWS, KEY_TILE), BF16),
                pltpu.VMEM((s // SEL_BLOCK, ATT_Q), F32),
                pltpu.VMEM((kdim - dh, ATT_Q), BF16),
                pltpu.VMEM((2, kdim, nq), BF16),
                pltpu.VMEM((ATT_UNROLL, KEY_TILE, nq), F32),
                pltpu.VMEM((ATT_UNROLL, KEY_TILE, nq), BF16),
                pltpu.VMEM((ATT_UNROLL, 1, nq), F32),
                pltpu.VMEM((ATT_UNROLL, 1, nq), F32),
                pltpu.VMEM((n_acc, 1, nq), F32),
                pltpu.VMEM((n_acc, VT_ROWS, nq), F32),
                pltpu.VMEM((ACC_RING, dh, nq), F32),
            ],
        ),
        out_shape=jax.ShapeDtypeStruct((bsz, s, hkv * gd), BF16),
        compiler_params=_params("parallel", "arbitrary"),
        name="nsa_attention",
    )(*tables, q_t, gates_t, kcv, vc_t, ks_aug, kw_aug, vs_t, vw_t, patterns)


def _oproj_kernel(o_ref, x_ref, mod_ref, w_ref, g_ref, b_ref, out_ref):
    y = _dot(o_ref[0], w_ref[...])
    out_ref[0] = _layer_norm(ALPHA * x_ref[0] + mod_ref[0, 2:3, :] * y, g_ref[...], b_ref[...])


def _out_projection(o, x, mod, w_o, ln_g, ln_b):
    bsz, s, d = x.shape
    tm = 512
    return pl.pallas_call(
        _oproj_kernel,
        grid=(bsz, s // tm),
        in_specs=[
            pl.BlockSpec((1, tm, o.shape[-1]), lambda b, j: (b, j, 0)),
            pl.BlockSpec((1, tm, d), lambda b, j: (b, j, 0)),
            pl.BlockSpec((1, 6, d), lambda b, j: (b, 0, 0)),
            _const_spec(w_o.shape),
            _const_spec((1, d)),
            _const_spec((1, d)),
        ],
        out_specs=pl.BlockSpec((1, tm, d), lambda b, j: (b, j, 0)),
        out_shape=jax.ShapeDtypeStruct((bsz, s, d), F32),
        compiler_params=_params("parallel", "parallel"),
        name="nsa_out_proj",
    )(o, x, mod, w_o, ln_g.reshape(1, d), ln_b.reshape(1, d))


def _moe_block(x, mod, router_wt, router_bias, w_gu, w_down, layer, ln_g, ln_b):
    bsz, s, d = x.shape
    t = bsz * s
    xf = x.reshape(t, d)
    wcol, route, counts = _router(xf, mod, router_wt, router_bias, s)

    cnt = counts[:, 0]
    padded = (cnt + EXPERT_TILE - 1) // EXPERT_TILE * EXPERT_TILE
    ends = jnp.cumsum(padded)
    offs = ends - padded
    n_tiles = (TOP_K * t) // EXPERT_TILE + N_EXPERTS
    tile_start = jnp.arange(n_tiles, dtype=jnp.int32) * EXPERT_TILE
    tile_expert = jnp.minimum(jnp.sum(tile_start[:, None] >= ends[None, :], axis=1), N_EXPERTS - 1).astype(jnp.int32)
    n_used = (ends[-1:] // EXPERT_TILE).astype(jnp.int32)
    experts, ranks = route[:TOP_K], route[TOP_K:2 * TOP_K]
    dst = ranks + sum(jnp.where(experts == e, offs[e], 0) for e in range(N_EXPERTS))
    idx = dst.reshape(TOP_K, t // TOKEN_TILE, TOKEN_TILE).transpose(1, 0, 2).reshape(t // TOKEN_TILE, 1, TOP_K * TOKEN_TILE)

    last_tile = jnp.where(cnt > 0, ends - EXPERT_TILE, -1)
    spare = n_used + jnp.arange(N_EXPERTS)
    spare = jnp.where(spare < n_tiles, spare * EXPERT_TILE, -1)
    xs = _dispatch(jnp.concatenate([last_tile, spare]).astype(jnp.int32), idx, xf, mod, n_tiles * EXPERT_TILE, s)
    ys = _experts(tile_expert, n_used, xs, w_gu, w_down, layer)
    out = _combine(idx, xf, mod, wcol, ys, ln_g, ln_b, s)
    return out.reshape(bsz, s, d)


def _nsa_layer(x, mod, w_kv, cmp_pe, cmp_w1, cmp_b1, cmp_w2, cmp_b2, w_qg, b_g, w_o, ln_g, ln_b):
    bsz, s, d = x.shape
    hkv, dh, grp = N_KV_HEADS, HEAD_DIM, GQA_GROUP
    hw = hkv * dh
    kvw = w_kv.reshape(d, 6, hw)
    kvc_w = jnp.stack([kvw[:, 0].reshape(d, hkv, dh), kvw[:, 1].reshape(d, hkv, dh)], axis=2).reshape(d, 2 * hw)
    w_nat = jnp.concatenate([kvc_w, kvw[:, 2], kvw[:, 4]], axis=1).astype(BF16)
    w_vt = jnp.concatenate([kvw[:, 3], kvw[:, 5]], axis=1).T.astype(BF16)
    nq = N_HEADS * dh
    wg = w_qg[:, nq:].reshape(d, hkv, grp, 3).transpose(0, 1, 3, 2).reshape(d, 3 * N_HEADS)
    w_qgt = jnp.concatenate([w_qg[:, :nq], wg], axis=1).T.astype(BF16)
    bg = b_g.reshape(hkv, grp, 3).transpose(0, 2, 1).reshape(3 * N_HEADS, 1)
    kvc, ks_aug, kw_aug, vs_t, vw_t, q_t, gates_t = _projections(x, mod, w_nat, w_vt, w_qgt, bg)
    cmp_kv, cmp_vt = _compress(kvc, cmp_pe, cmp_w1, cmp_b1, cmp_w2, cmp_b2)
    o = _attention(q_t, gates_t, cmp_kv, cmp_vt, ks_aug, kw_aug, vs_t, vw_t)
    return _out_projection(o, x, mod, w_o.astype(BF16), ln_g, ln_b)


def kernel(x, c, ada_w, ada_b, ln_g, ln_b, conv_w_in, conv_w, conv_b, conv_w_out, w_kv, cmp_pe, cmp_w1, cmp_b1, cmp_w2, cmp_b2, w_qg, b_g, w_o, router_w, router_bias, w_gu, w_down):
    bsz, s, d = x.shape
    mod = _modulation(c, ada_w, ada_b).reshape(DEPTH, bsz, 6, d)
    router_wt = router_w.T

    x = _conv_layer(x, mod[0], conv_w_in[0].astype(BF16), conv_w[0], conv_b[0], conv_w_out[0].astype(BF16),
                    ln_g[0, 0], ln_b[0, 0])
    x = _moe_block(x, mod[0], router_wt, router_bias, w_gu, w_down, 0, ln_g[0, 1], ln_b[0, 1])

    x = _nsa_layer(x, mod[1], w_kv, cmp_pe, cmp_w1, cmp_b1, cmp_w2, cmp_b2, w_qg[0], b_g[0], w_o[0],
                   ln_g[1, 0], ln_b[1, 0])
    x = _moe_block(x, mod[1], router_wt, router_bias, w_gu, w_down, 1, ln_g[1, 1], ln_b[1, 1])
    return x
```

```python
import functools
import math

import jax
import jax.numpy as jnp
import numpy as np
from jax import lax
from jax.experimental import pallas as pl
from jax.experimental.pallas import tpu as pltpu

F32 = jnp.float32
BF16 = jnp.bfloat16

D_MODEL = 1024
DEPTH = 2
N_A_LAYERS = DEPTH // 2
CONV_WIDTH = 3
N_HEADS = 16
HEAD_DIM = D_MODEL // N_HEADS
N_KV_HEADS = 4
GQA_GROUP = N_HEADS // N_KV_HEADS
CMP_BLOCK = 32
CMP_STRIDE = 16
CMP_HIDDEN = 4 * HEAD_DIM
SEL_BLOCK = 64
SEL_TOP = 16
WINDOW = 512
N_EXPERTS = 16
N_GROUPS = 4
EXPERTS_PER_GROUP = N_EXPERTS // N_GROUPS
TOP_K = 2
D_FF_EXPERT = D_MODEL // 2
ALPHA = (2 * DEPTH) ** 0.25
LN_EPS = 1e-5
NEG = -1e30
FORCE = 1e9

LANES = 128
VMEM_LIMIT_BYTES = 56 * 1024 * 1024

LOG2E = math.log2(math.e)

ATT_Q = 256
KEY_TILE = 256
PAT_ZERO, PAT_DIAG, PAT_ANTI, PAT_NONE = range(4)
ATT_UNROLL = 4
ACC_RING = 8
EPILOGUE_LAG = 3
RANK_BUCKET = 16
VT_ROWS = HEAD_DIM + 16


def _dot(a, b):
    return jnp.dot(a, b, preferred_element_type=F32)


def _dot_nt(a, b):
    return lax.dot_general(a, b, (((1,), (1,)), ((), ())), preferred_element_type=F32)


def _split(x):
    hi = x.astype(BF16)
    lo = (x - hi.astype(F32)).astype(BF16)
    return hi, lo


def _dot3(a, b):
    ah, al = _split(a)
    bh, bl = _split(b)
    return _dot(ah, bh) + (_dot(ah, bl) + _dot(al, bh))


def _dot3_nt(a, b):
    ah, al = _split(a)
    bh, bl = _split(b)
    return _dot_nt(ah, bh) + (_dot_nt(ah, bl) + _dot_nt(al, bh))


def _layer_norm(r, g, b):
    mu = jnp.mean(r, axis=-1, keepdims=True)
    d = r - mu
    var = jnp.mean(d * d, axis=-1, keepdims=True)
    return d * lax.rsqrt(var + LN_EPS) * g + b


def _params(*sem):
    return pltpu.CompilerParams(dimension_semantics=sem, vmem_limit_bytes=VMEM_LIMIT_BYTES)


def _const_spec(shape):
    zeros = (0,) * len(shape)
    return pl.BlockSpec(shape, lambda *_: zeros)


def _mod_kernel(c_ref, w_ref, b_ref, o_ref):
    c = c_ref[...]
    s = c * jax.nn.sigmoid(c)
    o_ref[0] = _dot3(s, w_ref[0]) + b_ref[0]


def _modulation(c, ada_w, ada_b):
    depth, d, n = ada_w.shape
    bsz = c.shape[0]
    tn = 1536
    return pl.pallas_call(
        _mod_kernel,
        grid=(depth, n // tn),
        in_specs=[
            pl.BlockSpec((bsz, d), lambda l, j: (0, 0)),
            pl.BlockSpec((1, d, tn), lambda l, j: (l, 0, j)),
            pl.BlockSpec((1, 1, tn), lambda l, j: (l, 0, j)),
        ],
        out_specs=pl.BlockSpec((1, bsz, tn), lambda l, j: (l, 0, j)),
        out_shape=jax.ShapeDtypeStruct((depth, bsz, n), F32),
        compiler_params=_params("parallel", "parallel"),
        name="adaln_mod",
    )(c, ada_w, ada_b.reshape(depth, 1, n))


CONV_HALO = 8


def _conv_kernel(x_ref, mod_ref, win_ref, cw_ref, cb_ref, wout_ref, g_ref, b_ref, o_ref, z_ref):
    tm = x_ref.shape[1]
    d = x_ref.shape[2]

    @pl.when(pl.program_id(1) == 0)
    def _():
        z_ref[0:CONV_HALO, :] = jnp.zeros((CONV_HALO, d), F32)

    x = x_ref[0]
    sh = mod_ref[0, 0:1, :]
    sc = mod_ref[0, 1:2, :]
    gate = mod_ref[0, 2:3, :]
    u = (x * (1.0 + sc) + sh).astype(BF16)
    bch = _dot(u, win_ref[...])
    z = bch[:, d:2 * d] * bch[:, 2 * d:]
    z_ref[CONV_HALO:CONV_HALO + tm, :] = z
    z1 = z_ref[CONV_HALO - 1:CONV_HALO - 1 + tm, :]
    z2 = z_ref[CONV_HALO - 2:CONV_HALO - 2 + tm, :]
    conv = cw_ref[0:1, :] * z2 + cw_ref[1:2, :] * z1 + cw_ref[2:3, :] * z + cb_ref[...]
    v = (bch[:, :d] * conv).astype(BF16)
    y = _dot(v, wout_ref[...])
    o_ref[0] = _layer_norm(ALPHA * x + gate * y, g_ref[...], b_ref[...])
    z_ref[0:CONV_HALO, :] = z_ref[tm:tm + CONV_HALO, :]


def _conv_layer(x, mod, w_in, conv_w, conv_b, w_out, ln_g, ln_b):
    bsz, s, d = x.shape
    tm = 512
    return pl.pallas_call(
        _conv_kernel,
        grid=(bsz, s // tm),
        in_specs=[
            pl.BlockSpec((1, tm, d), lambda b, j: (b, j, 0)),
            pl.BlockSpec((1, 6, d), lambda b, j: (b, 0, 0)),
            _const_spec((d, 3 * d)),
            _const_spec((CONV_WIDTH, d)),
            _const_spec((1, d)),
            _const_spec((d, d)),
            _const_spec((1, d)),
            _const_spec((1, d)),
        ],
        out_specs=pl.BlockSpec((1, tm, d), lambda b, j: (b, j, 0)),
        out_shape=jax.ShapeDtypeStruct((bsz, s, d), F32),
        scratch_shapes=[pltpu.VMEM((tm + CONV_HALO, d), F32)],
        compiler_params=_params("arbitrary", "arbitrary"),
        name="conv_mixer",
    )(x, mod, w_in, conv_w, conv_b.reshape(1, d), w_out, ln_g.reshape(1, d), ln_b.reshape(1, d))


def _router_kernel(x_ref, mod_ref, rwt_ref, rb_ref, wcol_ref, route_ref, count_ref, base_ref, tri_ref):
    tm = x_ref.shape[0]
    x = x_ref[...]
    u = x * (1.0 + mod_ref[0, 4:5, :]) + mod_ref[0, 3:4, :]
    logits = _dot3_nt(rwt_ref[...], u)
    m = jnp.max(logits, axis=0, keepdims=True)
    e = jnp.exp(logits - m)
    aff = e / jnp.sum(e, axis=0, keepdims=True)
    biased = aff + rb_ref[...]
    aff_r = [aff[i:i + 1, :] for i in range(N_EXPERTS)]
    row = [biased[i:i + 1, :] for i in range(N_EXPERTS)]

    best_s, best = None, None
    for g in range(N_GROUPS):
        r = row[g * EXPERTS_PER_GROUP:(g + 1) * EXPERTS_PER_GROUP]
        gs = None
        for i in range(EXPERTS_PER_GROUP):
            for j in range(i + 1, EXPERTS_PER_GROUP):
                p = r[i] + r[j]
                gs = p if gs is None else jnp.maximum(gs, p)
        if g == 0:
            best_s, best = gs, jnp.zeros((1, tm), jnp.int32)
        else:
            upd = gs > best_s
            best = jnp.where(upd, g, best)
            best_s = jnp.where(upd, gs, best_s)

    masked = [jnp.where(best == (i // EXPERTS_PER_GROUP), row[i], NEG) for i in range(N_EXPERTS)]

    def first_argmax(vals):
        v, idx = vals[0], jnp.zeros((1, tm), jnp.int32)
        for i in range(1, N_EXPERTS):
            upd = vals[i] > v
            idx = jnp.where(upd, i, idx)
            v = jnp.where(upd, vals[i], v)
        return idx

    i0 = first_argmax(masked)
    i1 = first_argmax([jnp.where(i0 == i, -jnp.inf, masked[i]) for i in range(N_EXPERTS)])
    w0 = sum(jnp.where(i0 == i, aff_r[i], 0.0) for i in range(N_EXPERTS))
    w1 = sum(jnp.where(i1 == i, aff_r[i], 0.0) for i in range(N_EXPERTS))
    tot = w0 + w1
    w0 = w0 / tot
    w1 = w1 / tot
    w_t = jnp.concatenate([w0, w1, jnp.zeros((LANES - TOP_K, tm), F32)], axis=0)
    wcol_ref[...] = w_t.T

    @pl.when(pl.program_id(0) == 0)
    def _():
        base_ref[...] = jnp.zeros(base_ref.shape, F32)
        r_i = lax.broadcasted_iota(jnp.int32, tri_ref.shape, 0)
        c_i = lax.broadcasted_iota(jnp.int32, tri_ref.shape, 1)
        tri_ref[...] = jnp.where(r_i <= c_i, 1.0, 0.0).astype(BF16)

    hot = jnp.concatenate([jnp.where((i0 == i) | (i1 == i), 1.0, 0.0) for i in range(N_EXPERTS)], axis=0)
    incl = _dot(hot.astype(BF16), tri_ref[...])
    pos = base_ref[...] + (incl - hot)
    rank0 = sum(jnp.where(i0 == i, pos[i:i + 1, :], 0.0) for i in range(N_EXPERTS))
    rank1 = sum(jnp.where(i1 == i, pos[i:i + 1, :], 0.0) for i in range(N_EXPERTS))
    base = base_ref[...] + jnp.sum(hot, axis=1, keepdims=True)
    base_ref[...] = base
    route_ref[...] = jnp.concatenate(
        [i0, i1, rank0.astype(jnp.int32), rank1.astype(jnp.int32), jnp.zeros((4, tm), jnp.int32)], axis=0)
    count_ref[...] = jnp.broadcast_to(base, count_ref.shape).astype(jnp.int32)


def _router(x, mod, router_wt, router_bias, seq):
    t, d = x.shape
    tm = 1024
    return pl.pallas_call(
        _router_kernel,
        grid=(t // tm,),
        in_specs=[
            pl.BlockSpec((tm, d), lambda i: (i, 0)),
            pl.BlockSpec((1, 6, d), lambda i: ((i * tm) // seq, 0, 0)),
            _const_spec((N_EXPERTS, d)),
            _const_spec((N_EXPERTS, 1)),
        ],
        out_specs=(
            pl.BlockSpec((tm, LANES), lambda i: (i, 0)),
            pl.BlockSpec((8, tm), lambda i: (0, i)),
            _const_spec((N_EXPERTS, LANES)),
        ),
        out_shape=(
            jax.ShapeDtypeStruct((t, LANES), F32),
            jax.ShapeDtypeStruct((8, t), jnp.int32),
            jax.ShapeDtypeStruct((N_EXPERTS, LANES), jnp.int32),
        ),
        scratch_shapes=[pltpu.VMEM((N_EXPERTS, 1), F32), pltpu.VMEM((tm, tm), BF16)],
        compiler_params=_params("arbitrary"),
        name="moe_router",
    )(x, mod, router_wt, router_bias.reshape(N_EXPERTS, 1))


EXPERT_TILE = 512
TOKEN_TILE = 512
ROW_COPY_UNROLL = 8
ROW_TILES = D_MODEL // LANES


def _store_rows(ref, x):
    rows = x.shape[0]
    for c in range(ROW_TILES):
        ref[pl.ds(c, rows, stride=ROW_TILES), :] = x[:, c * LANES:(c + 1) * LANES]


def _load_rows(ref, rows):
    return jnp.concatenate([ref[pl.ds(c, rows, stride=ROW_TILES), :] for c in range(ROW_TILES)], axis=1)


def _row(ref, r):
    return ref.at[pl.ds(pl.multiple_of(r * ROW_TILES, ROW_TILES), ROW_TILES)]


def _start_row_copies(tm, row_copy):
    def start(t, c):
        for k in range(TOP_K):
            row_copy(t, k).start(priority=k % 2)
        return c

    lax.fori_loop(0, tm, start, 0, unroll=ROW_COPY_UNROLL)


def _dispatch_kernel(pad_ref, idx_ref, x_ref, mod_ref, xs_ref, u_ref, sem):
    tm = x_ref.shape[0]

    i = pl.program_id(0)
    slot = i % 2

    @pl.when(i == 0)
    def _():
        u_ref[1] = jnp.zeros(u_ref.shape[1:], F32)
        for wait in (False, True):
            for e in range(pad_ref.shape[0]):
                @pl.when(pad_ref[e] >= 0)
                def _():
                    row = pl.multiple_of(jnp.maximum(pad_ref[e], 0) * ROW_TILES, tm * ROW_TILES)
                    fill = pltpu.make_async_copy(u_ref.at[1], xs_ref.at[pl.ds(row, tm * ROW_TILES)], sem.at[1])
                    fill.wait() if wait else fill.start()

    _store_rows(u_ref.at[slot], x_ref[...] * (1.0 + mod_ref[0, 4:5, :]) + mod_ref[0, 3:4, :])

    def row_copy(t, k):
        dst = idx_ref[0, 0, k * tm + t]
        return pltpu.make_async_copy(_row(u_ref.at[slot], t), _row(xs_ref, dst), sem.at[slot])

    _start_row_copies(tm, row_copy)

    def wait_tile(s):
        for _ in range(TOP_K):
            pltpu.make_async_copy(u_ref.at[s], xs_ref.at[pl.ds(0, tm * ROW_TILES)], sem.at[s]).wait()

    @pl.when(i > 0)
    def _():
        wait_tile(1 - slot)

    @pl.when(i == pl.num_programs(0) - 1)
    def _():
        wait_tile(slot)


def _dispatch(pad_start, idx, x, mod, n_rows, seq):
    t, d = x.shape
    tm = TOKEN_TILE
    assert tm == EXPERT_TILE
    return pl.pallas_call(
        _dispatch_kernel,
        grid_spec=pltpu.PrefetchScalarGridSpec(
            num_scalar_prefetch=1,
            grid=(t // tm,),
            in_specs=[
                pl.BlockSpec((1, 1, TOP_K * tm), lambda i, pad: (i, 0, 0), memory_space=pltpu.SMEM),
                pl.BlockSpec((tm, d), lambda i, pad: (i, 0)),
                pl.BlockSpec((1, 6, d), lambda i, pad: ((i * tm) // seq, 0, 0)),
            ],
            out_specs=pl.BlockSpec(memory_space=pl.ANY),
            scratch_shapes=[pltpu.VMEM((2, tm * ROW_TILES, LANES), F32), pltpu.SemaphoreType.DMA((2,))],
        ),
        out_shape=jax.ShapeDtypeStruct((n_rows * ROW_TILES, LANES), F32),
        compiler_params=_params("arbitrary"),
        name="moe_dispatch",
    )(pad_start, idx, x, mod)


def _expert_kernel(te_ref, nu_ref, xs_ref, wgu_ref, wdn_ref, ys_ref):
    del te_ref
    used = pl.program_id(0) < nu_ref[0]

    @pl.when(used)
    def _():
        f = wdn_ref.shape[2]
        h = _dot(_load_rows(xs_ref, EXPERT_TILE).astype(BF16), wgu_ref[0, 0].astype(BF16))
        gate = h[:, :f]
        a = (gate * jax.nn.sigmoid(gate) * h[:, f:]).astype(BF16)
        _store_rows(ys_ref, _dot(a, wdn_ref[0, 0].astype(BF16)))

    @pl.when(jnp.logical_not(used))
    def _():
        ys_ref[...] = jnp.zeros(ys_ref.shape, F32)


def _experts(tile_expert, n_used, xs, w_gu, w_down, layer):
    _, _, d, f2 = w_gu.shape
    f = f2 // 2
    row_map = lambda s, te, nu: (s, 0)
    in_row_map = lambda s, te, nu: (jnp.minimum(s, nu[0]), 0)
    w_map = lambda s, te, nu: (layer, te[s], 0, 0)
    return pl.pallas_call(
        _expert_kernel,
        grid_spec=pltpu.PrefetchScalarGridSpec(
            num_scalar_prefetch=2,
            grid=(tile_expert.shape[0],),
            in_specs=[
                pl.BlockSpec((EXPERT_TILE * ROW_TILES, LANES), in_row_map),
                pl.BlockSpec((1, 1, d, f2), w_map),
                pl.BlockSpec((1, 1, f, d), w_map),
            ],
            out_specs=pl.BlockSpec((EXPERT_TILE * ROW_TILES, LANES), row_map),
        ),
        out_shape=jax.ShapeDtypeStruct(xs.shape, F32),
        compiler_params=_params("arbitrary"),
        name="moe_experts",
    )(tile_expert, n_used, xs, w_gu, w_down)


def _combine_kernel(idx_ref, next_idx_ref, x_ref, mod_ref, w_ref, ys_ref, g_ref, b_ref, o_ref, y_ref, sem):
    tm = x_ref.shape[0]
    i = pl.program_id(0)
    slot = i % 2

    def gather(ids_ref, into):
        def row_copy(t, k):
            src = ids_ref[0, 0, k * tm + t]
            return pltpu.make_async_copy(_row(ys_ref, src), _row(y_ref.at[into, k], t), sem.at[into])

        _start_row_copies(tm, row_copy)

    @pl.when(i == 0)
    def _():
        gather(idx_ref, 0)

    @pl.when(i + 1 < pl.num_programs(0))
    def _():
        gather(next_idx_ref, 1 - slot)

    for k in range(TOP_K):
        pltpu.make_async_copy(ys_ref.at[pl.ds(0, tm * ROW_TILES)], y_ref.at[slot, k], sem.at[slot]).wait()
    w = w_ref[...]
    out = w[:, 0:1] * _load_rows(y_ref.at[slot, 0], tm) + w[:, 1:2] * _load_rows(y_ref.at[slot, 1], tm)
    r = ALPHA * x_ref[...] + mod_ref[0, 5:6, :] * out
    o_ref[...] = _layer_norm(r, g_ref[...], b_ref[...])


def _combine(idx, x, mod, wcol, ys, ln_g, ln_b, seq):
    t, d = x.shape
    tm = TOKEN_TILE
    n_tiles = t // tm
    return pl.pallas_call(
        _combine_kernel,
        grid=(n_tiles,),
        in_specs=[
            pl.BlockSpec((1, 1, TOP_K * tm), lambda i: (i, 0, 0), memory_space=pltpu.SMEM),
            pl.BlockSpec((1, 1, TOP_K * tm), lambda i: (jnp.minimum(i + 1, n_tiles - 1), 0, 0),
                         memory_space=pltpu.SMEM),
            pl.BlockSpec((tm, d), lambda i: (i, 0)),
            pl.BlockSpec((1, 6, d), lambda i: ((i * tm) // seq, 0, 0)),
            pl.BlockSpec((tm, LANES), lambda i: (i, 0)),
            pl.BlockSpec(memory_space=pl.ANY),
            _const_spec((1, d)),
            _const_spec((1, d)),
        ],
        out_specs=pl.BlockSpec((tm, d), lambda i: (i, 0)),
        out_shape=jax.ShapeDtypeStruct((t, d), F32),
        scratch_shapes=[pltpu.VMEM((2, TOP_K, tm * ROW_TILES, LANES), F32), pltpu.SemaphoreType.DMA((2,))],
        compiler_params=_params("arbitrary"),
        name="moe_combine",
    )(idx, idx, x, mod, wcol, ys, ln_g.reshape(1, d), ln_b.reshape(1, d))


def _proj_kernel(x_ref, mod_ref, wn_ref, wvt_ref, wqt_ref, bg_ref,
                 kvc_ref, ks_ref, kw_ref, vst_ref, vwt_ref, qt_ref, gt_ref):
    tm = x_ref.shape[1]
    j = pl.program_id(1)
    x = x_ref[0]
    xb = x.astype(BF16)
    nat = _dot(xb, wn_ref[...])
    vt = _dot_nt(wvt_ref[...], xb)
    u = (x * (1.0 + mod_ref[0, 1:2, :]) + mod_ref[0, 0:1, :]).astype(BF16)
    qg = _dot_nt(wqt_ref[...], u)
    nq = N_HEADS * HEAD_DIM
    q = (qg[:nq, :] * (HEAD_DIM ** -0.5 * LOG2E)).astype(BF16)
    gates = jax.nn.sigmoid(qg[nq:, :] + bg_ref[...])
    per = 3 * GQA_GROUP
    blk = (j * tm + lax.broadcasted_iota(jnp.int32, (tm, SEL_BLOCK), 0)) // SEL_BLOCK
    onehot = (blk == lax.broadcasted_iota(jnp.int32, (tm, SEL_BLOCK), 1)).astype(BF16)
    zeros = jnp.zeros((tm, SEL_BLOCK), BF16)
    hw = N_KV_HEADS * HEAD_DIM
    for h in range(N_KV_HEADS):
        c0 = h * HEAD_DIM
        kvc_ref[0, h] = nat[:, 2 * c0:2 * c0 + 2 * HEAD_DIM]
        ks = nat[:, 2 * hw + c0:2 * hw + c0 + HEAD_DIM].astype(BF16)
        ks_ref[0, h] = jnp.concatenate([ks, onehot], axis=1)
        kw = nat[:, 3 * hw + c0:3 * hw + c0 + HEAD_DIM].astype(BF16)
        kw_ref[0, h] = jnp.concatenate([kw, zeros], axis=1)
        vst_ref[0, h] = vt[c0:c0 + HEAD_DIM, :].astype(BF16)
        vwt_ref[0, h] = vt[hw + c0:hw + c0 + HEAD_DIM, :].astype(BF16)
        for p in range(tm // ATT_Q):
            cols = slice(p * ATT_Q, (p + 1) * ATT_Q)
            qt_ref[0, h, p] = q[h * GQA_GROUP * HEAD_DIM:(h + 1) * GQA_GROUP * HEAD_DIM, cols]
            gt_ref[0, h, p] = gates[h * per:(h + 1) * per, cols]


def _projections(x, mod, w_nat, w_vt, w_qgt, b_g):
    bsz, s, d = x.shape
    tm = 512
    hkv, dh = N_KV_HEADS, HEAD_DIM
    per = 3 * GQA_GROUP
    nqg = w_qgt.shape[0]
    out_shape = (
        jax.ShapeDtypeStruct((bsz, hkv, s, 2 * dh), F32),
        jax.ShapeDtypeStruct((bsz, hkv, s, 2 * dh), BF16),
        jax.ShapeDtypeStruct((bsz, hkv, s, 2 * dh), BF16),
        jax.ShapeDtypeStruct((bsz, hkv, dh, s), BF16),
        jax.ShapeDtypeStruct((bsz, hkv, dh, s), BF16),
        jax.ShapeDtypeStruct((bsz, hkv, s // ATT_Q, GQA_GROUP * dh, ATT_Q), BF16),
        jax.ShapeDtypeStruct((bsz, hkv, s // ATT_Q, per, ATT_Q), F32),
    )
    out_specs = (
        pl.BlockSpec((1, hkv, tm, 2 * dh), lambda b, j: (b, 0, j, 0)),
        pl.BlockSpec((1, hkv, tm, 2 * dh), lambda b, j: (b, 0, j, 0)),
        pl.BlockSpec((1, hkv, tm, 2 * dh), lambda b, j: (b, 0, j, 0)),
        pl.BlockSpec((1, hkv, dh, tm), lambda b, j: (b, 0, 0, j)),
        pl.BlockSpec((1, hkv, dh, tm), lambda b, j: (b, 0, 0, j)),
        pl.BlockSpec((1, hkv, tm // ATT_Q, GQA_GROUP * dh, ATT_Q), lambda b, j: (b, 0, j, 0, 0)),
        pl.BlockSpec((1, hkv, tm // ATT_Q, per, ATT_Q), lambda b, j: (b, 0, j, 0, 0)),
    )
    return pl.pallas_call(
        _proj_kernel,
        grid=(bsz, s // tm),
        in_specs=[
            pl.BlockSpec((1, tm, d), lambda b, j: (b, j, 0)),
            pl.BlockSpec((1, 6, d), lambda b, j: (b, 0, 0)),
            _const_spec(w_nat.shape),
            _const_spec(w_vt.shape),
            _const_spec(w_qgt.shape),
            _const_spec((nqg - N_HEADS * dh, 1)),
        ],
        out_specs=out_specs,
        out_shape=out_shape,
        compiler_params=_params("parallel", "parallel"),
        name="nsa_projections",
    )(x, mod, w_nat, w_vt, w_qgt, b_g)


def _compress_kernel(kvc_ref, pe_ref, w1_ref, b1_ref, w2_ref, b2_ref, nat_ref, tr_ref):
    n = kvc_ref.shape[2] // CMP_STRIDE
    hid2 = w1_ref.shape[-1]
    p = jnp.zeros((n, hid2), F32)
    q = jnp.zeros((n, hid2), F32)
    for l in range(CMP_STRIDE):
        x = kvc_ref[0, 0, pl.ds(l, n, stride=CMP_STRIDE), :]
        p = p + _dot((x + pe_ref[l:l + 1, :]).astype(BF16), w1_ref[0, l])
        q = q + _dot((x + pe_ref[CMP_STRIDE + l:CMP_STRIDE + l + 1, :]).astype(BF16), w1_ref[1, l])
    pre = p + pltpu.roll(q, n - 1, 0) + b1_ref[...]
    hdn = 0.5 * pre * (1.0 + jnp.tanh(0.7978845608028654 * (pre + 0.044715 * (pre * pre * pre))))
    out = _dot(hdn.astype(BF16), w2_ref[...]) + b2_ref[...]
    nat_ref[0, 0] = out.astype(BF16)
    tr_ref[0, 0] = out[:, HEAD_DIM:].T.astype(BF16)


def _compress(kvc, cmp_pe, cmp_w1, cmp_b1, cmp_w2, cmp_b2):
    bsz, hkv, s, two_dh = kvc.shape
    dh = two_dh // 2
    n = s // CMP_STRIDE
    hid = cmp_w1.shape[-1]
    zw = jnp.zeros((2, CMP_STRIDE, dh, hid), F32)
    w1 = cmp_w1.reshape(2, 2, CMP_STRIDE, dh, hid)
    w1 = jnp.concatenate([jnp.concatenate([w1[0], zw], axis=-1), jnp.concatenate([zw, w1[1]], axis=-1)], axis=-2)
    zd = jnp.zeros((hid, dh), F32)
    w2 = jnp.concatenate([jnp.concatenate([cmp_w2[0], zd], axis=1), jnp.concatenate([zd, cmp_w2[1]], axis=1)], axis=0)
    pe = jnp.concatenate([cmp_pe[0], cmp_pe[1]], axis=1)
    return pl.pallas_call(
        _compress_kernel,
        grid=(bsz, hkv),
        in_specs=[
            pl.BlockSpec((1, 1, s, two_dh), lambda b, h: (b, h, 0, 0)),
            _const_spec(pe.shape),
            _const_spec(w1.shape),
            _const_spec((1, 2 * hid)),
            _const_spec(w2.shape),
            _const_spec((1, two_dh)),
        ],
        out_specs=(
            pl.BlockSpec((1, 1, n, two_dh), lambda b, h: (b, h, 0, 0)),
            pl.BlockSpec((1, 1, dh, n), lambda b, h: (b, h, 0, 0)),
        ),
        out_shape=(
            jax.ShapeDtypeStruct((bsz, hkv, n, two_dh), BF16),
            jax.ShapeDtypeStruct((bsz, hkv, dh, n), BF16),
        ),
        compiler_params=_params("parallel", "parallel"),
        name="nsa_compress",
    )(kvc, pe, w1.astype(BF16), cmp_b1.reshape(1, 2 * hid), w2.astype(BF16), cmp_b2.reshape(1, two_dh))


def _mask_patterns():
    assert WINDOW == 2 * KEY_TILE and ATT_Q == KEY_TILE
    keyl = np.arange(KEY_TILE)[:, None]
    ql = np.arange(ATT_Q)[None, :]
    true = np.ones((KEY_TILE, ATT_Q), bool)
    valid = np.stack([
        true,
        keyl <= ql,
        keyl > ql,
        ~true,
    ])
    return jnp.asarray(np.where(valid, 0.0, NEG), F32)


def _job_tables(seq):
    nqb, nkt, u = seq // ATT_Q, seq // KEY_TILE, ATT_UNROLL
    tile, pat, acc, qs, base = [], [], [], [], []
    for i in range(nqb):
        base.append(len(tile))
        for j in range(i + 1):
            tile.append(j)
            pat.append(PAT_DIAG if j == i else PAT_ZERO)
            acc.append(2 * (i % ACC_RING))
            qs.append(i % 2)
        for w in range(min(i, WINDOW // KEY_TILE) + 1):
            tile.append(nkt + i - w)
            pat.append((PAT_DIAG, PAT_ZERO, PAT_ANTI)[w])
            acc.append(2 * (i % ACC_RING) + 1)
            qs.append(i % 2)
    n_real = len(tile)
    n_steps = (n_real - 1 + 2 * u) // u + 1
    pad = lambda x, fill: np.array([fill] * (2 * u) + x + [fill] * (n_steps * u - n_real), np.int32)
    step_lo = np.array([-(-b // u) for b in base] + [n_steps], np.int32)
    assert all(base[i + 2] >= step_lo[i + 1] * u for i in range(nqb - 2))
    return pad(tile, 0), pad(pat, PAT_NONE), pad(acc, 2 * ACC_RING), pad(qs, 0), step_lo


def _attn_kernel(jt_ref, jp_ref, ja_ref, jq_ref, lo_ref,
                 q_ref, g_ref, kcv_ref, vct_ref, ks_ref, kw_ref, vs_ref, vw_ref, pat_ref, o_ref,
                 kall_ref, vt_ref, sc_ref, bias_ref, qaug_ref, s_ref, p_ref, al_ref, mt_ref, m_ref, acc_ref, oc_ref):
    seq = ks_ref.shape[2]
    nq = GQA_GROUP * ATT_Q
    dh = HEAD_DIM
    n_key_tiles = seq // KEY_TILE
    n_qb = seq // ATT_Q
    u_jobs = ATT_UNROLL

    kall_ref[0:seq, :] = ks_ref[0, 0]
    kall_ref[seq:2 * seq, :] = kw_ref[0, 0]
    ones_rows = (lax.broadcasted_iota(jnp.int32, (VT_ROWS - dh, KEY_TILE), 0) == 0).astype(BF16)
    for n in range(n_key_tiles):
        vt_ref[n, 0:dh, :] = vs_ref[0, 0, :, n * KEY_TILE:(n + 1) * KEY_TILE]
        vt_ref[n_key_tiles + n, 0:dh, :] = vw_ref[0, 0, :, n * KEY_TILE:(n + 1) * KEY_TILE]
        vt_ref[n, dh:VT_ROWS, :] = ones_rows
        vt_ref[n_key_tiles + n, dh:VT_ROWS, :] = ones_rows

    s_ref[...] = jnp.full(s_ref.shape, NEG, F32)
    mt_ref[...] = jnp.full(mt_ref.shape, NEG, F32)
    p_ref[...] = jnp.zeros(p_ref.shape, BF16)
    al_ref[...] = jnp.ones(al_ref.shape, F32)
    m_ref[...] = jnp.full(m_ref.shape, NEG, F32)
    acc_ref[...] = jnp.ones(acc_ref.shape, F32)
    oc_ref[...] = jnp.zeros(oc_ref.shape, F32)

    def prologue(qb):
        slot = qb % ACC_RING
        qa = q_ref[0, 0, qb]
        qt = jnp.concatenate([qa[g * dh:(g + 1) * dh, :] for g in range(GQA_GROUP)], axis=1)
        t_row = qb * ATT_Q + lax.broadcasted_iota(jnp.int32, (1, ATT_Q), 1)
        t4 = jnp.concatenate([t_row] * GQA_GROUP, axis=1)

        n_cmp = kcv_ref.shape[2]
        q_c = jnp.concatenate([qt, jnp.zeros((kcv_ref.shape[3] - dh, nq), BF16)], axis=0)
        s_c = _dot(kcv_ref[0, 0], q_c)
        cmp_end = lax.broadcasted_iota(jnp.int32, (n_cmp, 1), 0) * CMP_STRIDE + (CMP_BLOCK - 1)
        vis = cmp_end <= t4
        s_c = jnp.where(vis, s_c, NEG)
        m_c = jnp.max(s_c, axis=0, keepdims=True)
        e_c = jnp.where(vis, jnp.exp2(s_c - m_c), 0.0)
        p_c = e_c / jnp.maximum(jnp.sum(e_c, axis=0, keepdims=True), 1e-30)
        oc_ref[slot] = _dot(vct_ref[0, 0], p_c.astype(BF16))

        n_sel = seq // SEL_BLOCK
        p_sum = sum(p_c[:, g * ATT_Q:(g + 1) * ATT_Q] for g in range(GQA_GROUP))
        jj = lax.broadcasted_iota(jnp.int32, (n_sel, n_cmp), 0) * SEL_BLOCK
        nn = lax.broadcasted_iota(jnp.int32, (n_sel, n_cmp), 1) * CMP_STRIDE
        ov = jnp.minimum(nn + CMP_BLOCK, jj + SEL_BLOCK) - jnp.maximum(nn, jj)
        ov_t = (jnp.maximum(ov, 0).astype(F32) * (1.0 / CMP_BLOCK)).astype(BF16)
        p_hi, p_lo = _split(p_sum)
        imp = _dot(ov_t, p_hi) + _dot(ov_t, p_lo)
        jb = lax.broadcasted_iota(jnp.int32, (n_sel, 1), 0)
        cur = t_row // SEL_BLOCK
        allowed = jb <= cur
        forced = (jb == 0) | (jb == cur) | (jb == cur - 1)
        score = jnp.where(forced & allowed, FORCE, jnp.where(allowed, imp, NEG))
        sc_ref[...] = score
        n_top = min(SEL_TOP, n_sel)

        def selection_bias(n_blk):
            sub = 8
            groups = [sc_ref[r:r + sub, :] for r in range(0, n_blk, sub)]
            ranks = [jnp.zeros((sub, ATT_Q), F32) for _ in groups]
            for k in range(n_blk):
                row = jnp.broadcast_to(sc_ref[k:k + 1, :], (sub, ATT_Q))
                for r, grp in enumerate(groups):
                    if r * sub > k:
                        ahead = row >= grp
                    elif r * sub + sub - 1 < k:
                        ahead = row > grp
                    else:
                        later = (r * sub + lax.broadcasted_iota(jnp.int32, (sub, 1), 0)) > k
                        ahead = (row > grp) | ((row == grp) & later)
                    ranks[r] = ranks[r] + jnp.where(ahead, 1.0, 0.0)
            chosen = (jnp.concatenate(ranks, axis=0) < n_top) & (jnp.concatenate(groups, axis=0) > 0.5 * NEG)
            bias = jnp.where(chosen, 0.0, NEG).astype(BF16)
            rest = bias_ref.shape[0] - n_blk
            return jnp.concatenate([bias, jnp.full((rest, ATT_Q), NEG, BF16)], axis=0) if rest else bias

        n_buckets = -(-n_sel // RANK_BUCKET)
        bucket = jnp.minimum((qb + 1) * (ATT_Q // SEL_BLOCK) - 1, n_sel - 1) // RANK_BUCKET
        for b in range(n_buckets):
            @pl.when(bucket == b)
            def _():
                bias_ref[...] = selection_bias(min((b + 1) * RANK_BUCKET, n_sel))
        qaug_ref[qb % 2] = jnp.concatenate([qt, jnp.concatenate([bias_ref[...]] * GQA_GROUP, axis=1)], axis=0)

        for br in range(2):
            m_ref[2 * slot + br] = jnp.full((1, nq), NEG, F32)
            acc_ref[2 * slot + br] = jnp.zeros((VT_ROWS, nq), F32)

    def epilogue(qb, slot):
        a_s, a_w = acc_ref[2 * slot], acc_ref[2 * slot + 1]
        o_s = a_s[:dh] / a_s[dh:dh + 1]
        o_w = a_w[:dh] / a_w[dh:dh + 1]
        gates = g_ref[0, 0, qb]

        def gate_row(br):
            return jnp.concatenate([gates[br * GQA_GROUP + g:br * GQA_GROUP + g + 1, :] for g in range(GQA_GROUP)],
                                   axis=1)
        o = gate_row(0) * oc_ref[slot] + gate_row(1) * o_s + gate_row(2) * o_w
        rows = pl.ds(pl.multiple_of(qb * ATT_Q, ATT_Q), ATT_Q)
        o_ref[0, rows, :] = jnp.concatenate(
            [o[:, g * ATT_Q:(g + 1) * ATT_Q].T for g in range(GQA_GROUP)], axis=1).astype(BF16)

    def stage_a(t, u):
        tile, pat, qs = jt_ref[t], jp_ref[t], jq_ref[t]
        k = kall_ref[pl.ds(pl.multiple_of(tile * KEY_TILE, KEY_TILE), KEY_TILE), :]
        s = _dot(k, qaug_ref[qs]) + jnp.concatenate([pat_ref[pat]] * GQA_GROUP, axis=1)
        s_ref[u] = s
        mt_ref[u] = jnp.max(s, axis=0, keepdims=True)

    def stage_b(t, u):
        a = ja_ref[t]
        m_old = m_ref[a]
        m_new = jnp.maximum(m_old, mt_ref[u])
        alpha = jnp.exp2(m_old - m_new)
        m_ref[a] = m_new
        al_ref[u] = alpha
        p_ref[u] = jnp.exp2(s_ref[u] - m_new).astype(BF16)

    def stage_c(t, u):
        tile, a = jt_ref[t], ja_ref[t]
        acc_ref[a] = al_ref[u] * acc_ref[a] + _dot(vt_ref[tile], p_ref[u])

    def step(g, carry):
        t0 = g * u_jobs
        for u in range(u_jobs):
            stage_c(t0 + u, u)
        for u in range(u_jobs):
            stage_b(t0 + u_jobs + u, u)
        for u in range(u_jobs):
            stage_a(t0 + 2 * u_jobs + u, u)
        return carry

    prologue(0)

    def block(i, carry):
        prologue(jnp.minimum(i + 1, n_qb - 1))
        epilogue(jnp.maximum(i - EPILOGUE_LAG, 0), (i - EPILOGUE_LAG) % ACC_RING)
        lax.fori_loop(lo_ref[i], lo_ref[i + 1], step, 0)
        return carry

    lax.fori_loop(0, n_qb, block, 0)
    for qb in range(max(n_qb - EPILOGUE_LAG, 0), n_qb):
        epilogue(qb, qb % ACC_RING)


def _attention(q_t, gates_t, kcv, vc_t, ks_aug, kw_aug, vs_t, vw_t):
    bsz, hkv, s, kdim = ks_aug.shape
    dh = vs_t.shape[2]
    n_cmp = kcv.shape[2]
    gd = GQA_GROUP * dh
    nq = GQA_GROUP * ATT_Q
    n_qb = s // ATT_Q
    patterns = _mask_patterns()
    tables = _job_tables(s)
    per_head = lambda shape: pl.BlockSpec((1, 1) + shape, lambda b, h, *_: (b, h) + (0,) * len(shape))
    n_acc = 2 * ACC_RING + 1
    return pl.pallas_call(
        _attn_kernel,
        grid_spec=pltpu.PrefetchScalarGridSpec(
            num_scalar_prefetch=len(tables),
            grid=(bsz, hkv),
            in_specs=[
                per_head((n_qb, gd, ATT_Q)),
                per_head((n_qb, 3 * GQA_GROUP, ATT_Q)),
                per_head((n_cmp, kcv.shape[3])),
                per_head((dh, n_cmp)),
                per_head((s, kdim)),
                per_head((s, kdim)),
                per_head((dh, s)),
                per_head((dh, s)),
                pl.BlockSpec(patterns.shape, lambda b, h, *_: (0, 0, 0)),
            ],
            out_specs=pl.BlockSpec((1, s, gd), lambda b, h, *_: (b, 0, h)),
            scratch_shapes=[
                pltpu.VMEM((2 * s, kdim), BF16),
                pltpu.VMEM((2 * (s // KEY_TILE), VT_ROWS, KEY_TILE), BF16),
                pltpu.VMEM((s // SEL_BLOCK, ATT_Q), F32),
                pltpu.VMEM((kdim - dh, ATT_Q), BF16),
                pltpu.VMEM((2, kdim, nq), BF16),
                pltpu.VMEM((ATT_UNROLL, KEY_TILE, nq), F32),
                pltpu.VMEM((ATT_UNROLL, KEY_TILE, nq), BF16),
                pltpu.VMEM((ATT_UNROLL, 1, nq), F32),
                pltpu.VMEM((ATT_UNROLL, 1, nq), F32),
                pltpu.VMEM((n_acc, 1, nq), F32),
                pltpu.VMEM((n_acc, VT_ROWS, nq), F32),
                pltpu.VMEM((ACC_RING, dh, nq), F32),
            ],
        ),
        out_shape=jax.ShapeDtypeStruct((bsz, s, hkv * gd), BF16),
        compiler_params=_params("parallel", "arbitrary"),
        name="nsa_attention",
    )(*tables, q_t, gates_t, kcv, vc_t, ks_aug, kw_aug, vs_t, vw_t, patterns)


def _oproj_kernel(o_ref, x_ref, mod_ref, w_ref, g_ref, b_ref, out_ref):
    y = _dot(o_ref[0], w_ref[...])
    out_ref[0] = _layer_norm(ALPHA * x_ref[0] + mod_ref[0, 2:3, :] * y, g_ref[...], b_ref[...])


def _out_projection(o, x, mod, w_o, ln_g, ln_b):
    bsz, s, d = x.shape
    tm = 512
    return pl.pallas_call(
        _oproj_kernel,
        grid=(bsz, s // tm),
        in_specs=[
            pl.BlockSpec((1, tm, o.shape[-1]), lambda b, j: (b, j, 0)),
            pl.BlockSpec((1, tm, d), lambda b, j: (b, j, 0)),
            pl.BlockSpec((1, 6, d), lambda b, j: (b, 0, 0)),
            _const_spec(w_o.shape),
            _const_spec((1, d)),
            _const_spec((1, d)),
        ],
        out_specs=pl.BlockSpec((1, tm, d), lambda b, j: (b, j, 0)),
        out_shape=jax.ShapeDtypeStruct((bsz, s, d), F32),
        compiler_params=_params("parallel", "parallel"),
        name="nsa_out_proj",
    )(o, x, mod, w_o, ln_g.reshape(1, d), ln_b.reshape(1, d))


def _moe_block(x, mod, router_wt, router_bias, w_gu, w_down, layer, ln_g, ln_b):
    bsz, s, d = x.shape
    t = bsz * s
    xf = x.reshape(t, d)
    wcol, route, counts = _router(xf, mod, router_wt, router_bias, s)

    cnt = counts[:, 0]
    padded = (cnt + EXPERT_TILE - 1) // EXPERT_TILE * EXPERT_TILE
    ends = jnp.cumsum(padded)
    offs = ends - padded
    n_tiles = (TOP_K * t) // EXPERT_TILE + N_EXPERTS
    tile_start = jnp.arange(n_tiles, dtype=jnp.int32) * EXPERT_TILE
    tile_expert = jnp.minimum(jnp.sum(tile_start[:, None] >= ends[None, :], axis=1), N_EXPERTS - 1).astype(jnp.int32)
    n_used = (ends[-1:] // EXPERT_TILE).astype(jnp.int32)
    experts, ranks = route[:TOP_K], route[TOP_K:2 * TOP_K]
    dst = ranks + sum(jnp.where(experts == e, offs[e], 0) for e in range(N_EXPERTS))
    idx = dst.reshape(TOP_K, t // TOKEN_TILE, TOKEN_TILE).transpose(1, 0, 2).reshape(t // TOKEN_TILE, 1, TOP_K * TOKEN_TILE)

    last_tile = jnp.where(cnt > 0, ends - EXPERT_TILE, -1)
    spare = n_used + jnp.arange(N_EXPERTS)
    spare = jnp.where(spare < n_tiles, spare * EXPERT_TILE, -1)
    xs = _dispatch(jnp.concatenate([last_tile, spare]).astype(jnp.int32), idx, xf, mod, n_tiles * EXPERT_TILE, s)
    ys = _experts(tile_expert, n_used, xs, w_gu, w_down, layer)
    out = _combine(idx, xf, mod, wcol, ys, ln_g, ln_b, s)
    return out.reshape(bsz, s, d)


def _nsa_layer(x, mod, w_kv, cmp_pe, cmp_w1, cmp_b1, cmp_w2, cmp_b2, w_qg, b_g, w_o, ln_g, ln_b):
    bsz, s, d = x.shape
    hkv, dh, grp = N_KV_HEADS, HEAD_DIM, GQA_GROUP
    hw = hkv * dh
    kvw = w_kv.reshape(d, 6, hw)
    kvc_w = jnp.stack([kvw[:, 0].reshape(d, hkv, dh), kvw[:, 1].reshape(d, hkv, dh)], axis=2).reshape(d, 2 * hw)
    w_nat = jnp.concatenate([kvc_w, kvw[:, 2], kvw[:, 4]], axis=1).astype(BF16)
    w_vt = jnp.concatenate([kvw[:, 3], kvw[:, 5]], axis=1).T.astype(BF16)
    nq = N_HEADS * dh
    wg = w_qg[:, nq:].reshape(d, hkv, grp, 3).transpose(0, 1, 3, 2).reshape(d, 3 * N_HEADS)
    w_qgt = jnp.concatenate([w_qg[:, :nq], wg], axis=1).T.astype(BF16)
    bg = b_g.reshape(hkv, grp, 3).transpose(0, 2, 1).reshape(3 * N_HEADS, 1)
    kvc, ks_aug, kw_aug, vs_t, vw_t, q_t, gates_t = _projections(x, mod, w_nat, w_vt, w_qgt, bg)
    cmp_kv, cmp_vt = _compress(kvc, cmp_pe, cmp_w1, cmp_b1, cmp_w2, cmp_b2)
    o = _attention(q_t, gates_t, cmp_kv, cmp_vt, ks_aug, kw_aug, vs_t, vw_t)
    return _out_projection(o, x, mod, w_o.astype(BF16), ln_g, ln_b)


def kernel(x, c, ada_w, ada_b, ln_g, ln_b, conv_w_in, conv_w, conv_b, conv_w_out, w_kv, cmp_pe, cmp_w1, cmp_b1, cmp_w2, cmp_b2, w_qg, b_g, w_o, router_w, router_bias, w_gu, w_down):
    bsz, s, d = x.shape
    mod = _modulation(c, ada_w, ada_b).reshape(DEPTH, bsz, 6, d)
    router_wt = router_w.T

    x = _conv_layer(x, mod[0], conv_w_in[0].astype(BF16), conv_w[0], conv_b[0], conv_w_out[0].astype(BF16),
                    ln_g[0, 0], ln_b[0, 0])
    x = _moe_block(x, mod[0], router_wt, router_bias, w_gu, w_down, 0, ln_g[0, 1], ln_b[0, 1])

    x = _nsa_layer(x, mod[1], w_kv, cmp_pe, cmp_w1, cmp_b1, cmp_w2, cmp_b2, w_qg[0], b_g[0], w_o[0],
                   ln_g[1, 0], ln_b[1, 0])
    x = _moe_block(x, mod[1], router_wt, router_bias, w_gu, w_down, 1, ln_g[1, 1], ln_b[1, 1])
    return x
```

```python
import functools
import math

import jax
import jax.numpy as jnp
import numpy as np
from jax import lax
from jax.experimental import pallas as pl
from jax.experimental.pallas import tpu as pltpu

F32 = jnp.float32
BF16 = jnp.bfloat16

D_MODEL = 1024
DEPTH = 2
N_A_LAYERS = DEPTH // 2
CONV_WIDTH = 3
N_HEADS = 16
HEAD_DIM = D_MODEL // N_HEADS
N_KV_HEADS = 4
GQA_GROUP = N_HEADS // N_KV_HEADS
CMP_BLOCK = 32
CMP_STRIDE = 16
CMP_HIDDEN = 4 * HEAD_DIM
SEL_BLOCK = 64
SEL_TOP = 16
WINDOW = 512
N_EXPERTS = 16
N_GROUPS = 4
EXPERTS_PER_GROUP = N_EXPERTS // N_GROUPS
TOP_K = 2
D_FF_EXPERT = D_MODEL // 2
ALPHA = (2 * DEPTH) ** 0.25
LN_EPS = 1e-5
NEG = -1e30
FORCE = 1e9

LANES = 128
VMEM_LIMIT_BYTES = 56 * 1024 * 1024

LOG2E = math.log2(math.e)

ATT_Q = 256
KEY_TILE = 256
PAT_ZERO, PAT_DIAG, PAT_ANTI, PAT_NONE = range(4)
ATT_UNROLL = 4
ACC_RING = 8
EPILOGUE_LAG = 3
RANK_BUCKET = 16
VT_ROWS = HEAD_DIM + 16


def _dot(a, b):
    return jnp.dot(a, b, preferred_element_type=F32)


def _dot_nt(a, b):
    return lax.dot_general(a, b, (((1,), (1,)), ((), ())), preferred_element_type=F32)


def _split(x):
    hi = x.astype(BF16)
    lo = (x - hi.astype(F32)).astype(BF16)
    return hi, lo


def _dot3(a, b):
    ah, al = _split(a)
    bh, bl = _split(b)
    return _dot(ah, bh) + (_dot(ah, bl) + _dot(al, bh))


def _dot3_nt(a, b):
    ah, al = _split(a)
    bh, bl = _split(b)
    return _dot_nt(ah, bh) + (_dot_nt(ah, bl) + _dot_nt(al, bh))


def _layer_norm(r, g, b):
    mu = jnp.mean(r, axis=-1, keepdims=True)
    d = r - mu
    var = jnp.mean(d * d, axis=-1, keepdims=True)
    return d * lax.rsqrt(var + LN_EPS) * g + b


def _params(*sem):
    return pltpu.CompilerParams(dimension_semantics=sem, vmem_limit_bytes=VMEM_LIMIT_BYTES)


def _const_spec(shape):
    zeros = (0,) * len(shape)
    return pl.BlockSpec(shape, lambda *_: zeros)


def _mod_kernel(c_ref, w_ref, b_ref, o_ref):
    c = c_ref[...]
    s = c * jax.nn.sigmoid(c)
    o_ref[0] = _dot3(s, w_ref[0]) + b_ref[0]


def _modulation(c, ada_w, ada_b):
    depth, d, n = ada_w.shape
    bsz = c.shape[0]
    tn = 1536
    return pl.pallas_call(
        _mod_kernel,
        grid=(depth, n // tn),
        in_specs=[
            pl.BlockSpec((bsz, d), lambda l, j: (0, 0)),
            pl.BlockSpec((1, d, tn), lambda l, j: (l, 0, j)),
            pl.BlockSpec((1, 1, tn), lambda l, j: (l, 0, j)),
        ],
        out_specs=pl.BlockSpec((1, bsz, tn), lambda l, j: (l, 0, j)),
        out_shape=jax.ShapeDtypeStruct((depth, bsz, n), F32),
        compiler_params=_params("parallel", "parallel"),
        name="adaln_mod",
    )(c, ada_w, ada_b.reshape(depth, 1, n))


CONV_HALO = 8


def _conv_kernel(x_ref, mod_ref, win_ref, cw_ref, cb_ref, wout_ref, g_ref, b_ref, o_ref, z_ref):
    tm = x_ref.shape[1]
    d = x_ref.shape[2]

    @pl.when(pl.program_id(1) == 0)
    def _():
        z_ref[0:CONV_HALO, :] = jnp.zeros((CONV_HALO, d), F32)

    x = x_ref[0]
    sh = mod_ref[0, 0:1, :]
    sc = mod_ref[0, 1:2, :]
    gate = mod_ref[0, 2:3, :]
    u = (x * (1.0 + sc) + sh).astype(BF16)
    bch = _dot(u, win_ref[...])
    z = bch[:, d:2 * d] * bch[:, 2 * d:]
    z_ref[CONV_HALO:CONV_HALO + tm, :] = z
    z1 = z_ref[CONV_HALO - 1:CONV_HALO - 1 + tm, :]
    z2 = z_ref[CONV_HALO - 2:CONV_HALO - 2 + tm, :]
    conv = cw_ref[0:1, :] * z2 + cw_ref[1:2, :] * z1 + cw_ref[2:3, :] * z + cb_ref[...]
    v = (bch[:, :d] * conv).astype(BF16)
    y = _dot(v, wout_ref[...])
    o_ref[0] = _layer_norm(ALPHA * x + gate * y, g_ref[...], b_ref[...])
    z_ref[0:CONV_HALO, :] = z_ref[tm:tm + CONV_HALO, :]


def _conv_layer(x, mod, w_in, conv_w, conv_b, w_out, ln_g, ln_b):
    bsz, s, d = x.shape
    tm = 512
    return pl.pallas_call(
        _conv_kernel,
        grid=(bsz, s // tm),
        in_specs=[
            pl.BlockSpec((1, tm, d), lambda b, j: (b, j, 0)),
            pl.BlockSpec((1, 6, d), lambda b, j: (b, 0, 0)),
            _const_spec((d, 3 * d)),
            _const_spec((CONV_WIDTH, d)),
            _const_spec((1, d)),
            _const_spec((d, d)),
            _const_spec((1, d)),
            _const_spec((1, d)),
        ],
        out_specs=pl.BlockSpec((1, tm, d), lambda b, j: (b, j, 0)),
        out_shape=jax.ShapeDtypeStruct((bsz, s, d), F32),
        scratch_shapes=[pltpu.VMEM((tm + CONV_HALO, d), F32)],
        compiler_params=_params("arbitrary", "arbitrary"),
        name="conv_mixer",
    )(x, mod, w_in, conv_w, conv_b.reshape(1, d), w_out, ln_g.reshape(1, d), ln_b.reshape(1, d))


def _router_kernel(x_ref, mod_ref, rwt_ref, rb_ref, wcol_ref, route_ref, count_ref, base_ref, tri_ref):
    tm = x_ref.shape[0]
    x = x_ref[...]
    u = x * (1.0 + mod_ref[0, 4:5, :]) + mod_ref[0, 3:4, :]
    logits = _dot3_nt(rwt_ref[...], u)
    m = jnp.max(logits, axis=0, keepdims=True)
    e = jnp.exp(logits - m)
    aff = e / jnp.sum(e, axis=0, keepdims=True)
    biased = aff + rb_ref[...]
    aff_r = [aff[i:i + 1, :] for i in range(N_EXPERTS)]
    row = [biased[i:i + 1, :] for i in range(N_EXPERTS)]

    best_s, best = None, None
    for g in range(N_GROUPS):
        r = row[g * EXPERTS_PER_GROUP:(g + 1) * EXPERTS_PER_GROUP]
        gs = None
        for i in range(EXPERTS_PER_GROUP):
            for j in range(i + 1, EXPERTS_PER_GROUP):
                p = r[i] + r[j]
                gs = p if gs is None else jnp.maximum(gs, p)
        if g == 0:
            best_s, best = gs, jnp.zeros((1, tm), jnp.int32)
        else:
            upd = gs > best_s
            best = jnp.where(upd, g, best)
            best_s = jnp.where(upd, gs, best_s)

    masked = [jnp.where(best == (i // EXPERTS_PER_GROUP), row[i], NEG) for i in range(N_EXPERTS)]

    def first_argmax(vals):
        v, idx = vals[0], jnp.zeros((1, tm), jnp.int32)
        for i in range(1, N_EXPERTS):
            upd = vals[i] > v
            idx = jnp.where(upd, i, idx)
            v = jnp.where(upd, vals[i], v)
        return idx

    i0 = first_argmax(masked)
    i1 = first_argmax([jnp.where(i0 == i, -jnp.inf, masked[i]) for i in range(N_EXPERTS)])
    w0 = sum(jnp.where(i0 == i, aff_r[i], 0.0) for i in range(N_EXPERTS))
    w1 = sum(jnp.where(i1 == i, aff_r[i], 0.0) for i in range(N_EXPERTS))
    tot = w0 + w1
    w0 = w0 / tot
    w1 = w1 / tot
    w_t = jnp.concatenate([w0, w1, jnp.zeros((LANES - TOP_K, tm), F32)], axis=0)
    wcol_ref[...] = w_t.T

    @pl.when(pl.program_id(0) == 0)
    def _():
        base_ref[...] = jnp.zeros(base_ref.shape, F32)
        r_i = lax.broadcasted_iota(jnp.int32, tri_ref.shape, 0)
        c_i = lax.broadcasted_iota(jnp.int32, tri_ref.shape, 1)
        tri_ref[...] = jnp.where(r_i <= c_i, 1.0, 0.0).astype(BF16)

    hot = jnp.concatenate([jnp.where((i0 == i) | (i1 == i), 1.0, 0.0) for i in range(N_EXPERTS)], axis=0)
    incl = _dot(hot.astype(BF16), tri_ref[...])
    pos = base_ref[...] + (incl - hot)
    rank0 = sum(jnp.where(i0 == i, pos[i:i + 1, :], 0.0) for i in range(N_EXPERTS))
    rank1 = sum(jnp.where(i1 == i, pos[i:i + 1, :], 0.0) for i in range(N_EXPERTS))
    base = base_ref[...] + jnp.sum(hot, axis=1, keepdims=True)
    base_ref[...] = base
    route_ref[...] = jnp.concatenate(
        [i0, i1, rank0.astype(jnp.int32), rank1.astype(jnp.int32), jnp.zeros((4, tm), jnp.int32)], axis=0)
    count_ref[...] = jnp.broadcast_to(base, count_ref.shape).astype(jnp.int32)


def _router(x, mod, router_wt, router_bias, seq):
    t, d = x.shape
    tm = 1024
    return pl.pallas_call(
        _router_kernel,
        grid=(t // tm,),
        in_specs=[
            pl.BlockSpec((tm, d), lambda i: (i, 0)),
            pl.BlockSpec((1, 6, d), lambda i: ((i * tm) // seq, 0, 0)),
            _const_spec((N_EXPERTS, d)),
            _const_spec((N_EXPERTS, 1)),
        ],
        out_specs=(
            pl.BlockSpec((tm, LANES), lambda i: (i, 0)),
            pl.BlockSpec((8, tm), lambda i: (0, i)),
            _const_spec((N_EXPERTS, LANES)),
        ),
        out_shape=(
            jax.ShapeDtypeStruct((t, LANES), F32),
            jax.ShapeDtypeStruct((8, t), jnp.int32),
            jax.ShapeDtypeStruct((N_EXPERTS, LANES), jnp.int32),
        ),
        scratch_shapes=[pltpu.VMEM((N_EXPERTS, 1), F32), pltpu.VMEM((tm, tm), BF16)],
        compiler_params=_params("arbitrary"),
        name="moe_router",
    )(x, mod, router_wt, router_bias.reshape(N_EXPERTS, 1))


EXPERT_TILE = 512
TOKEN_TILE = 512
ROW_COPY_UNROLL = 8
ROW_TILES = D_MODEL // LANES


def _store_rows(ref, x):
    rows = x.shape[0]
    for c in range(ROW_TILES):
        ref[pl.ds(c, rows, stride=ROW_TILES), :] = x[:, c * LANES:(c + 1) * LANES]


def _load_rows(ref, rows):
    return jnp.concatenate([ref[pl.ds(c, rows, stride=ROW_TILES), :] for c in range(ROW_TILES)], axis=1)


def _row(ref, r):
    return ref.at[pl.ds(pl.multiple_of(r * ROW_TILES, ROW_TILES), ROW_TILES)]


def _start_row_copies(tm, row_copy):
    def start(t, c):
        for k in range(TOP_K):
            row_copy(t, k).start(priority=k % 2)
        return c

    lax.fori_loop(0, tm, start, 0, unroll=ROW_COPY_UNROLL)


def _dispatch_kernel(pad_ref, idx_ref, x_ref, mod_ref, xs_ref, u_ref, sem):
    tm = x_ref.shape[0]

    i = pl.program_id(0)
    slot = i % 2

    @pl.when(i == 0)
    def _():
        u_ref[1] = jnp.zeros(u_ref.shape[1:], F32)
        for wait in (False, True):
            for e in range(pad_ref.shape[0]):
                @pl.when(pad_ref[e] >= 0)
                def _():
                    row = pl.multiple_of(jnp.maximum(pad_ref[e], 0) * ROW_TILES, tm * ROW_TILES)
                    fill = pltpu.make_async_copy(u_ref.at[1], xs_ref.at[pl.ds(row, tm * ROW_TILES)], sem.at[1])
                    fill.wait() if wait else fill.start()

    _store_rows(u_ref.at[slot], x_ref[...] * (1.0 + mod_ref[0, 4:5, :]) + mod_ref[0, 3:4, :])

    def row_copy(t, k):
        dst = idx_ref[0, 0, k * tm + t]
        return pltpu.make_async_copy(_row(u_ref.at[slot], t), _row(xs_ref, dst), sem.at[slot])

    _start_row_copies(tm, row_copy)

    def wait_tile(s):
        for _ in range(TOP_K):
            pltpu.make_async_copy(u_ref.at[s], xs_ref.at[pl.ds(0, tm * ROW_TILES)], sem.at[s]).wait()

    @pl.when(i > 0)
    def _():
        wait_tile(1 - slot)

    @pl.when(i == pl.num_programs(0) - 1)
    def _():
        wait_tile(slot)


def _dispatch(pad_start, idx, x, mod, n_rows, seq):
    t, d = x.shape
    tm = TOKEN_TILE
    assert tm == EXPERT_TILE
    return pl.pallas_call(
        _dispatch_kernel,
        grid_spec=pltpu.PrefetchScalarGridSpec(
            num_scalar_prefetch=1,
            grid=(t // tm,),
            in_specs=[
                pl.BlockSpec((1, 1, TOP_K * tm), lambda i, pad: (i, 0, 0), memory_space=pltpu.SMEM),
                pl.BlockSpec((tm, d), lambda i, pad: (i, 0)),
                pl.BlockSpec((1, 6, d), lambda i, pad: ((i * tm) // seq, 0, 0)),
            ],
            out_specs=pl.BlockSpec(memory_space=pl.ANY),
            scratch_shapes=[pltpu.VMEM((2, tm * ROW_TILES, LANES), F32), pltpu.SemaphoreType.DMA((2,))],
        ),
        out_shape=jax.ShapeDtypeStruct((n_rows * ROW_TILES, LANES), F32),
        compiler_params=_params("arbitrary"),
        name="moe_dispatch",
    )(pad_start, idx, x, mod)


def _expert_kernel(te_ref, first_ref, nu_ref, xs_ref, wgu_ref, wdn_ref, ys_ref, wgu_bf, wdn_bf):
    del te_ref
    s = pl.program_id(0)
    used = s < nu_ref[0]

    @pl.when(used & (first_ref[s] == 1))
    def _():
        wgu_bf[...] = wgu_ref[0, 0].astype(BF16)
        wdn_bf[...] = wdn_ref[0, 0].astype(BF16)

    @pl.when(used)
    def _():
        f = wdn_bf.shape[0]
        h = _dot(_load_rows(xs_ref, EXPERT_TILE).astype(BF16), wgu_bf[...])
        gate = h[:, :f]
        a = (gate * jax.nn.sigmoid(gate) * h[:, f:]).astype(BF16)
        _store_rows(ys_ref, _dot(a, wdn_bf[...]))

    @pl.when(jnp.logical_not(used))
    def _():
        ys_ref[...] = jnp.zeros(ys_ref.shape, F32)


def _experts(tile_expert, n_used, xs, w_gu, w_down, layer):
    _, _, d, f2 = w_gu.shape
    f = f2 // 2
    first = jnp.concatenate([jnp.ones((1,), jnp.int32), (tile_expert[1:] != tile_expert[:-1]).astype(jnp.int32)])
    row_map = lambda s, te, fi, nu: (s, 0)
    in_row_map = lambda s, te, fi, nu: (jnp.minimum(s, nu[0]), 0)
    w_map = lambda s, te, fi, nu: (layer, te[s], 0, 0)
    return pl.pallas_call(
        _expert_kernel,
        grid_spec=pltpu.PrefetchScalarGridSpec(
            num_scalar_prefetch=3,
            grid=(tile_expert.shape[0],),
            in_specs=[
                pl.BlockSpec((EXPERT_TILE * ROW_TILES, LANES), in_row_map),
                pl.BlockSpec((1, 1, d, f2), w_map),
                pl.BlockSpec((1, 1, f, d), w_map),
            ],
            out_specs=pl.BlockSpec((EXPERT_TILE * ROW_TILES, LANES), row_map),
            scratch_shapes=[pltpu.VMEM((d, f2), BF16), pltpu.VMEM((f, d), BF16)],
        ),
        out_shape=jax.ShapeDtypeStruct(xs.shape, F32),
        compiler_params=_params("arbitrary"),
        name="moe_experts",
    )(tile_expert, first, n_used, xs, w_gu, w_down)


def _combine_kernel(idx_ref, next_idx_ref, x_ref, mod_ref, w_ref, ys_ref, g_ref, b_ref, o_ref, y_ref, sem):
    tm = x_ref.shape[0]
    i = pl.program_id(0)
    slot = i % 2

    def gather(ids_ref, into):
        def row_copy(t, k):
            src = ids_ref[0, 0, k * tm + t]
            return pltpu.make_async_copy(_row(ys_ref, src), _row(y_ref.at[into, k], t), sem.at[into])

        _start_row_copies(tm, row_copy)

    @pl.when(i == 0)
    def _():
        gather(idx_ref, 0)

    @pl.when(i + 1 < pl.num_programs(0))
    def _():
        gather(next_idx_ref, 1 - slot)

    for k in range(TOP_K):
        pltpu.make_async_copy(ys_ref.at[pl.ds(0, tm * ROW_TILES)], y_ref.at[slot, k], sem.at[slot]).wait()
    w = w_ref[...]
    out = w[:, 0:1] * _load_rows(y_ref.at[slot, 0], tm) + w[:, 1:2] * _load_rows(y_ref.at[slot, 1], tm)
    r = ALPHA * x_ref[...] + mod_ref[0, 5:6, :] * out
    o_ref[...] = _layer_norm(r, g_ref[...], b_ref[...])


def _combine(idx, x, mod, wcol, ys, ln_g, ln_b, seq):
    t, d = x.shape
    tm = TOKEN_TILE
    n_tiles = t // tm
    return pl.pallas_call(
        _combine_kernel,
        grid=(n_tiles,),
        in_specs=[
            pl.BlockSpec((1, 1, TOP_K * tm), lambda i: (i, 0, 0), memory_space=pltpu.SMEM),
            pl.BlockSpec((1, 1, TOP_K * tm), lambda i: (jnp.minimum(i + 1, n_tiles - 1), 0, 0),
                         memory_space=pltpu.SMEM),
            pl.BlockSpec((tm, d), lambda i: (i, 0)),
            pl.BlockSpec((1, 6, d), lambda i: ((i * tm) // seq, 0, 0)),
            pl.BlockSpec((tm, LANES), lambda i: (i, 0)),
            pl.BlockSpec(memory_space=pl.ANY),
            _const_spec((1, d)),
            _const_spec((1, d)),
        ],
        out_specs=pl.BlockSpec((tm, d), lambda i: (i, 0)),
        out_shape=jax.ShapeDtypeStruct((t, d), F32),
        scratch_shapes=[pltpu.VMEM((2, TOP_K, tm * ROW_TILES, LANES), F32), pltpu.SemaphoreType.DMA((2,))],
        compiler_params=_params("arbitrary"),
        name="moe_combine",
    )(idx, idx, x, mod, wcol, ys, ln_g.reshape(1, d), ln_b.reshape(1, d))


def _proj_kernel(x_ref, mod_ref, wn_ref, wvt_ref, wqt_ref, bg_ref,
                 kvc_ref, ks_ref, kw_ref, vst_ref, vwt_ref, qt_ref, gt_ref):
    tm = x_ref.shape[1]
    j = pl.program_id(1)
    x = x_ref[0]
    xb = x.astype(BF16)
    nat = _dot(xb, wn_ref[...])
    vt = _dot_nt(wvt_ref[...], xb)
    u = (x * (1.0 + mod_ref[0, 1:2, :]) + mod_ref[0, 0:1, :]).astype(BF16)
    qg = _dot_nt(wqt_ref[...], u)
    nq = N_HEADS * HEAD_DIM
    q = (qg[:nq, :] * (HEAD_DIM ** -0.5 * LOG2E)).astype(BF16)
    gates = jax.nn.sigmoid(qg[nq:, :] + bg_ref[...])
    per = 3 * GQA_GROUP
    blk = (j * tm + lax.broadcasted_iota(jnp.int32, (tm, SEL_BLOCK), 0)) // SEL_BLOCK
    onehot = (blk == lax.broadcasted_iota(jnp.int32, (tm, SEL_BLOCK), 1)).astype(BF16)
    zeros = jnp.zeros((tm, SEL_BLOCK), BF16)
    hw = N_KV_HEADS * HEAD_DIM
    for h in range(N_KV_HEADS):
        c0 = h * HEAD_DIM
        kvc_ref[0, h] = nat[:, 2 * c0:2 * c0 + 2 * HEAD_DIM]
        ks = nat[:, 2 * hw + c0:2 * hw + c0 + HEAD_DIM].astype(BF16)
        ks_ref[0, h] = jnp.concatenate([ks, onehot], axis=1)
        kw = nat[:, 3 * hw + c0:3 * hw + c0 + HEAD_DIM].astype(BF16)
        kw_ref[0, h] = jnp.concatenate([kw, zeros], axis=1)
        vst_ref[0, h] = vt[c0:c0 + HEAD_DIM, :].astype(BF16)
        vwt_ref[0, h] = vt[hw + c0:hw + c0 + HEAD_DIM, :].astype(BF16)
        for p in range(tm // ATT_Q):
            cols = slice(p * ATT_Q, (p + 1) * ATT_Q)
            qt_ref[0, h, p] = q[h * GQA_GROUP * HEAD_DIM:(h + 1) * GQA_GROUP * HEAD_DIM, cols]
            gt_ref[0, h, p] = gates[h * per:(h + 1) * per, cols]


def _projections(x, mod, w_nat, w_vt, w_qgt, b_g):
    bsz, s, d = x.shape
    tm = 512
    hkv, dh = N_KV_HEADS, HEAD_DIM
    per = 3 * GQA_GROUP
    nqg = w_qgt.shape[0]
    out_shape = (
        jax.ShapeDtypeStruct((bsz, hkv, s, 2 * dh), F32),
        jax.ShapeDtypeStruct((bsz, hkv, s, 2 * dh), BF16),
        jax.ShapeDtypeStruct((bsz, hkv, s, 2 * dh), BF16),
        jax.ShapeDtypeStruct((bsz, hkv, dh, s), BF16),
        jax.ShapeDtypeStruct((bsz, hkv, dh, s), BF16),
        jax.ShapeDtypeStruct((bsz, hkv, s // ATT_Q, GQA_GROUP * dh, ATT_Q), BF16),
        jax.ShapeDtypeStruct((bsz, hkv, s // ATT_Q, per, ATT_Q), F32),
    )
    out_specs = (
        pl.BlockSpec((1, hkv, tm, 2 * dh), lambda b, j: (b, 0, j, 0)),
        pl.BlockSpec((1, hkv, tm, 2 * dh), lambda b, j: (b, 0, j, 0)),
        pl.BlockSpec((1, hkv, tm, 2 * dh), lambda b, j: (b, 0, j, 0)),
        pl.BlockSpec((1, hkv, dh, tm), lambda b, j: (b, 0, 0, j)),
        pl.BlockSpec((1, hkv, dh, tm), lambda b, j: (b, 0, 0, j)),
        pl.BlockSpec((1, hkv, tm // ATT_Q, GQA_GROUP * dh, ATT_Q), lambda b, j: (b, 0, j, 0, 0)),
        pl.BlockSpec((1, hkv, tm // ATT_Q, per, ATT_Q), lambda b, j: (b, 0, j, 0, 0)),
    )
    return pl.pallas_call(
        _proj_kernel,
        grid=(bsz, s // tm),
        in_specs=[
            pl.BlockSpec((1, tm, d), lambda b, j: (b, j, 0)),
            pl.BlockSpec((1, 6, d), lambda b, j: (b, 0, 0)),
            _const_spec(w_nat.shape),
            _const_spec(w_vt.shape),
            _const_spec(w_qgt.shape),
            _const_spec((nqg - N_HEADS * dh, 1)),
        ],
        out_specs=out_specs,
        out_shape=out_shape,
        compiler_params=_params("parallel", "parallel"),
        name="nsa_projections",
    )(x, mod, w_nat, w_vt, w_qgt, b_g)


def _compress_kernel(kvc_ref, pe_ref, w1_ref, b1_ref, w2_ref, b2_ref, nat_ref, tr_ref):
    n = kvc_ref.shape[2] // CMP_STRIDE
    hid2 = w1_ref.shape[-1]
    p = jnp.zeros((n, hid2), F32)
    q = jnp.zeros((n, hid2), F32)
    for l in range(CMP_STRIDE):
        x = kvc_ref[0, 0, pl.ds(l, n, stride=CMP_STRIDE), :]
        p = p + _dot((x + pe_ref[l:l + 1, :]).astype(BF16), w1_ref[0, l])
        q = q + _dot((x + pe_ref[CMP_STRIDE + l:CMP_STRIDE + l + 1, :]).astype(BF16), w1_ref[1, l])
    pre = p + pltpu.roll(q, n - 1, 0) + b1_ref[...]
    hdn = 0.5 * pre * (1.0 + jnp.tanh(0.7978845608028654 * (pre + 0.044715 * (pre * pre * pre))))
    out = _dot(hdn.astype(BF16), w2_ref[...]) + b2_ref[...]
    nat_ref[0, 0] = out.astype(BF16)
    tr_ref[0, 0] = out[:, HEAD_DIM:].T.astype(BF16)


def _compress(kvc, cmp_pe, cmp_w1, cmp_b1, cmp_w2, cmp_b2):
    bsz, hkv, s, two_dh = kvc.shape
    dh = two_dh // 2
    n = s // CMP_STRIDE
    hid = cmp_w1.shape[-1]
    zw = jnp.zeros((2, CMP_STRIDE, dh, hid), F32)
    w1 = cmp_w1.reshape(2, 2, CMP_STRIDE, dh, hid)
    w1 = jnp.concatenate([jnp.concatenate([w1[0], zw], axis=-1), jnp.concatenate([zw, w1[1]], axis=-1)], axis=-2)
    zd = jnp.zeros((hid, dh), F32)
    w2 = jnp.concatenate([jnp.concatenate([cmp_w2[0], zd], axis=1), jnp.concatenate([zd, cmp_w2[1]], axis=1)], axis=0)
    pe = jnp.concatenate([cmp_pe[0], cmp_pe[1]], axis=1)
    return pl.pallas_call(
        _compress_kernel,
        grid=(bsz, hkv),
        in_specs=[
            pl.BlockSpec((1, 1, s, two_dh), lambda b, h: (b, h, 0, 0)),
            _const_spec(pe.shape),
            _const_spec(w1.shape),
            _const_spec((1, 2 * hid)),
            _const_spec(w2.shape),
            _const_spec((1, two_dh)),
        ],
        out_specs=(
            pl.BlockSpec((1, 1, n, two_dh), lambda b, h: (b, h, 0, 0)),
            pl.BlockSpec((1, 1, dh, n), lambda b, h: (b, h, 0, 0)),
        ),
        out_shape=(
            jax.ShapeDtypeStruct((bsz, hkv, n, two_dh), BF16),
            jax.ShapeDtypeStruct((bsz, hkv, dh, n), BF16),
        ),
        compiler_params=_params("parallel", "parallel"),
        name="nsa_compress",
    )(kvc, pe, w1.astype(BF16), cmp_b1.reshape(1, 2 * hid), w2.astype(BF16), cmp_b2.reshape(1, two_dh))


def _mask_patterns():
    assert WINDOW == 2 * KEY_TILE and ATT_Q == KEY_TILE
    keyl = np.arange(KEY_TILE)[:, None]
    ql = np.arange(ATT_Q)[None, :]
    true = np.ones((KEY_TILE, ATT_Q), bool)
    valid = np.stack([
        true,
        keyl <= ql,
        keyl > ql,
        ~true,
    ])
    return jnp.asarray(np.where(valid, 0.0, NEG), F32)


def _compressed_visibility(seq):
    n_cmp = seq // CMP_STRIDE
    r = np.arange(-n_cmp, n_cmp)[:, None]
    ql = np.arange(ATT_Q)[None, :]
    return jnp.asarray(np.where(r * CMP_STRIDE + CMP_BLOCK - 1 <= ql, 0.0, NEG), F32)


def _job_tables(seq):
    nqb, nkt, u = seq // ATT_Q, seq // KEY_TILE, ATT_UNROLL
    tile, pat, acc, qs, base = [], [], [], [], []
    for i in range(nqb):
        base.append(len(tile))
        for j in range(i + 1):
            tile.append(j)
            pat.append(PAT_DIAG if j == i else PAT_ZERO)
            acc.append(2 * (i % ACC_RING))
            qs.append(i % 2)
        for w in range(min(i, WINDOW // KEY_TILE) + 1):
            tile.append(nkt + i - w)
            pat.append((PAT_DIAG, PAT_ZERO, PAT_ANTI)[w])
            acc.append(2 * (i % ACC_RING) + 1)
            qs.append(i % 2)
    n_real = len(tile)
    n_steps = (n_real - 1 + 2 * u) // u + 1
    pad = lambda x, fill: np.array([fill] * (2 * u) + x + [fill] * (n_steps * u - n_real), np.int32)
    step_lo = np.array([-(-b // u) for b in base] + [n_steps], np.int32)
    assert all(base[i + 2] >= step_lo[i + 1] * u for i in range(nqb - 2))
    return pad(tile, 0), pad(pat, PAT_NONE), pad(acc, 2 * ACC_RING), pad(qs, 0), step_lo


def _attn_kernel(jt_ref, jp_ref, ja_ref, jq_ref, lo_ref,
                 q_ref, g_ref, kcv_ref, vct_ref, ks_ref, kw_ref, vs_ref, vw_ref, pat_ref, cvis_ref, o_ref,
                 kall_ref, vt_ref, sc_ref, bias_ref, qaug_ref, s_ref, p_ref, al_ref, mt_ref, m_ref, acc_ref, oc_ref):
    seq = ks_ref.shape[2]
    nq = GQA_GROUP * ATT_Q
    dh = HEAD_DIM
    n_key_tiles = seq // KEY_TILE
    n_qb = seq // ATT_Q
    u_jobs = ATT_UNROLL

    kall_ref[0:seq, :] = ks_ref[0, 0]
    kall_ref[seq:2 * seq, :] = kw_ref[0, 0]
    ones_rows = (lax.broadcasted_iota(jnp.int32, (VT_ROWS - dh, KEY_TILE), 0) == 0).astype(BF16)
    for n in range(n_key_tiles):
        vt_ref[n, 0:dh, :] = vs_ref[0, 0, :, n * KEY_TILE:(n + 1) * KEY_TILE]
        vt_ref[n_key_tiles + n, 0:dh, :] = vw_ref[0, 0, :, n * KEY_TILE:(n + 1) * KEY_TILE]
        vt_ref[n, dh:VT_ROWS, :] = ones_rows
        vt_ref[n_key_tiles + n, dh:VT_ROWS, :] = ones_rows

    s_ref[...] = jnp.full(s_ref.shape, NEG, F32)
    mt_ref[...] = jnp.full(mt_ref.shape, NEG, F32)
    p_ref[...] = jnp.zeros(p_ref.shape, BF16)
    al_ref[...] = jnp.ones(al_ref.shape, F32)
    m_ref[...] = jnp.full(m_ref.shape, NEG, F32)
    acc_ref[...] = jnp.ones(acc_ref.shape, F32)
    oc_ref[...] = jnp.zeros(oc_ref.shape, F32)

    def prologue(qb):
        slot = qb % ACC_RING
        qa = q_ref[0, 0, qb]
        qt = jnp.concatenate([qa[g * dh:(g + 1) * dh, :] for g in range(GQA_GROUP)], axis=1)
        t_row = qb * ATT_Q + lax.broadcasted_iota(jnp.int32, (1, ATT_Q), 1)
        t4 = jnp.concatenate([t_row] * GQA_GROUP, axis=1)

        n_cmp = kcv_ref.shape[2]
        q_c = jnp.concatenate([qt, jnp.zeros((kcv_ref.shape[3] - dh, nq), BF16)], axis=0)
        per_qb = ATT_Q // CMP_STRIDE
        vis = cvis_ref[pl.ds(pl.multiple_of(n_cmp - per_qb * qb, per_qb), n_cmp), :]
        s_c = _dot(kcv_ref[0, 0], q_c) + jnp.concatenate([vis] * GQA_GROUP, axis=1)
        m_c = jnp.max(s_c, axis=0, keepdims=True)
        m_c = jnp.where(t4 >= CMP_BLOCK - 1, m_c, 0.0)
        e_c = jnp.exp2(s_c - m_c)
        p_c = e_c / jnp.maximum(jnp.sum(e_c, axis=0, keepdims=True), 1e-30)
        oc_ref[slot] = _dot(vct_ref[0, 0], p_c.astype(BF16))

        n_sel = seq // SEL_BLOCK
        p_sum = sum(p_c[:, g * ATT_Q:(g + 1) * ATT_Q] for g in range(GQA_GROUP))
        jj = lax.broadcasted_iota(jnp.int32, (n_sel, n_cmp), 0) * SEL_BLOCK
        nn = lax.broadcasted_iota(jnp.int32, (n_sel, n_cmp), 1) * CMP_STRIDE
        ov = jnp.minimum(nn + CMP_BLOCK, jj + SEL_BLOCK) - jnp.maximum(nn, jj)
        ov_t = (jnp.maximum(ov, 0).astype(F32) * (1.0 / CMP_BLOCK)).astype(BF16)
        p_hi, p_lo = _split(p_sum)
        imp = _dot(ov_t, p_hi) + _dot(ov_t, p_lo)
        jb = lax.broadcasted_iota(jnp.int32, (n_sel, 1), 0)
        cur = t_row // SEL_BLOCK
        allowed = jb <= cur
        forced = (jb == 0) | (jb == cur) | (jb == cur - 1)
        score = jnp.where(forced & allowed, FORCE, jnp.where(allowed, imp, NEG))
        sc_ref[...] = score
        n_top = min(SEL_TOP, n_sel)

        def selection_bias(n_blk):
            sub = 8
            groups = [sc_ref[r:r + sub, :] for r in range(0, n_blk, sub)]
            ranks = [jnp.zeros((sub, ATT_Q), F32) for _ in groups]
            for k in range(n_blk):
                row = jnp.broadcast_to(sc_ref[k:k + 1, :], (sub, ATT_Q))
                for r, grp in enumerate(groups):
                    if r * sub > k:
                        ahead = row >= grp
                    elif r * sub + sub - 1 < k:
                        ahead = row > grp
                    else:
                        later = (r * sub + lax.broadcasted_iota(jnp.int32, (sub, 1), 0)) > k
                        ahead = (row > grp) | ((row == grp) & later)
                    ranks[r] = ranks[r] + jnp.where(ahead, 1.0, 0.0)
            chosen = (jnp.concatenate(ranks, axis=0) < n_top) & (jnp.concatenate(groups, axis=0) > 0.5 * NEG)
            bias = jnp.where(chosen, 0.0, NEG).astype(BF16)
            rest = bias_ref.shape[0] - n_blk
            return jnp.concatenate([bias, jnp.full((rest, ATT_Q), NEG, BF16)], axis=0) if rest else bias

        n_buckets = -(-n_sel // RANK_BUCKET)
        bucket = jnp.minimum((qb + 1) * (ATT_Q // SEL_BLOCK) - 1, n_sel - 1) // RANK_BUCKET
        for b in range(n_buckets):
            @pl.when(bucket == b)
            def _():
                bias_ref[...] = selection_bias(min((b + 1) * RANK_BUCKET, n_sel))
        qaug_ref[qb % 2] = jnp.concatenate([qt, jnp.concatenate([bias_ref[...]] * GQA_GROUP, axis=1)], axis=0)

        for br in range(2):
            m_ref[2 * slot + br] = jnp.full((1, nq), NEG, F32)
            acc_ref[2 * slot + br] = jnp.zeros((VT_ROWS, nq), F32)

    def epilogue(qb, slot):
        a_s, a_w = acc_ref[2 * slot], acc_ref[2 * slot + 1]
        o_s = a_s[:dh] / a_s[dh:dh + 1]
        o_w = a_w[:dh] / a_w[dh:dh + 1]
        gates = g_ref[0, 0, qb]

        def gate_row(br):
            return jnp.concatenate([gates[br * GQA_GROUP + g:br * GQA_GROUP + g + 1, :] for g in range(GQA_GROUP)],
                                   axis=1)
        o = gate_row(0) * oc_ref[slot] + gate_row(1) * o_s + gate_row(2) * o_w
        rows = pl.ds(pl.multiple_of(qb * ATT_Q, ATT_Q), ATT_Q)
        o_ref[0, rows, :] = jnp.concatenate(
            [o[:, g * ATT_Q:(g + 1) * ATT_Q].T for g in range(GQA_GROUP)], axis=1).astype(BF16)

    def stage_a(t, u):
        tile, pat, qs = jt_ref[t], jp_ref[t], jq_ref[t]
        k = kall_ref[pl.ds(pl.multiple_of(tile * KEY_TILE, KEY_TILE), KEY_TILE), :]
        s = _dot(k, qaug_ref[qs]) + jnp.concatenate([pat_ref[pat]] * GQA_GROUP, axis=1)
        s_ref[u] = s
        mt_ref[u] = jnp.max(s, axis=0, keepdims=True)

    def stage_b(t, u):
        a = ja_ref[t]
        m_old = m_ref[a]
        m_new = jnp.maximum(m_old, mt_ref[u])
        alpha = jnp.exp2(m_old - m_new)
        m_ref[a] = m_new
        al_ref[u] = alpha
        p_ref[u] = jnp.exp2(s_ref[u] - m_new).astype(BF16)

    def stage_c(t, u):
        tile, a = jt_ref[t], ja_ref[t]
        acc_ref[a] = al_ref[u] * acc_ref[a] + _dot(vt_ref[tile], p_ref[u])

    def step(g, carry):
        t0 = g * u_jobs
        for u in range(u_jobs):
            stage_c(t0 + u, u)
        for u in range(u_jobs):
            stage_b(t0 + u_jobs + u, u)
        for u in range(u_jobs):
            stage_a(t0 + 2 * u_jobs + u, u)
        return carry

    prologue(0)

    def block(i, carry):
        prologue(jnp.minimum(i + 1, n_qb - 1))
        epilogue(jnp.maximum(i - EPILOGUE_LAG, 0), (i - EPILOGUE_LAG) % ACC_RING)
        lax.fori_loop(lo_ref[i], lo_ref[i + 1], step, 0)
        return carry

    lax.fori_loop(0, n_qb, block, 0)
    for qb in range(max(n_qb - EPILOGUE_LAG, 0), n_qb):
        epilogue(qb, qb % ACC_RING)


def _attention(q_t, gates_t, kcv, vc_t, ks_aug, kw_aug, vs_t, vw_t):
    bsz, hkv, s, kdim = ks_aug.shape
    dh = vs_t.shape[2]
    n_cmp = kcv.shape[2]
    gd = GQA_GROUP * dh
    nq = GQA_GROUP * ATT_Q
    n_qb = s // ATT_Q
    patterns = _mask_patterns()
    cmp_vis = _compressed_visibility(s)
    tables = _job_tables(s)
    per_head = lambda shape: pl.BlockSpec((1, 1) + shape, lambda b, h, *_: (b, h) + (0,) * len(shape))
    n_acc = 2 * ACC_RING + 1
    return pl.pallas_call(
        _attn_kernel,
        grid_spec=pltpu.PrefetchScalarGridSpec(
            num_scalar_prefetch=len(tables),
            grid=(bsz, hkv),
            in_specs=[
                per_head((n_qb, gd, ATT_Q)),
                per_head((n_qb, 3 * GQA_GROUP, ATT_Q)),
                per_head((n_cmp, kcv.shape[3])),
                per_head((dh, n_cmp)),
                per_head((s, kdim)),
                per_head((s, kdim)),
                per_head((dh, s)),
                per_head((dh, s)),
                pl.BlockSpec(patterns.shape, lambda b, h, *_: (0, 0, 0)),
                pl.BlockSpec(cmp_vis.shape, lambda b, h, *_: (0, 0)),
            ],
            out_specs=pl.BlockSpec((1, s, gd), lambda b, h, *_: (b, 0, h)),
            scratch_shapes=[
                pltpu.VMEM((2 * s, kdim), BF16),
                pltpu.VMEM((2 * (s // KEY_TILE), VT_ROWS, KEY_TILE), BF16),
                pltpu.VMEM((s // SEL_BLOCK, ATT_Q), F32),
                pltpu.VMEM((kdim - dh, ATT_Q), BF16),
                pltpu.VMEM((2, kdim, nq), BF16),
                pltpu.VMEM((ATT_UNROLL, KEY_TILE, nq), F32),
                pltpu.VMEM((ATT_UNROLL, KEY_TILE, nq), BF16),
                pltpu.VMEM((ATT_UNROLL, 1, nq), F32),
                pltpu.VMEM((ATT_UNROLL, 1, nq), F32),
                pltpu.VMEM((n_acc, 1, nq), F32),
                pltpu.VMEM((n_acc, VT_ROWS, nq), F32),
                pltpu.VMEM((ACC_RING, dh, nq), F32),
            ],
        ),
        out_shape=jax.ShapeDtypeStruct((bsz, s, hkv * gd), BF16),
        compiler_params=_params("parallel", "arbitrary"),
        name="nsa_attention",
    )(*tables, q_t, gates_t, kcv, vc_t, ks_aug, kw_aug, vs_t, vw_t, patterns, cmp_vis)


def _oproj_kernel(o_ref, x_ref, mod_ref, w_ref, g_ref, b_ref, out_ref):
    y = _dot(o_ref[0], w_ref[...])
    out_ref[0] = _layer_norm(ALPHA * x_ref[0] + mod_ref[0, 2:3, :] * y, g_ref[...], b_ref[...])


def _out_projection(o, x, mod, w_o, ln_g, ln_b):
    bsz, s, d = x.shape
    tm = 512
    return pl.pallas_call(
        _oproj_kernel,
        grid=(bsz, s // tm),
        in_specs=[
            pl.BlockSpec((1, tm, o.shape[-1]), lambda b, j: (b, j, 0)),
            pl.BlockSpec((1, tm, d), lambda b, j: (b, j, 0)),
            pl.BlockSpec((1, 6, d), lambda b, j: (b, 0, 0)),
            _const_spec(w_o.shape),
            _const_spec((1, d)),
            _const_spec((1, d)),
        ],
        out_specs=pl.BlockSpec((1, tm, d), lambda b, j: (b, j, 0)),
        out_shape=jax.ShapeDtypeStruct((bsz, s, d), F32),
        compiler_params=_params("parallel", "parallel"),
        name="nsa_out_proj",
    )(o, x, mod, w_o, ln_g.reshape(1, d), ln_b.reshape(1, d))


def _moe_block(x, mod, router_wt, router_bias, w_gu, w_down, layer, ln_g, ln_b):
    bsz, s, d = x.shape
    t = bsz * s
    xf = x.reshape(t, d)
    wcol, route, counts = _router(xf, mod, router_wt, router_bias, s)

    cnt = counts[:, 0]
    padded = (cnt + EXPERT_TILE - 1) // EXPERT_TILE * EXPERT_TILE
    ends = jnp.cumsum(padded)
    offs = ends - padded
    n_tiles = (TOP_K * t) // EXPERT_TILE + N_EXPERTS
    tile_start = jnp.arange(n_tiles, dtype=jnp.int32) * EXPERT_TILE
    tile_expert = jnp.minimum(jnp.sum(tile_start[:, None] >= ends[None, :], axis=1), N_EXPERTS - 1).astype(jnp.int32)
    n_used = (ends[-1:] // EXPERT_TILE).astype(jnp.int32)
    experts, ranks = route[:TOP_K], route[TOP_K:2 * TOP_K]
    dst = ranks + sum(jnp.where(experts == e, offs[e], 0) for e in range(N_EXPERTS))
    idx = dst.reshape(TOP_K, t // TOKEN_TILE, TOKEN_TILE).transpose(1, 0, 2).reshape(t // TOKEN_TILE, 1, TOP_K * TOKEN_TILE)

    last_tile = jnp.where(cnt > 0, ends - EXPERT_TILE, -1)
    spare = n_used + jnp.arange(N_EXPERTS)
    spare = jnp.where(spare < n_tiles, spare * EXPERT_TILE, -1)
    xs = _dispatch(jnp.concatenate([last_tile, spare]).astype(jnp.int32), idx, xf, mod, n_tiles * EXPERT_TILE, s)
    ys = _experts(tile_expert, n_used, xs, w_gu, w_down, layer)
    out = _combine(idx, xf, mod, wcol, ys, ln_g, ln_b, s)
    return out.reshape(bsz, s, d)


def _nsa_layer(x, mod, w_kv, cmp_pe, cmp_w1, cmp_b1, cmp_w2, cmp_b2, w_qg, b_g, w_o, ln_g, ln_b):
    bsz, s, d = x.shape
    hkv, dh, grp = N_KV_HEADS, HEAD_DIM, GQA_GROUP
    hw = hkv * dh
    kvw = w_kv.reshape(d, 6, hw)
    kvc_w = jnp.stack([kvw[:, 0].reshape(d, hkv, dh), kvw[:, 1].reshape(d, hkv, dh)], axis=2).reshape(d, 2 * hw)
    w_nat = jnp.concatenate([kvc_w, kvw[:, 2], kvw[:, 4]], axis=1).astype(BF16)
    w_vt = jnp.concatenate([kvw[:, 3], kvw[:, 5]], axis=1).T.astype(BF16)
    nq = N_HEADS * dh
    wg = w_qg[:, nq:].reshape(d, hkv, grp, 3).transpose(0, 1, 3, 2).reshape(d, 3 * N_HEADS)
    w_qgt = jnp.concatenate([w_qg[:, :nq], wg], axis=1).T.astype(BF16)
    bg = b_g.reshape(hkv, grp, 3).transpose(0, 2, 1).reshape(3 * N_HEADS, 1)
    kvc, ks_aug, kw_aug, vs_t, vw_t, q_t, gates_t = _projections(x, mod, w_nat, w_vt, w_qgt, bg)
    cmp_kv, cmp_vt = _compress(kvc, cmp_pe, cmp_w1, cmp_b1, cmp_w2, cmp_b2)
    o = _attention(q_t, gates_t, cmp_kv, cmp_vt, ks_aug, kw_aug, vs_t, vw_t)
    return _out_projection(o, x, mod, w_o.astype(BF16), ln_g, ln_b)


def kernel(x, c, ada_w, ada_b, ln_g, ln_b, conv_w_in, conv_w, conv_b, conv_w_out, w_kv, cmp_pe, cmp_w1, cmp_b1, cmp_w2, cmp_b2, w_qg, b_g, w_o, router_w, router_bias, w_gu, w_down):
    bsz, s, d = x.shape
    mod = _modulation(c, ada_w, ada_b).reshape(DEPTH, bsz, 6, d)
    router_wt = router_w.T

    x = _conv_layer(x, mod[0], conv_w_in[0].astype(BF16), conv_w[0], conv_b[0], conv_w_out[0].astype(BF16),
                    ln_g[0, 0], ln_b[0, 0])
    x = _moe_block(x, mod[0], router_wt, router_bias, w_gu, w_down, 0, ln_g[0, 1], ln_b[0, 1])

    x = _nsa_layer(x, mod[1], w_kv, cmp_pe, cmp_w1, cmp_b1, cmp_w2, cmp_b2, w_qg[0], b_g[0], w_o[0],
                   ln_g[1, 0], ln_b[1, 0])
    x = _moe_block(x, mod[1], router_wt, router_bias, w_gu, w_down, 1, ln_g[1, 1], ln_b[1, 1])
    return x
```

```python
import functools
import math

import jax
import jax.numpy as jnp
import numpy as np
from jax import lax
from jax.experimental import pallas as pl
from jax.experimental.pallas import tpu as pltpu

F32 = jnp.float32
BF16 = jnp.bfloat16

D_MODEL = 1024
DEPTH = 2
N_A_LAYERS = DEPTH // 2
CONV_WIDTH = 3
N_HEADS = 16
HEAD_DIM = D_MODEL // N_HEADS
N_KV_HEADS = 4
GQA_GROUP = N_HEADS // N_KV_HEADS
CMP_BLOCK = 32
CMP_STRIDE = 16
CMP_HIDDEN = 4 * HEAD_DIM
SEL_BLOCK = 64
SEL_TOP = 16
WINDOW = 512
N_EXPERTS = 16
N_GROUPS = 4
EXPERTS_PER_GROUP = N_EXPERTS // N_GROUPS
TOP_K = 2
D_FF_EXPERT = D_MODEL // 2
ALPHA = (2 * DEPTH) ** 0.25
LN_EPS = 1e-5
NEG = -1e30
FORCE = 1e9

LANES = 128
VMEM_LIMIT_BYTES = 56 * 1024 * 1024

LOG2E = math.log2(math.e)

ATT_Q = 256
KEY_TILE = 256
PAT_ZERO, PAT_DIAG, PAT_ANTI, PAT_NONE = range(4)
ATT_UNROLL = 4
ACC_RING = 8
EPILOGUE_LAG = 3
RANK_BUCKET = 16
VT_ROWS = HEAD_DIM + 16


def _dot(a, b):
    return jnp.dot(a, b, preferred_element_type=F32)


def _dot_nt(a, b):
    return lax.dot_general(a, b, (((1,), (1,)), ((), ())), preferred_element_type=F32)


def _split(x):
    hi = x.astype(BF16)
    lo = (x - hi.astype(F32)).astype(BF16)
    return hi, lo


def _dot3(a, b):
    ah, al = _split(a)
    bh, bl = _split(b)
    return _dot(ah, bh) + (_dot(ah, bl) + _dot(al, bh))


def _dot3_nt(a, b):
    ah, al = _split(a)
    bh, bl = _split(b)
    return _dot_nt(ah, bh) + (_dot_nt(ah, bl) + _dot_nt(al, bh))


def _layer_norm(r, g, b):
    mu = jnp.mean(r, axis=-1, keepdims=True)
    d = r - mu
    var = jnp.mean(d * d, axis=-1, keepdims=True)
    return d * lax.rsqrt(var + LN_EPS) * g + b


def _params(*sem):
    return pltpu.CompilerParams(dimension_semantics=sem, vmem_limit_bytes=VMEM_LIMIT_BYTES)


def _const_spec(shape):
    zeros = (0,) * len(shape)
    return pl.BlockSpec(shape, lambda *_: zeros)


def _mod_kernel(c_ref, w_ref, b_ref, o_ref):
    c = c_ref[...]
    s = c * jax.nn.sigmoid(c)
    o_ref[0] = _dot3(s, w_ref[0]) + b_ref[0]


def _modulation(c, ada_w, ada_b):
    depth, d, n = ada_w.shape
    bsz = c.shape[0]
    tn = 1536
    return pl.pallas_call(
        _mod_kernel,
        grid=(depth, n // tn),
        in_specs=[
            pl.BlockSpec((bsz, d), lambda l, j: (0, 0)),
            pl.BlockSpec((1, d, tn), lambda l, j: (l, 0, j)),
            pl.BlockSpec((1, 1, tn), lambda l, j: (l, 0, j)),
        ],
        out_specs=pl.BlockSpec((1, bsz, tn), lambda l, j: (l, 0, j)),
        out_shape=jax.ShapeDtypeStruct((depth, bsz, n), F32),
        compiler_params=_params("parallel", "parallel"),
        name="adaln_mod",
    )(c, ada_w, ada_b.reshape(depth, 1, n))


CONV_HALO = 8


def _conv_kernel(x_ref, mod_ref, win_ref, cw_ref, cb_ref, wout_ref, g_ref, b_ref, o_ref, z_ref):
    tm = x_ref.shape[1]
    d = x_ref.shape[2]

    @pl.when(pl.program_id(1) == 0)
    def _():
        z_ref[0:CONV_HALO, :] = jnp.zeros((CONV_HALO, d), F32)

    x = x_ref[0]
    sh = mod_ref[0, 0:1, :]
    sc = mod_ref[0, 1:2, :]
    gate = mod_ref[0, 2:3, :]
    u = (x * (1.0 + sc) + sh).astype(BF16)
    bch = _dot(u, win_ref[...])
    z = bch[:, d:2 * d] * bch[:, 2 * d:]
    z_ref[CONV_HALO:CONV_HALO + tm, :] = z
    z1 = z_ref[CONV_HALO - 1:CONV_HALO - 1 + tm, :]
    z2 = z_ref[CONV_HALO - 2:CONV_HALO - 2 + tm, :]
    conv = cw_ref[0:1, :] * z2 + cw_ref[1:2, :] * z1 + cw_ref[2:3, :] * z + cb_ref[...]
    v = (bch[:, :d] * conv).astype(BF16)
    y = _dot(v, wout_ref[...])
    o_ref[0] = _layer_norm(ALPHA * x + gate * y, g_ref[...], b_ref[...])
    z_ref[0:CONV_HALO, :] = z_ref[tm:tm + CONV_HALO, :]


def _conv_layer(x, mod, w_in, conv_w, conv_b, w_out, ln_g, ln_b):
    bsz, s, d = x.shape
    tm = 512
    return pl.pallas_call(
        _conv_kernel,
        grid=(bsz, s // tm),
        in_specs=[
            pl.BlockSpec((1, tm, d), lambda b, j: (b, j, 0)),
            pl.BlockSpec((1, 6, d), lambda b, j: (b, 0, 0)),
            _const_spec((d, 3 * d)),
            _const_spec((CONV_WIDTH, d)),
            _const_spec((1, d)),
            _const_spec((d, d)),
            _const_spec((1, d)),
            _const_spec((1, d)),
        ],
        out_specs=pl.BlockSpec((1, tm, d), lambda b, j: (b, j, 0)),
        out_shape=jax.ShapeDtypeStruct((bsz, s, d), F32),
        scratch_shapes=[pltpu.VMEM((tm + CONV_HALO, d), F32)],
        compiler_params=_params("arbitrary", "arbitrary"),
        name="conv_mixer",
    )(x, mod, w_in, conv_w, conv_b.reshape(1, d), w_out, ln_g.reshape(1, d), ln_b.reshape(1, d))


def _router_kernel(x_ref, mod_ref, rwt_ref, rb_ref, wcol_ref, route_ref, count_ref, base_ref, tri_ref):
    tm = x_ref.shape[0]
    x = x_ref[...]
    u = x * (1.0 + mod_ref[0, 4:5, :]) + mod_ref[0, 3:4, :]
    logits = _dot3_nt(rwt_ref[...], u)
    m = jnp.max(logits, axis=0, keepdims=True)
    e = jnp.exp(logits - m)
    aff = e / jnp.sum(e, axis=0, keepdims=True)
    biased = aff + rb_ref[...]
    aff_r = [aff[i:i + 1, :] for i in range(N_EXPERTS)]
    row = [biased[i:i + 1, :] for i in range(N_EXPERTS)]

    best_s, best = None, None
    for g in range(N_GROUPS):
        r = row[g * EXPERTS_PER_GROUP:(g + 1) * EXPERTS_PER_GROUP]
        gs = None
        for i in range(EXPERTS_PER_GROUP):
            for j in range(i + 1, EXPERTS_PER_GROUP):
                p = r[i] + r[j]
                gs = p if gs is None else jnp.maximum(gs, p)
        if g == 0:
            best_s, best = gs, jnp.zeros((1, tm), jnp.int32)
        else:
            upd = gs > best_s
            best = jnp.where(upd, g, best)
            best_s = jnp.where(upd, gs, best_s)

    masked = [jnp.where(best == (i // EXPERTS_PER_GROUP), row[i], NEG) for i in range(N_EXPERTS)]

    def first_argmax(vals):
        v, idx = vals[0], jnp.zeros((1, tm), jnp.int32)
        for i in range(1, N_EXPERTS):
            upd = vals[i] > v
            idx = jnp.where(upd, i, idx)
            v = jnp.where(upd, vals[i], v)
        return idx

    i0 = first_argmax(masked)
    i1 = first_argmax([jnp.where(i0 == i, -jnp.inf, masked[i]) for i in range(N_EXPERTS)])
    w0 = sum(jnp.where(i0 == i, aff_r[i], 0.0) for i in range(N_EXPERTS))
    w1 = sum(jnp.where(i1 == i, aff_r[i], 0.0) for i in range(N_EXPERTS))
    tot = w0 + w1
    w0 = w0 / tot
    w1 = w1 / tot
    w_t = jnp.concatenate([w0, w1, jnp.zeros((LANES - TOP_K, tm), F32)], axis=0)
    wcol_ref[...] = w_t.T

    @pl.when(pl.program_id(0) == 0)
    def _():
        base_ref[...] = jnp.zeros(base_ref.shape, F32)
        r_i = lax.broadcasted_iota(jnp.int32, tri_ref.shape, 0)
        c_i = lax.broadcasted_iota(jnp.int32, tri_ref.shape, 1)
        tri_ref[...] = jnp.where(r_i <= c_i, 1.0, 0.0).astype(BF16)

    hot = jnp.concatenate([jnp.where((i0 == i) | (i1 == i), 1.0, 0.0) for i in range(N_EXPERTS)], axis=0)
    incl = _dot(hot.astype(BF16), tri_ref[...])
    pos = base_ref[...] + (incl - hot)
    rank0 = sum(jnp.where(i0 == i, pos[i:i + 1, :], 0.0) for i in range(N_EXPERTS))
    rank1 = sum(jnp.where(i1 == i, pos[i:i + 1, :], 0.0) for i in range(N_EXPERTS))
    base = base_ref[...] + jnp.sum(hot, axis=1, keepdims=True)
    base_ref[...] = base
    route_ref[...] = jnp.concatenate(
        [i0, i1, rank0.astype(jnp.int32), rank1.astype(jnp.int32), jnp.zeros((4, tm), jnp.int32)], axis=0)
    count_ref[...] = jnp.broadcast_to(base, count_ref.shape).astype(jnp.int32)


def _router(x, mod, router_wt, router_bias, seq):
    t, d = x.shape
    tm = 1024
    return pl.pallas_call(
        _router_kernel,
        grid=(t // tm,),
        in_specs=[
            pl.BlockSpec((tm, d), lambda i: (i, 0)),
            pl.BlockSpec((1, 6, d), lambda i: ((i * tm) // seq, 0, 0)),
            _const_spec((N_EXPERTS, d)),
            _const_spec((N_EXPERTS, 1)),
        ],
        out_specs=(
            pl.BlockSpec((tm, LANES), lambda i: (i, 0)),
            pl.BlockSpec((8, tm), lambda i: (0, i)),
            _const_spec((N_EXPERTS, LANES)),
        ),
        out_shape=(
            jax.ShapeDtypeStruct((t, LANES), F32),
            jax.ShapeDtypeStruct((8, t), jnp.int32),
            jax.ShapeDtypeStruct((N_EXPERTS, LANES), jnp.int32),
        ),
        scratch_shapes=[pltpu.VMEM((N_EXPERTS, 1), F32), pltpu.VMEM((tm, tm), BF16)],
        compiler_params=_params("arbitrary"),
        name="moe_router",
    )(x, mod, router_wt, router_bias.reshape(N_EXPERTS, 1))


EXPERT_TILE = 512
TOKEN_TILE = 512
ROW_COPY_UNROLL = 8
ROW_TILES = D_MODEL // LANES


def _store_rows(ref, x):
    rows = x.shape[0]
    for c in range(ROW_TILES):
        ref[pl.ds(c, rows, stride=ROW_TILES), :] = x[:, c * LANES:(c + 1) * LANES]


def _load_rows(ref, rows):
    return jnp.concatenate([ref[pl.ds(c, rows, stride=ROW_TILES), :] for c in range(ROW_TILES)], axis=1)


def _row(ref, r):
    return ref.at[pl.ds(pl.multiple_of(r * ROW_TILES, ROW_TILES), ROW_TILES)]


def _start_row_copies(tm, row_copy):
    def start(t, c):
        for k in range(TOP_K):
            row_copy(t, k).start(priority=k % 2)
        return c

    lax.fori_loop(0, tm, start, 0, unroll=ROW_COPY_UNROLL)


def _dispatch_kernel(pad_ref, idx_ref, x_ref, mod_ref, xs_ref, u_ref, sem):
    tm = x_ref.shape[0]

    i = pl.program_id(0)
    slot = i % 2

    @pl.when(i == 0)
    def _():
        u_ref[1] = jnp.zeros(u_ref.shape[1:], F32)
        for wait in (False, True):
            for e in range(pad_ref.shape[0]):
                @pl.when(pad_ref[e] >= 0)
                def _():
                    row = pl.multiple_of(jnp.maximum(pad_ref[e], 0) * ROW_TILES, tm * ROW_TILES)
                    fill = pltpu.make_async_copy(u_ref.at[1], xs_ref.at[pl.ds(row, tm * ROW_TILES)], sem.at[1])
                    fill.wait() if wait else fill.start()

    _store_rows(u_ref.at[slot], x_ref[...] * (1.0 + mod_ref[0, 4:5, :]) + mod_ref[0, 3:4, :])

    def row_copy(t, k):
        dst = idx_ref[0, 0, k * tm + t]
        return pltpu.make_async_copy(_row(u_ref.at[slot], t), _row(xs_ref, dst), sem.at[slot])

    _start_row_copies(tm, row_copy)

    def wait_tile(s):
        for _ in range(TOP_K):
            pltpu.make_async_copy(u_ref.at[s], xs_ref.at[pl.ds(0, tm * ROW_TILES)], sem.at[s]).wait()

    @pl.when(i > 0)
    def _():
        wait_tile(1 - slot)

    @pl.when(i == pl.num_programs(0) - 1)
    def _():
        wait_tile(slot)


def _dispatch(pad_start, idx, x, mod, n_rows, seq):
    t, d = x.shape
    tm = TOKEN_TILE
    assert tm == EXPERT_TILE
    return pl.pallas_call(
        _dispatch_kernel,
        grid_spec=pltpu.PrefetchScalarGridSpec(
            num_scalar_prefetch=1,
            grid=(t // tm,),
            in_specs=[
                pl.BlockSpec((1, 1, TOP_K * tm), lambda i, pad: (i, 0, 0), memory_space=pltpu.SMEM),
                pl.BlockSpec((tm, d), lambda i, pad: (i, 0)),
                pl.BlockSpec((1, 6, d), lambda i, pad: ((i * tm) // seq, 0, 0)),
            ],
            out_specs=pl.BlockSpec(memory_space=pl.ANY),
            scratch_shapes=[pltpu.VMEM((2, tm * ROW_TILES, LANES), F32), pltpu.SemaphoreType.DMA((2,))],
        ),
        out_shape=jax.ShapeDtypeStruct((n_rows * ROW_TILES, LANES), F32),
        compiler_params=_params("arbitrary"),
        name="moe_dispatch",
    )(pad_start, idx, x, mod)


def _expert_kernel(te_ref, first_ref, par_ref, nu_ref, xs_ref, wgu0_ref, wgu_next_ref, wdn_ref, ys_ref, wgu_bf, wdn_bf):
    del te_ref
    s = pl.program_id(0)
    used = s < nu_ref[0]
    nxt = jnp.minimum(s + 1, pl.num_programs(0) - 1)

    @pl.when(s == 0)
    def _():
        wgu_bf[par_ref[0]] = wgu0_ref[0, 0].astype(BF16)

    @pl.when((s + 1 < nu_ref[0]) & (first_ref[nxt] == 1))
    def _():
        wgu_bf[par_ref[nxt]] = wgu_next_ref[0, 0].astype(BF16)

    @pl.when(used & (first_ref[s] == 1))
    def _():
        wdn_bf[...] = wdn_ref[0, 0].astype(BF16)

    @pl.when(used)
    def _():
        f = wdn_bf.shape[0]
        h = _dot(_load_rows(xs_ref, EXPERT_TILE).astype(BF16), wgu_bf[par_ref[s]])
        gate = h[:, :f]
        a = (gate * jax.nn.sigmoid(gate) * h[:, f:]).astype(BF16)
        _store_rows(ys_ref, _dot(a, wdn_bf[...]))

    @pl.when(jnp.logical_not(used))
    def _():
        ys_ref[...] = jnp.zeros(ys_ref.shape, F32)


def _experts(tile_expert, n_used, xs, w_gu, w_down, layer):
    _, _, d, f2 = w_gu.shape
    f = f2 // 2
    n_tiles = tile_expert.shape[0]
    first = jnp.concatenate([jnp.ones((1,), jnp.int32), (tile_expert[1:] != tile_expert[:-1]).astype(jnp.int32)])
    parity = ((jnp.cumsum(first) - 1) % 2).astype(jnp.int32)
    row_map = lambda s, te, fi, pa, nu: (s, 0)
    in_row_map = lambda s, te, fi, pa, nu: (jnp.minimum(s, nu[0]), 0)
    return pl.pallas_call(
        _expert_kernel,
        grid_spec=pltpu.PrefetchScalarGridSpec(
            num_scalar_prefetch=4,
            grid=(n_tiles,),
            in_specs=[
                pl.BlockSpec((EXPERT_TILE * ROW_TILES, LANES), in_row_map),
                pl.BlockSpec((1, 1, d, f2), lambda s, te, fi, pa, nu: (layer, te[0], 0, 0)),
                pl.BlockSpec((1, 1, d, f2), lambda s, te, fi, pa, nu: (layer, te[jnp.minimum(s + 1, n_tiles - 1)], 0, 0)),
                pl.BlockSpec((1, 1, f, d), lambda s, te, fi, pa, nu: (layer, te[s], 0, 0)),
            ],
            out_specs=pl.BlockSpec((EXPERT_TILE * ROW_TILES, LANES), row_map),
            scratch_shapes=[pltpu.VMEM((2, d, f2), BF16), pltpu.VMEM((f, d), BF16)],
        ),
        out_shape=jax.ShapeDtypeStruct(xs.shape, F32),
        compiler_params=_params("arbitrary"),
        name="moe_experts",
    )(tile_expert, first, parity, n_used, xs, w_gu, w_gu, w_down)


def _combine_kernel(idx_ref, next_idx_ref, x_ref, mod_ref, w_ref, ys_ref, g_ref, b_ref, o_ref, y_ref, sem):
    tm = x_ref.shape[0]
    i = pl.program_id(0)
    slot = i % 2

    def gather(ids_ref, into):
        def row_copy(t, k):
            src = ids_ref[0, 0, k * tm + t]
            return pltpu.make_async_copy(_row(ys_ref, src), _row(y_ref.at[into, k], t), sem.at[into])

        _start_row_copies(tm, row_copy)

    @pl.when(i == 0)
    def _():
        gather(idx_ref, 0)

    @pl.when(i + 1 < pl.num_programs(0))
    def _():
        gather(next_idx_ref, 1 - slot)

    for k in range(TOP_K):
        pltpu.make_async_copy(ys_ref.at[pl.ds(0, tm * ROW_TILES)], y_ref.at[slot, k], sem.at[slot]).wait()
    w = w_ref[...]
    out = w[:, 0:1] * _load_rows(y_ref.at[slot, 0], tm) + w[:, 1:2] * _load_rows(y_ref.at[slot, 1], tm)
    r = ALPHA * x_ref[...] + mod_ref[0, 5:6, :] * out
    o_ref[...] = _layer_norm(r, g_ref[...], b_ref[...])


def _combine(idx, x, mod, wcol, ys, ln_g, ln_b, seq):
    t, d = x.shape
    tm = TOKEN_TILE
    n_tiles = t // tm
    return pl.pallas_call(
        _combine_kernel,
        grid=(n_tiles,),
        in_specs=[
            pl.BlockSpec((1, 1, TOP_K * tm), lambda i: (i, 0, 0), memory_space=pltpu.SMEM),
            pl.BlockSpec((1, 1, TOP_K * tm), lambda i: (jnp.minimum(i + 1, n_tiles - 1), 0, 0),
                         memory_space=pltpu.SMEM),
            pl.BlockSpec((tm, d), lambda i: (i, 0)),
            pl.BlockSpec((1, 6, d), lambda i: ((i * tm) // seq, 0, 0)),
            pl.BlockSpec((tm, LANES), lambda i: (i, 0)),
            pl.BlockSpec(memory_space=pl.ANY),
            _const_spec((1, d)),
            _const_spec((1, d)),
        ],
        out_specs=pl.BlockSpec((tm, d), lambda i: (i, 0)),
        out_shape=jax.ShapeDtypeStruct((t, d), F32),
        scratch_shapes=[pltpu.VMEM((2, TOP_K, tm * ROW_TILES, LANES), F32), pltpu.SemaphoreType.DMA((2,))],
        compiler_params=_params("arbitrary"),
        name="moe_combine",
    )(idx, idx, x, mod, wcol, ys, ln_g.reshape(1, d), ln_b.reshape(1, d))


def _proj_kernel(x_ref, mod_ref, wn_ref, wvt_ref, wqt_ref, bg_ref,
                 kvc_ref, ks_ref, kw_ref, vst_ref, vwt_ref, qt_ref, gt_ref):
    tm = x_ref.shape[1]
    j = pl.program_id(1)
    x = x_ref[0]
    xb = x.astype(BF16)
    nat = _dot(xb, wn_ref[...])
    vt = _dot_nt(wvt_ref[...], xb)
    u = (x * (1.0 + mod_ref[0, 1:2, :]) + mod_ref[0, 0:1, :]).astype(BF16)
    qg = _dot_nt(wqt_ref[...], u)
    nq = N_HEADS * HEAD_DIM
    q = (qg[:nq, :] * (HEAD_DIM ** -0.5 * LOG2E)).astype(BF16)
    gates = jax.nn.sigmoid(qg[nq:, :] + bg_ref[...])
    per = 3 * GQA_GROUP
    blk = (j * tm + lax.broadcasted_iota(jnp.int32, (tm, SEL_BLOCK), 0)) // SEL_BLOCK
    onehot = (blk == lax.broadcasted_iota(jnp.int32, (tm, SEL_BLOCK), 1)).astype(BF16)
    zeros = jnp.zeros((tm, SEL_BLOCK), BF16)
    hw = N_KV_HEADS * HEAD_DIM
    for h in range(N_KV_HEADS):
        c0 = h * HEAD_DIM
        kvc_ref[0, h] = nat[:, 2 * c0:2 * c0 + 2 * HEAD_DIM]
        ks = nat[:, 2 * hw + c0:2 * hw + c0 + HEAD_DIM].astype(BF16)
        ks_ref[0, h] = jnp.concatenate([ks, onehot], axis=1)
        kw = nat[:, 3 * hw + c0:3 * hw + c0 + HEAD_DIM].astype(BF16)
        kw_ref[0, h] = jnp.concatenate([kw, zeros], axis=1)
        vst_ref[0, h] = vt[c0:c0 + HEAD_DIM, :].astype(BF16)
        vwt_ref[0, h] = vt[hw + c0:hw + c0 + HEAD_DIM, :].astype(BF16)
        for p in range(tm // ATT_Q):
            cols = slice(p * ATT_Q, (p + 1) * ATT_Q)
            qt_ref[0, h, p] = q[h * GQA_GROUP * HEAD_DIM:(h + 1) * GQA_GROUP * HEAD_DIM, cols]
            gt_ref[0, h, p] = gates[h * per:(h + 1) * per, cols]


def _projections(x, mod, w_nat, w_vt, w_qgt, b_g):
    bsz, s, d = x.shape
    tm = 512
    hkv, dh = N_KV_HEADS, HEAD_DIM
    per = 3 * GQA_GROUP
    nqg = w_qgt.shape[0]
    out_shape = (
        jax.ShapeDtypeStruct((bsz, hkv, s, 2 * dh), F32),
        jax.ShapeDtypeStruct((bsz, hkv, s, 2 * dh), BF16),
        jax.ShapeDtypeStruct((bsz, hkv, s, 2 * dh), BF16),
        jax.ShapeDtypeStruct((bsz, hkv, dh, s), BF16),
        jax.ShapeDtypeStruct((bsz, hkv, dh, s), BF16),
        jax.ShapeDtypeStruct((bsz, hkv, s // ATT_Q, GQA_GROUP * dh, ATT_Q), BF16),
        jax.ShapeDtypeStruct((bsz, hkv, s // ATT_Q, per, ATT_Q), F32),
    )
    out_specs = (
        pl.BlockSpec((1, hkv, tm, 2 * dh), lambda b, j: (b, 0, j, 0)),
        pl.BlockSpec((1, hkv, tm, 2 * dh), lambda b, j: (b, 0, j, 0)),
        pl.BlockSpec((1, hkv, tm, 2 * dh), lambda b, j: (b, 0, j, 0)),
        pl.BlockSpec((1, hkv, dh, tm), lambda b, j: (b, 0, 0, j)),
        pl.BlockSpec((1, hkv, dh, tm), lambda b, j: (b, 0, 0, j)),
        pl.BlockSpec((1, hkv, tm // ATT_Q, GQA_GROUP * dh, ATT_Q), lambda b, j: (b, 0, j, 0, 0)),
        pl.BlockSpec((1, hkv, tm // ATT_Q, per, ATT_Q), lambda b, j: (b, 0, j, 0, 0)),
    )
    return pl.pallas_call(
        _proj_kernel,
        grid=(bsz, s // tm),
        in_specs=[
            pl.BlockSpec((1, tm, d), lambda b, j: (b, j, 0)),
            pl.BlockSpec((1, 6, d), lambda b, j: (b, 0, 0)),
            _const_spec(w_nat.shape),
            _const_spec(w_vt.shape),
            _const_spec(w_qgt.shape),
            _const_spec((nqg - N_HEADS * dh, 1)),
        ],
        out_specs=out_specs,
        out_shape=out_shape,
        compiler_params=_params("parallel", "parallel"),
        name="nsa_projections",
    )(x, mod, w_nat, w_vt, w_qgt, b_g)


def _compress_kernel(kvc_ref, pe_ref, w1_ref, b1_ref, w2_ref, b2_ref, nat_ref, tr_ref):
    n = kvc_ref.shape[2] // CMP_STRIDE
    hid2 = w1_ref.shape[-1]
    p = jnp.zeros((n, hid2), F32)
    q = jnp.zeros((n, hid2), F32)
    for l in range(CMP_STRIDE):
        x = kvc_ref[0, 0, pl.ds(l, n, stride=CMP_STRIDE), :]
        p = p + _dot((x + pe_ref[l:l + 1, :]).astype(BF16), w1_ref[0, l])
        q = q + _dot((x + pe_ref[CMP_STRIDE + l:CMP_STRIDE + l + 1, :]).astype(BF16), w1_ref[1, l])
    pre = p + pltpu.roll(q, n - 1, 0) + b1_ref[...]
    hdn = 0.5 * pre * (1.0 + jnp.tanh(0.7978845608028654 * (pre + 0.044715 * (pre * pre * pre))))
    out = _dot(hdn.astype(BF16), w2_ref[...]) + b2_ref[...]
    nat_ref[0, 0] = out.astype(BF16)
    tr_ref[0, 0] = out[:, HEAD_DIM:].T.astype(BF16)


def _compress(kvc, cmp_pe, cmp_w1, cmp_b1, cmp_w2, cmp_b2):
    bsz, hkv, s, two_dh = kvc.shape
    dh = two_dh // 2
    n = s // CMP_STRIDE
    hid = cmp_w1.shape[-1]
    zw = jnp.zeros((2, CMP_STRIDE, dh, hid), F32)
    w1 = cmp_w1.reshape(2, 2, CMP_STRIDE, dh, hid)
    w1 = jnp.concatenate([jnp.concatenate([w1[0], zw], axis=-1), jnp.concatenate([zw, w1[1]], axis=-1)], axis=-2)
    zd = jnp.zeros((hid, dh), F32)
    w2 = jnp.concatenate([jnp.concatenate([cmp_w2[0], zd], axis=1), jnp.concatenate([zd, cmp_w2[1]], axis=1)], axis=0)
    pe = jnp.concatenate([cmp_pe[0], cmp_pe[1]], axis=1)
    return pl.pallas_call(
        _compress_kernel,
        grid=(bsz, hkv),
        in_specs=[
            pl.BlockSpec((1, 1, s, two_dh), lambda b, h: (b, h, 0, 0)),
            _const_spec(pe.shape),
            _const_spec(w1.shape),
            _const_spec((1, 2 * hid)),
            _const_spec(w2.shape),
            _const_spec((1, two_dh)),
        ],
        out_specs=(
            pl.BlockSpec((1, 1, n, two_dh), lambda b, h: (b, h, 0, 0)),
            pl.BlockSpec((1, 1, dh, n), lambda b, h: (b, h, 0, 0)),
        ),
        out_shape=(
            jax.ShapeDtypeStruct((bsz, hkv, n, two_dh), BF16),
            jax.ShapeDtypeStruct((bsz, hkv, dh, n), BF16),
        ),
        compiler_params=_params("parallel", "parallel"),
        name="nsa_compress",
    )(kvc, pe, w1.astype(BF16), cmp_b1.reshape(1, 2 * hid), w2.astype(BF16), cmp_b2.reshape(1, two_dh))


def _mask_patterns():
    assert WINDOW == 2 * KEY_TILE and ATT_Q == KEY_TILE
    keyl = np.arange(KEY_TILE)[:, None]
    ql = np.arange(ATT_Q)[None, :]
    true = np.ones((KEY_TILE, ATT_Q), bool)
    valid = np.stack([
        true,
        keyl <= ql,
        keyl > ql,
        ~true,
    ])
    return jnp.asarray(np.where(valid, 0.0, NEG), F32)


def _compressed_visibility(seq):
    n_cmp = seq // CMP_STRIDE
    r = np.arange(-n_cmp, n_cmp)[:, None]
    ql = np.arange(ATT_Q)[None, :]
    return jnp.asarray(np.where(r * CMP_STRIDE + CMP_BLOCK - 1 <= ql, 0.0, NEG), F32)


def _job_tables(seq):
    nqb, nkt, u = seq // ATT_Q, seq // KEY_TILE, ATT_UNROLL
    tile, pat, acc, qs, base = [], [], [], [], []
    for i in range(nqb):
        base.append(len(tile))
        for j in range(i + 1):
            tile.append(j)
            pat.append(PAT_DIAG if j == i else PAT_ZERO)
            acc.append(2 * (i % ACC_RING))
            qs.append(i % 2)
        for w in range(min(i, WINDOW // KEY_TILE) + 1):
            tile.append(nkt + i - w)
            pat.append((PAT_DIAG, PAT_ZERO, PAT_ANTI)[w])
            acc.append(2 * (i % ACC_RING) + 1)
            qs.append(i % 2)
    n_real = len(tile)
    n_steps = (n_real - 1 + 2 * u) // u + 1
    pad = lambda x, fill: np.array([fill] * (2 * u) + x + [fill] * (n_steps * u - n_real), np.int32)
    step_lo = np.array([-(-b // u) for b in base] + [n_steps], np.int32)
    assert all(base[i + 2] >= step_lo[i + 1] * u for i in range(nqb - 2))
    return pad(tile, 0), pad(pat, PAT_NONE), pad(acc, 2 * ACC_RING), pad(qs, 0), step_lo


def _attn_kernel(jt_ref, jp_ref, ja_ref, jq_ref, lo_ref,
                 q_ref, g_ref, kcv_ref, vct_ref, ks_ref, kw_ref, vs_ref, vw_ref, pat_ref, cvis_ref, o_ref,
                 kall_ref, vt_ref, sc_ref, bias_ref, qaug_ref, s_ref, p_ref, al_ref, mt_ref, m_ref, acc_ref, oc_ref):
    seq = ks_ref.shape[2]
    nq = GQA_GROUP * ATT_Q
    dh = HEAD_DIM
    n_key_tiles = seq // KEY_TILE
    n_qb = seq // ATT_Q
    u_jobs = ATT_UNROLL

    kall_ref[0:seq, :] = ks_ref[0, 0]
    kall_ref[seq:2 * seq, :] = kw_ref[0, 0]
    ones_rows = (lax.broadcasted_iota(jnp.int32, (VT_ROWS - dh, KEY_TILE), 0) == 0).astype(BF16)
    for n in range(n_key_tiles):
        vt_ref[n, 0:dh, :] = vs_ref[0, 0, :, n * KEY_TILE:(n + 1) * KEY_TILE]
        vt_ref[n_key_tiles + n, 0:dh, :] = vw_ref[0, 0, :, n * KEY_TILE:(n + 1) * KEY_TILE]
        vt_ref[n, dh:VT_ROWS, :] = ones_rows
        vt_ref[n_key_tiles + n, dh:VT_ROWS, :] = ones_rows

    s_ref[...] = jnp.full(s_ref.shape, NEG, F32)
    mt_ref[...] = jnp.full(mt_ref.shape, NEG, F32)
    p_ref[...] = jnp.zeros(p_ref.shape, BF16)
    al_ref[...] = jnp.ones(al_ref.shape, F32)
    m_ref[...] = jnp.full(m_ref.shape, NEG, F32)
    acc_ref[...] = jnp.ones(acc_ref.shape, F32)
    oc_ref[...] = jnp.zeros(oc_ref.shape, F32)

    def prologue(qb):
        slot = qb % ACC_RING
        qa = q_ref[0, 0, qb]
        qt = jnp.concatenate([qa[g * dh:(g + 1) * dh, :] for g in range(GQA_GROUP)], axis=1)
        t_row = qb * ATT_Q + lax.broadcasted_iota(jnp.int32, (1, ATT_Q), 1)
        t4 = jnp.concatenate([t_row] * GQA_GROUP, axis=1)

        n_cmp = kcv_ref.shape[2]
        q_c = jnp.concatenate([qt, jnp.zeros((kcv_ref.shape[3] - dh, nq), BF16)], axis=0)
        per_qb = ATT_Q // CMP_STRIDE
        vis = cvis_ref[pl.ds(pl.multiple_of(n_cmp - per_qb * qb, per_qb), n_cmp), :]
        s_c = _dot(kcv_ref[0, 0], q_c) + jnp.concatenate([vis] * GQA_GROUP, axis=1)
        m_c = jnp.max(s_c, axis=0, keepdims=True)
        m_c = jnp.where(t4 >= CMP_BLOCK - 1, m_c, 0.0)
        e_c = jnp.exp2(s_c - m_c)
        p_c = e_c / jnp.maximum(jnp.sum(e_c, axis=0, keepdims=True), 1e-30)
        oc_ref[slot] = _dot(vct_ref[0, 0], p_c.astype(BF16))

        n_sel = seq // SEL_BLOCK
        p_sum = sum(p_c[:, g * ATT_Q:(g + 1) * ATT_Q] for g in range(GQA_GROUP))
        jj = lax.broadcasted_iota(jnp.int32, (n_sel, n_cmp), 0) * SEL_BLOCK
        nn = lax.broadcasted_iota(jnp.int32, (n_sel, n_cmp), 1) * CMP_STRIDE
        ov = jnp.minimum(nn + CMP_BLOCK, jj + SEL_BLOCK) - jnp.maximum(nn, jj)
        ov_t = (jnp.maximum(ov, 0).astype(F32) * (1.0 / CMP_BLOCK)).astype(BF16)
        p_hi, p_lo = _split(p_sum)
        imp = _dot(ov_t, p_hi) + _dot(ov_t, p_lo)
        jb = lax.broadcasted_iota(jnp.int32, (n_sel, 1), 0)
        cur = t_row // SEL_BLOCK
        allowed = jb <= cur
        forced = (jb == 0) | (jb == cur) | (jb == cur - 1)
        score = jnp.where(forced & allowed, FORCE, jnp.where(allowed, imp, NEG))
        sc_ref[...] = score
        n_top = min(SEL_TOP, n_sel)

        def selection_bias(n_blk):
            sub = 8
            groups = [sc_ref[r:r + sub, :] for r in range(0, n_blk, sub)]
            ranks = [jnp.zeros((sub, ATT_Q), F32) for _ in groups]
            for k in range(n_blk):
                row = jnp.broadcast_to(sc_ref[k:k + 1, :], (sub, ATT_Q))
                for r, grp in enumerate(groups):
                    if r * sub > k:
                        ahead = row >= grp
                    elif r * sub + sub - 1 < k:
                        ahead = row > grp
                    else:
                        later = (r * sub + lax.broadcasted_iota(jnp.int32, (sub, 1), 0)) > k
                        ahead = (row > grp) | ((row == grp) & later)
                    ranks[r] = ranks[r] + jnp.where(ahead, 1.0, 0.0)
            chosen = (jnp.concatenate(ranks, axis=0) < n_top) & (jnp.concatenate(groups, axis=0) > 0.5 * NEG)
            bias = jnp.where(chosen, 0.0, NEG).astype(BF16)
            rest = bias_ref.shape[0] - n_blk
            return jnp.concatenate([bias, jnp.full((rest, ATT_Q), NEG, BF16)], axis=0) if rest else bias

        n_buckets = -(-n_sel // RANK_BUCKET)
        bucket = jnp.minimum((qb + 1) * (ATT_Q // SEL_BLOCK) - 1, n_sel - 1) // RANK_BUCKET
        for b in range(n_buckets):
            @pl.when(bucket == b)
            def _():
                bias_ref[...] = selection_bias(min((b + 1) * RANK_BUCKET, n_sel))
        qaug_ref[qb % 2] = jnp.concatenate([qt, jnp.concatenate([bias_ref[...]] * GQA_GROUP, axis=1)], axis=0)

        for br in range(2):
            m_ref[2 * slot + br] = jnp.full((1, nq), NEG, F32)
            acc_ref[2 * slot + br] = jnp.zeros((VT_ROWS, nq), F32)

    def epilogue(qb, slot):
        a_s, a_w = acc_ref[2 * slot], acc_ref[2 * slot + 1]
        o_s = a_s[:dh] / a_s[dh:dh + 1]
        o_w = a_w[:dh] / a_w[dh:dh + 1]
        gates = g_ref[0, 0, qb]

        def gate_row(br):
            return jnp.concatenate([gates[br * GQA_GROUP + g:br * GQA_GROUP + g + 1, :] for g in range(GQA_GROUP)],
                                   axis=1)
        o = gate_row(0) * oc_ref[slot] + gate_row(1) * o_s + gate_row(2) * o_w
        rows = pl.ds(pl.multiple_of(qb * ATT_Q, ATT_Q), ATT_Q)
        o_ref[0, rows, :] = jnp.concatenate(
            [o[:, g * ATT_Q:(g + 1) * ATT_Q].T for g in range(GQA_GROUP)], axis=1).astype(BF16)

    def stage_a(t, u):
        tile, pat, qs = jt_ref[t], jp_ref[t], jq_ref[t]
        k = kall_ref[pl.ds(pl.multiple_of(tile * KEY_TILE, KEY_TILE), KEY_TILE), :]
        s = _dot(k, qaug_ref[qs]) + jnp.concatenate([pat_ref[pat]] * GQA_GROUP, axis=1)
        s_ref[u] = s
        mt_ref[u] = jnp.max(s, axis=0, keepdims=True)

    def stage_b(t, u):
        a = ja_ref[t]
        m_old = m_ref[a]
        m_new = jnp.maximum(m_old, mt_ref[u])
        alpha = jnp.exp2(m_old - m_new)
        m_ref[a] = m_new
        al_ref[u] = alpha
        p_ref[u] = jnp.exp2(s_ref[u] - m_new).astype(BF16)

    def stage_c(t, u):
        tile, a = jt_ref[t], ja_ref[t]
        acc_ref[a] = al_ref[u] * acc_ref[a] + _dot(vt_ref[tile], p_ref[u])

    def step(g, carry):
        t0 = g * u_jobs
        for u in range(u_jobs):
            stage_c(t0 + u, u)
        for u in range(u_jobs):
            stage_b(t0 + u_jobs + u, u)
        for u in range(u_jobs):
            stage_a(t0 + 2 * u_jobs + u, u)
        return carry

    prologue(0)

    def block(i, carry):
        prologue(jnp.minimum(i + 1, n_qb - 1))
        epilogue(jnp.maximum(i - EPILOGUE_LAG, 0), (i - EPILOGUE_LAG) % ACC_RING)
        lax.fori_loop(lo_ref[i], lo_ref[i + 1], step, 0)
        return carry

    lax.fori_loop(0, n_qb, block, 0)
    for qb in range(max(n_qb - EPILOGUE_LAG, 0), n_qb):
        epilogue(qb, qb % ACC_RING)


def _attention(q_t, gates_t, kcv, vc_t, ks_aug, kw_aug, vs_t, vw_t):
    bsz, hkv, s, kdim = ks_aug.shape
    dh = vs_t.shape[2]
    n_cmp = kcv.shape[2]
    gd = GQA_GROUP * dh
    nq = GQA_GROUP * ATT_Q
    n_qb = s // ATT_Q
    patterns = _mask_patterns()
    cmp_vis = _compressed_visibility(s)
    tables = _job_tables(s)
    per_head = lambda shape: pl.BlockSpec((1, 1) + shape, lambda b, h, *_: (b, h) + (0,) * len(shape))
    n_acc = 2 * ACC_RING + 1
    return pl.pallas_call(
        _attn_kernel,
        grid_spec=pltpu.PrefetchScalarGridSpec(
            num_scalar_prefetch=len(tables),
            grid=(bsz, hkv),
            in_specs=[
                per_head((n_qb, gd, ATT_Q)),
                per_head((n_qb, 3 * GQA_GROUP, ATT_Q)),
                per_head((n_cmp, kcv.shape[3])),
                per_head((dh, n_cmp)),
                per_head((s, kdim)),
                per_head((s, kdim)),
                per_head((dh, s)),
                per_head((dh, s)),
                pl.BlockSpec(patterns.shape, lambda b, h, *_: (0, 0, 0)),
                pl.BlockSpec(cmp_vis.shape, lambda b, h, *_: (0, 0)),
            ],
            out_specs=pl.BlockSpec((1, s, gd), lambda b, h, *_: (b, 0, h)),
            scratch_shapes=[
                pltpu.VMEM((2 * s, kdim), BF16),
                pltpu.VMEM((2 * (s // KEY_TILE), VT_ROWS, KEY_TILE), BF16),
                pltpu.VMEM((s // SEL_BLOCK, ATT_Q), F32),
                pltpu.VMEM((kdim - dh, ATT_Q), BF16),
                pltpu.VMEM((2, kdim, nq), BF16),
                pltpu.VMEM((ATT_UNROLL, KEY_TILE, nq), F32),
                pltpu.VMEM((ATT_UNROLL, KEY_TILE, nq), BF16),
                pltpu.VMEM((ATT_UNROLL, 1, nq), F32),
                pltpu.VMEM((ATT_UNROLL, 1, nq), F32),
                pltpu.VMEM((n_acc, 1, nq), F32),
                pltpu.VMEM((n_acc, VT_ROWS, nq), F32),
                pltpu.VMEM((ACC_RING, dh, nq), F32),
            ],
        ),
        out_shape=jax.ShapeDtypeStruct((bsz, s, hkv * gd), BF16),
        compiler_params=_params("parallel", "arbitrary"),
        name="nsa_attention",
    )(*tables, q_t, gates_t, kcv, vc_t, ks_aug, kw_aug, vs_t, vw_t, patterns, cmp_vis)


def _oproj_kernel(o_ref, x_ref, mod_ref, w_ref, g_ref, b_ref, out_ref):
    y = _dot(o_ref[0], w_ref[...])
    out_ref[0] = _layer_norm(ALPHA * x_ref[0] + mod_ref[0, 2:3, :] * y, g_ref[...], b_ref[...])


def _out_projection(o, x, mod, w_o, ln_g, ln_b):
    bsz, s, d = x.shape
    tm = 512
    return pl.pallas_call(
        _oproj_kernel,
        grid=(bsz, s // tm),
        in_specs=[
            pl.BlockSpec((1, tm, o.shape[-1]), lambda b, j: (b, j, 0)),
            pl.BlockSpec((1, tm, d), lambda b, j: (b, j, 0)),
            pl.BlockSpec((1, 6, d), lambda b, j: (b, 0, 0)),
            _const_spec(w_o.shape),
            _const_spec((1, d)),
            _const_spec((1, d)),
        ],
        out_specs=pl.BlockSpec((1, tm, d), lambda b, j: (b, j, 0)),
        out_shape=jax.ShapeDtypeStruct((bsz, s, d), F32),
        compiler_params=_params("parallel", "parallel"),
        name="nsa_out_proj",
    )(o, x, mod, w_o, ln_g.reshape(1, d), ln_b.reshape(1, d))


def _moe_block(x, mod, router_wt, router_bias, w_gu, w_down, layer, ln_g, ln_b):
    bsz, s, d = x.shape
    t = bsz * s
    xf = x.reshape(t, d)
    wcol, route, counts = _router(xf, mod, router_wt, router_bias, s)

    cnt = counts[:, 0]
    padded = (cnt + EXPERT_TILE - 1) // EXPERT_TILE * EXPERT_TILE
    ends = jnp.cumsum(padded)
    offs = ends - padded
    n_tiles = (TOP_K * t) // EXPERT_TILE + N_EXPERTS
    tile_start = jnp.arange(n_tiles, dtype=jnp.int32) * EXPERT_TILE
    tile_expert = jnp.minimum(jnp.sum(tile_start[:, None] >= ends[None, :], axis=1), N_EXPERTS - 1).astype(jnp.int32)
    n_used = (ends[-1:] // EXPERT_TILE).astype(jnp.int32)
    experts, ranks = route[:TOP_K], route[TOP_K:2 * TOP_K]
    dst = ranks + sum(jnp.where(experts == e, offs[e], 0) for e in range(N_EXPERTS))
    idx = dst.reshape(TOP_K, t // TOKEN_TILE, TOKEN_TILE).transpose(1, 0, 2).reshape(t // TOKEN_TILE, 1, TOP_K * TOKEN_TILE)

    last_tile = jnp.where(cnt > 0, ends - EXPERT_TILE, -1)
    spare = n_used + jnp.arange(N_EXPERTS)
    spare = jnp.where(spare < n_tiles, spare * EXPERT_TILE, -1)
    xs = _dispatch(jnp.concatenate([last_tile, spare]).astype(jnp.int32), idx, xf, mod, n_tiles * EXPERT_TILE, s)
    ys = _experts(tile_expert, n_used, xs, w_gu, w_down, layer)
    out = _combine(idx, xf, mod, wcol, ys, ln_g, ln_b, s)
    return out.reshape(bsz, s, d)


def _nsa_layer(x, mod, w_kv, cmp_pe, cmp_w1, cmp_b1, cmp_w2, cmp_b2, w_qg, b_g, w_o, ln_g, ln_b):
    bsz, s, d = x.shape
    hkv, dh, grp = N_KV_HEADS, HEAD_DIM, GQA_GROUP
    hw = hkv * dh
    kvw = w_kv.reshape(d, 6, hw)
    kvc_w = jnp.stack([kvw[:, 0].reshape(d, hkv, dh), kvw[:, 1].reshape(d, hkv, dh)], axis=2).reshape(d, 2 * hw)
    w_nat = jnp.concatenate([kvc_w, kvw[:, 2], kvw[:, 4]], axis=1).astype(BF16)
    w_vt = jnp.concatenate([kvw[:, 3], kvw[:, 5]], axis=1).T.astype(BF16)
    nq = N_HEADS * dh
    wg = w_qg[:, nq:].reshape(d, hkv, grp, 3).transpose(0, 1, 3, 2).reshape(d, 3 * N_HEADS)
    w_qgt = jnp.concatenate([w_qg[:, :nq], wg], axis=1).T.astype(BF16)
    bg = b_g.reshape(hkv, grp, 3).transpose(0, 2, 1).reshape(3 * N_HEADS, 1)
    kvc, ks_aug, kw_aug, vs_t, vw_t, q_t, gates_t = _projections(x, mod, w_nat, w_vt, w_qgt, bg)
    cmp_kv, cmp_vt = _compress(kvc, cmp_pe, cmp_w1, cmp_b1, cmp_w2, cmp_b2)
    o = _attention(q_t, gates_t, cmp_kv, cmp_vt, ks_aug, kw_aug, vs_t, vw_t)
    return _out_projection(o, x, mod, w_o.astype(BF16), ln_g, ln_b)


def kernel(x, c, ada_w, ada_b, ln_g, ln_b, conv_w_in, conv_w, conv_b, conv_w_out, w_kv, cmp_pe, cmp_w1, cmp_b1, cmp_w2, cmp_b2, w_qg, b_g, w_o, router_w, router_bias, w_gu, w_down):
    bsz, s, d = x.shape
    mod = _modulation(c, ada_w, ada_b).reshape(DEPTH, bsz, 6, d)
    router_wt = router_w.T

    x = _conv_layer(x, mod[0], conv_w_in[0].astype(BF16), conv_w[0], conv_b[0], conv_w_out[0].astype(BF16),
                    ln_g[0, 0], ln_b[0, 0])
    x = _moe_block(x, mod[0], router_wt, router_bias, w_gu, w_down, 0, ln_g[0, 1], ln_b[0, 1])

    x = _nsa_layer(x, mod[1], w_kv, cmp_pe, cmp_w1, cmp_b1, cmp_w2, cmp_b2, w_qg[0], b_g[0], w_o[0],
                   ln_g[1, 0], ln_b[1, 0])
    x = _moe_block(x, mod[1], router_wt, router_bias, w_gu, w_down, 1, ln_g[1, 1], ln_b[1, 1])
    return x
```

```python
import math

import jax
import jax.numpy as jnp
import numpy as np
from jax import lax
from jax.experimental import pallas as pl
from jax.experimental.pallas import tpu as pltpu

F32 = jnp.float32
BF16 = jnp.bfloat16

D_MODEL = 1024
DEPTH = 2
CONV_WIDTH = 3
N_HEADS = 16
HEAD_DIM = D_MODEL // N_HEADS
N_KV_HEADS = 4
GQA_GROUP = N_HEADS // N_KV_HEADS
CMP_BLOCK = 32
CMP_STRIDE = 16
SEL_BLOCK = 64
SEL_TOP = 16
WINDOW = 512
N_EXPERTS = 16
N_GROUPS = 4
EXPERTS_PER_GROUP = N_EXPERTS // N_GROUPS
TOP_K = 2
ALPHA = (2 * DEPTH) ** 0.25
LN_EPS = 1e-5
NEG = -1e30
FORCE = 1e9

LANES = 128
VMEM_LIMIT_BYTES = 56 * 1024 * 1024

LOG2E = math.log2(math.e)

ATT_Q = 256
KEY_TILE = 256
PAT_ZERO, PAT_DIAG, PAT_ANTI, PAT_NONE = range(4)
ATT_UNROLL = 4
ACC_RING = 8
EPILOGUE_LAG = 3
RANK_BUCKET = 16
VT_ROWS = HEAD_DIM + 16


def _dot(a, b):
    return jnp.dot(a, b, preferred_element_type=F32)


def _dot_nt(a, b):
    return lax.dot_general(a, b, (((1,), (1,)), ((), ())), preferred_element_type=F32)


def _split(x):
    hi = x.astype(BF16)
    lo = (x - hi.astype(F32)).astype(BF16)
    return hi, lo


def _dot3(a, b):
    ah, al = _split(a)
    bh, bl = _split(b)
    return _dot(ah, bh) + (_dot(ah, bl) + _dot(al, bh))


def _dot3_nt(a, b):
    ah, al = _split(a)
    bh, bl = _split(b)
    return _dot_nt(ah, bh) + (_dot_nt(ah, bl) + _dot_nt(al, bh))


def _layer_norm(r, g, b):
    mu = jnp.mean(r, axis=-1, keepdims=True)
    d = r - mu
    var = jnp.mean(d * d, axis=-1, keepdims=True)
    return d * lax.rsqrt(var + LN_EPS) * g + b


def _params(*sem):
    return pltpu.CompilerParams(dimension_semantics=sem, vmem_limit_bytes=VMEM_LIMIT_BYTES)


def _const_spec(shape):
    zeros = (0,) * len(shape)
    return pl.BlockSpec(shape, lambda *_: zeros)


def _mod_kernel(c_ref, w_ref, b_ref, o_ref):
    c = c_ref[...]
    s = c * jax.nn.sigmoid(c)
    o_ref[0] = _dot3(s, w_ref[0]) + b_ref[0]


def _modulation(c, ada_w, ada_b):
    depth, d, n = ada_w.shape
    bsz = c.shape[0]
    tn = 1536
    return pl.pallas_call(
        _mod_kernel,
        grid=(depth, n // tn),
        in_specs=[
            pl.BlockSpec((bsz, d), lambda l, j: (0, 0)),
            pl.BlockSpec((1, d, tn), lambda l, j: (l, 0, j)),
            pl.BlockSpec((1, 1, tn), lambda l, j: (l, 0, j)),
        ],
        out_specs=pl.BlockSpec((1, bsz, tn), lambda l, j: (l, 0, j)),
        out_shape=jax.ShapeDtypeStruct((depth, bsz, n), F32),
        compiler_params=_params("parallel", "parallel"),
        name="adaln_mod",
    )(c, ada_w, ada_b.reshape(depth, 1, n))


CONV_HALO = 8


def _conv_kernel(x_ref, mod_ref, win_ref, cw_ref, cb_ref, wout_ref, g_ref, b_ref, o_ref, z_ref):
    tm = x_ref.shape[1]
    d = x_ref.shape[2]

    @pl.when(pl.program_id(1) == 0)
    def _():
        z_ref[0:CONV_HALO, :] = jnp.zeros((CONV_HALO, d), F32)

    x = x_ref[0]
    sh = mod_ref[0, 0:1, :]
    sc = mod_ref[0, 1:2, :]
    gate = mod_ref[0, 2:3, :]
    u = (x * (1.0 + sc) + sh).astype(BF16)
    bch = _dot(u, win_ref[...])
    z = bch[:, d:2 * d] * bch[:, 2 * d:]
    z_ref[CONV_HALO:CONV_HALO + tm, :] = z
    z1 = z_ref[CONV_HALO - 1:CONV_HALO - 1 + tm, :]
    z2 = z_ref[CONV_HALO - 2:CONV_HALO - 2 + tm, :]
    conv = cw_ref[0:1, :] * z2 + cw_ref[1:2, :] * z1 + cw_ref[2:3, :] * z + cb_ref[...]
    v = (bch[:, :d] * conv).astype(BF16)
    y = _dot(v, wout_ref[...])
    o_ref[0] = _layer_norm(ALPHA * x + gate * y, g_ref[...], b_ref[...])
    z_ref[0:CONV_HALO, :] = z_ref[tm:tm + CONV_HALO, :]


def _conv_layer(x, mod, w_in, conv_w, conv_b, w_out, ln_g, ln_b):
    bsz, s, d = x.shape
    tm = 512
    return pl.pallas_call(
        _conv_kernel,
        grid=(bsz, s // tm),
        in_specs=[
            pl.BlockSpec((1, tm, d), lambda b, j: (b, j, 0)),
            pl.BlockSpec((1, 6, d), lambda b, j: (b, 0, 0)),
            _const_spec((d, 3 * d)),
            _const_spec((CONV_WIDTH, d)),
            _const_spec((1, d)),
            _const_spec((d, d)),
            _const_spec((1, d)),
            _const_spec((1, d)),
        ],
        out_specs=pl.BlockSpec((1, tm, d), lambda b, j: (b, j, 0)),
        out_shape=jax.ShapeDtypeStruct((bsz, s, d), F32),
        scratch_shapes=[pltpu.VMEM((tm + CONV_HALO, d), F32)],
        compiler_params=_params("arbitrary", "arbitrary"),
        name="conv_mixer",
    )(x, mod, w_in, conv_w, conv_b.reshape(1, d), w_out, ln_g.reshape(1, d), ln_b.reshape(1, d))


def _router_kernel(x_ref, mod_ref, rwt_ref, rb_ref, wcol_ref, route_ref, count_ref, base_ref, tri_ref):
    tm = x_ref.shape[0]
    x = x_ref[...]
    u = x * (1.0 + mod_ref[0, 4:5, :]) + mod_ref[0, 3:4, :]
    logits = _dot3_nt(rwt_ref[...], u)
    m = jnp.max(logits, axis=0, keepdims=True)
    e = jnp.exp(logits - m)
    aff = e / jnp.sum(e, axis=0, keepdims=True)
    biased = aff + rb_ref[...]
    aff_r = [aff[i:i + 1, :] for i in range(N_EXPERTS)]
    row = [biased[i:i + 1, :] for i in range(N_EXPERTS)]

    best_s, best = None, None
    for g in range(N_GROUPS):
        r = row[g * EXPERTS_PER_GROUP:(g + 1) * EXPERTS_PER_GROUP]
        gs = None
        for i in range(EXPERTS_PER_GROUP):
            for j in range(i + 1, EXPERTS_PER_GROUP):
                p = r[i] + r[j]
                gs = p if gs is None else jnp.maximum(gs, p)
        if g == 0:
            best_s, best = gs, jnp.zeros((1, tm), jnp.int32)
        else:
            upd = gs > best_s
            best = jnp.where(upd, g, best)
            best_s = jnp.where(upd, gs, best_s)

    masked = [jnp.where(best == (i // EXPERTS_PER_GROUP), row[i], NEG) for i in range(N_EXPERTS)]

    def first_argmax(vals):
        v, idx = vals[0], jnp.zeros((1, tm), jnp.int32)
        for i in range(1, N_EXPERTS):
            upd = vals[i] > v
            idx = jnp.where(upd, i, idx)
            v = jnp.where(upd, vals[i], v)
        return idx

    i0 = first_argmax(masked)
    i1 = first_argmax([jnp.where(i0 == i, -jnp.inf, masked[i]) for i in range(N_EXPERTS)])
    w0 = sum(jnp.where(i0 == i, aff_r[i], 0.0) for i in range(N_EXPERTS))
    w1 = sum(jnp.where(i1 == i, aff_r[i], 0.0) for i in range(N_EXPERTS))
    tot = w0 + w1
    w0 = w0 / tot
    w1 = w1 / tot
    w_t = jnp.concatenate([w0, w1, jnp.zeros((LANES - TOP_K, tm), F32)], axis=0)
    wcol_ref[...] = w_t.T

    @pl.when(pl.program_id(0) == 0)
    def _():
        base_ref[...] = jnp.zeros(base_ref.shape, F32)
        r_i = lax.broadcasted_iota(jnp.int32, tri_ref.shape, 0)
        c_i = lax.broadcasted_iota(jnp.int32, tri_ref.shape, 1)
        tri_ref[...] = jnp.where(r_i <= c_i, 1.0, 0.0).astype(BF16)

    hot = jnp.concatenate([jnp.where((i0 == i) | (i1 == i), 1.0, 0.0) for i in range(N_EXPERTS)], axis=0)
    incl = _dot(hot.astype(BF16), tri_ref[...])
    pos = base_ref[...] + (incl - hot)
    rank0 = sum(jnp.where(i0 == i, pos[i:i + 1, :], 0.0) for i in range(N_EXPERTS))
    rank1 = sum(jnp.where(i1 == i, pos[i:i + 1, :], 0.0) for i in range(N_EXPERTS))
    base = base_ref[...] + jnp.sum(hot, axis=1, keepdims=True)
    base_ref[...] = base
    route_ref[...] = jnp.concatenate(
        [i0, i1, rank0.astype(jnp.int32), rank1.astype(jnp.int32), jnp.zeros((4, tm), jnp.int32)], axis=0)
    count_ref[...] = jnp.broadcast_to(base, count_ref.shape).astype(jnp.int32)


def _router(x, mod, router_wt, router_bias, seq):
    t, d = x.shape
    tm = 1024
    return pl.pallas_call(
        _router_kernel,
        grid=(t // tm,),
        in_specs=[
            pl.BlockSpec((tm, d), lambda i: (i, 0)),
            pl.BlockSpec((1, 6, d), lambda i: ((i * tm) // seq, 0, 0)),
            _const_spec((N_EXPERTS, d)),
            _const_spec((N_EXPERTS, 1)),
        ],
        out_specs=(
            pl.BlockSpec((tm, LANES), lambda i: (i, 0)),
            pl.BlockSpec((8, tm), lambda i: (0, i)),
            _const_spec((N_EXPERTS, LANES)),
        ),
        out_shape=(
            jax.ShapeDtypeStruct((t, LANES), F32),
            jax.ShapeDtypeStruct((8, t), jnp.int32),
            jax.ShapeDtypeStruct((N_EXPERTS, LANES), jnp.int32),
        ),
        scratch_shapes=[pltpu.VMEM((N_EXPERTS, 1), F32), pltpu.VMEM((tm, tm), BF16)],
        compiler_params=_params("arbitrary"),
        name="moe_router",
    )(x, mod, router_wt, router_bias.reshape(N_EXPERTS, 1))


EXPERT_TILE = 512
TOKEN_TILE = 512
ROW_COPY_UNROLL = 8
ROW_TILES = D_MODEL // LANES


def _store_rows(ref, x):
    rows = x.shape[0]
    for c in range(ROW_TILES):
        ref[pl.ds(c, rows, stride=ROW_TILES), :] = x[:, c * LANES:(c + 1) * LANES]


def _load_rows(ref, rows):
    return jnp.concatenate([ref[pl.ds(c, rows, stride=ROW_TILES), :] for c in range(ROW_TILES)], axis=1)


def _sorted_row(offs_ref, route_ref, tm, t, k):
    return offs_ref[route_ref[0, 0, k * tm + t]] + route_ref[0, 0, (TOP_K + k) * tm + t]


def _row(ref, r):
    return ref.at[pl.ds(pl.multiple_of(r * ROW_TILES, ROW_TILES), ROW_TILES)]


def _start_row_copies(tm, row_copy):
    def start(t, c):
        for k in range(TOP_K):
            row_copy(t, k).start(priority=k % 2)
        return c

    lax.fori_loop(0, tm, start, 0, unroll=ROW_COPY_UNROLL)


def _dispatch_kernel(pad_ref, offs_ref, idx_ref, x_ref, mod_ref, xs_ref, u_ref, sem):
    tm = x_ref.shape[0]

    i = pl.program_id(0)
    slot = i % 2

    @pl.when(i == 0)
    def _():
        u_ref[1] = jnp.zeros(u_ref.shape[1:], F32)
        for wait in (False, True):
            for e in range(pad_ref.shape[0]):
                @pl.when(pad_ref[e] >= 0)
                def _():
                    row = pl.multiple_of(jnp.maximum(pad_ref[e], 0) * ROW_TILES, tm * ROW_TILES)
                    fill = pltpu.make_async_copy(u_ref.at[1], xs_ref.at[pl.ds(row, tm * ROW_TILES)], sem.at[1])
                    fill.wait() if wait else fill.start()

    _store_rows(u_ref.at[slot], x_ref[...] * (1.0 + mod_ref[0, 4:5, :]) + mod_ref[0, 3:4, :])

    def row_copy(t, k):
        dst = _sorted_row(offs_ref, idx_ref, tm, t, k)
        return pltpu.make_async_copy(_row(u_ref.at[slot], t), _row(xs_ref, dst), sem.at[slot])

    _start_row_copies(tm, row_copy)

    def wait_tile(s):
        for _ in range(TOP_K):
            pltpu.make_async_copy(u_ref.at[s], xs_ref.at[pl.ds(0, tm * ROW_TILES)], sem.at[s]).wait()

    @pl.when(i > 0)
    def _():
        wait_tile(1 - slot)

    @pl.when(i == pl.num_programs(0) - 1)
    def _():
        wait_tile(slot)


def _dispatch(pad_start, offs, idx, x, mod, n_rows, seq):
    t, d = x.shape
    tm = TOKEN_TILE
    assert tm == EXPERT_TILE
    return pl.pallas_call(
        _dispatch_kernel,
        grid_spec=pltpu.PrefetchScalarGridSpec(
            num_scalar_prefetch=2,
            grid=(t // tm,),
            in_specs=[
                pl.BlockSpec((1, 1, 2 * TOP_K * tm), lambda i, *_: (i, 0, 0), memory_space=pltpu.SMEM),
                pl.BlockSpec((tm, d), lambda i, *_: (i, 0)),
                pl.BlockSpec((1, 6, d), lambda i, *_: ((i * tm) // seq, 0, 0)),
            ],
            out_specs=pl.BlockSpec(memory_space=pl.ANY),
            scratch_shapes=[pltpu.VMEM((2, tm * ROW_TILES, LANES), F32), pltpu.SemaphoreType.DMA((2,))],
        ),
        out_shape=jax.ShapeDtypeStruct((n_rows * ROW_TILES, LANES), F32),
        compiler_params=_params("arbitrary"),
        name="moe_dispatch",
    )(pad_start, offs, idx, x, mod)


def _expert_kernel(te_ref, first_ref, par_ref, nu_ref, xs_ref, wgu0_ref, wgu_next_ref, wdn_ref, ys_ref, wgu_bf, wdn_bf):
    del te_ref
    s = pl.program_id(0)
    used = s < nu_ref[0]
    nxt = jnp.minimum(s + 1, pl.num_programs(0) - 1)

    @pl.when(s == 0)
    def _():
        wgu_bf[par_ref[0]] = wgu0_ref[0, 0].astype(BF16)

    @pl.when((s + 1 < nu_ref[0]) & (first_ref[nxt] == 1))
    def _():
        wgu_bf[par_ref[nxt]] = wgu_next_ref[0, 0].astype(BF16)

    @pl.when(used & (first_ref[s] == 1))
    def _():
        wdn_bf[...] = wdn_ref[0, 0].astype(BF16)

    @pl.when(used)
    def _():
        f = wdn_bf.shape[0]
        h = _dot(_load_rows(xs_ref, EXPERT_TILE).astype(BF16), wgu_bf[par_ref[s]])
        gate = h[:, :f]
        a = (gate * jax.nn.sigmoid(gate) * h[:, f:]).astype(BF16)
        _store_rows(ys_ref, _dot(a, wdn_bf[...]))

    @pl.when(jnp.logical_not(used))
    def _():
        ys_ref[...] = jnp.zeros(ys_ref.shape, F32)


def _experts(tile_expert, n_used, xs, w_gu, w_down, layer):
    _, _, d, f2 = w_gu.shape
    f = f2 // 2
    n_tiles = tile_expert.shape[0]
    first = jnp.concatenate([jnp.ones((1,), jnp.int32), (tile_expert[1:] != tile_expert[:-1]).astype(jnp.int32)])
    parity = ((jnp.cumsum(first) - 1) % 2).astype(jnp.int32)
    row_map = lambda s, te, fi, pa, nu: (s, 0)
    in_row_map = lambda s, te, fi, pa, nu: (jnp.minimum(s, nu[0]), 0)
    return pl.pallas_call(
        _expert_kernel,
        grid_spec=pltpu.PrefetchScalarGridSpec(
            num_scalar_prefetch=4,
            grid=(n_tiles,),
            in_specs=[
                pl.BlockSpec((EXPERT_TILE * ROW_TILES, LANES), in_row_map),
                pl.BlockSpec((1, 1, d, f2), lambda s, te, fi, pa, nu: (layer, te[0], 0, 0)),
                pl.BlockSpec((1, 1, d, f2), lambda s, te, fi, pa, nu: (layer, te[jnp.minimum(s + 1, n_tiles - 1)], 0, 0)),
                pl.BlockSpec((1, 1, f, d), lambda s, te, fi, pa, nu: (layer, te[s], 0, 0)),
            ],
            out_specs=pl.BlockSpec((EXPERT_TILE * ROW_TILES, LANES), row_map),
            scratch_shapes=[pltpu.VMEM((2, d, f2), BF16), pltpu.VMEM((f, d), BF16)],
        ),
        out_shape=jax.ShapeDtypeStruct(xs.shape, F32),
        compiler_params=_params("arbitrary"),
        name="moe_experts",
    )(tile_expert, first, parity, n_used, xs, w_gu, w_gu, w_down)


def _combine_kernel(offs_ref, idx_ref, next_idx_ref, x_ref, mod_ref, w_ref, ys_ref, g_ref, b_ref, o_ref, y_ref, sem):
    tm = x_ref.shape[0]
    i = pl.program_id(0)
    slot = i % 2

    def gather(ids_ref, into):
        def row_copy(t, k):
            src = _sorted_row(offs_ref, ids_ref, tm, t, k)
            return pltpu.make_async_copy(_row(ys_ref, src), _row(y_ref.at[into, k], t), sem.at[into])

        _start_row_copies(tm, row_copy)

    @pl.when(i == 0)
    def _():
        gather(idx_ref, 0)

    @pl.when(i + 1 < pl.num_programs(0))
    def _():
        gather(next_idx_ref, 1 - slot)

    for k in range(TOP_K):
        pltpu.make_async_copy(ys_ref.at[pl.ds(0, tm * ROW_TILES)], y_ref.at[slot, k], sem.at[slot]).wait()
    w = w_ref[...]
    out = w[:, 0:1] * _load_rows(y_ref.at[slot, 0], tm) + w[:, 1:2] * _load_rows(y_ref.at[slot, 1], tm)
    r = ALPHA * x_ref[...] + mod_ref[0, 5:6, :] * out
    o_ref[...] = _layer_norm(r, g_ref[...], b_ref[...])


def _combine(offs, idx, x, mod, wcol, ys, ln_g, ln_b, seq):
    t, d = x.shape
    tm = TOKEN_TILE
    n_tiles = t // tm
    route_block = (1, 1, 2 * TOP_K * tm)
    return pl.pallas_call(
        _combine_kernel,
        grid_spec=pltpu.PrefetchScalarGridSpec(
            num_scalar_prefetch=1,
            grid=(n_tiles,),
            in_specs=[
                pl.BlockSpec(route_block, lambda i, *_: (i, 0, 0), memory_space=pltpu.SMEM),
                pl.BlockSpec(route_block, lambda i, *_: (jnp.minimum(i + 1, n_tiles - 1), 0, 0), memory_space=pltpu.SMEM),
                pl.BlockSpec((tm, d), lambda i, *_: (i, 0)),
                pl.BlockSpec((1, 6, d), lambda i, *_: ((i * tm) // seq, 0, 0)),
                pl.BlockSpec((tm, LANES), lambda i, *_: (i, 0)),
                pl.BlockSpec(memory_space=pl.ANY),
                _const_spec((1, d)),
                _const_spec((1, d)),
            ],
            out_specs=pl.BlockSpec((tm, d), lambda i, *_: (i, 0)),
            scratch_shapes=[pltpu.VMEM((2, TOP_K, tm * ROW_TILES, LANES), F32), pltpu.SemaphoreType.DMA((2,))],
        ),
        out_shape=jax.ShapeDtypeStruct((t, d), F32),
        compiler_params=_params("arbitrary"),
        name="moe_combine",
    )(offs, idx, idx, x, mod, wcol, ys, ln_g.reshape(1, d), ln_b.reshape(1, d))


def _proj_kernel(x_ref, mod_ref, wn_ref, wvt_ref, wqt_ref, bg_ref,
                 kvc_ref, ks_ref, kw_ref, vst_ref, vwt_ref, qt_ref, gt_ref):
    tm = x_ref.shape[1]
    j = pl.program_id(1)
    x = x_ref[0]
    xb = x.astype(BF16)
    nat = _dot(xb, wn_ref[...])
    vt = _dot_nt(wvt_ref[...], xb)
    u = (x * (1.0 + mod_ref[0, 1:2, :]) + mod_ref[0, 0:1, :]).astype(BF16)
    qg = _dot_nt(wqt_ref[...], u)
    nq = N_HEADS * HEAD_DIM
    q = (qg[:nq, :] * (HEAD_DIM ** -0.5 * LOG2E)).astype(BF16)
    gates = jax.nn.sigmoid(qg[nq:, :] + bg_ref[...])
    per = 3 * GQA_GROUP
    blk = (j * tm + lax.broadcasted_iota(jnp.int32, (tm, SEL_BLOCK), 0)) // SEL_BLOCK
    onehot = (blk == lax.broadcasted_iota(jnp.int32, (tm, SEL_BLOCK), 1)).astype(BF16)
    zeros = jnp.zeros((tm, SEL_BLOCK), BF16)
    hw = N_KV_HEADS * HEAD_DIM
    for h in range(N_KV_HEADS):
        c0 = h * HEAD_DIM
        kvc_ref[0, h] = nat[:, 2 * c0:2 * c0 + 2 * HEAD_DIM]
        ks = nat[:, 2 * hw + c0:2 * hw + c0 + HEAD_DIM].astype(BF16)
        ks_ref[0, h] = jnp.concatenate([ks, onehot], axis=1)
        kw = nat[:, 3 * hw + c0:3 * hw + c0 + HEAD_DIM].astype(BF16)
        kw_ref[0, h] = jnp.concatenate([kw, zeros], axis=1)
        vst_ref[0, h] = vt[c0:c0 + HEAD_DIM, :].astype(BF16)
        vwt_ref[0, h] = vt[hw + c0:hw + c0 + HEAD_DIM, :].astype(BF16)
        for p in range(tm // ATT_Q):
            cols = slice(p * ATT_Q, (p + 1) * ATT_Q)
            qt_ref[0, h, p] = q[h * GQA_GROUP * HEAD_DIM:(h + 1) * GQA_GROUP * HEAD_DIM, cols]
            gt_ref[0, h, p] = gates[h * per:(h + 1) * per, cols]


def _projections(x, mod, w_nat, w_vt, w_qgt, b_g):
    bsz, s, d = x.shape
    tm = 512
    hkv, dh = N_KV_HEADS, HEAD_DIM
    per = 3 * GQA_GROUP
    nqg = w_qgt.shape[0]
    out_shape = (
        jax.ShapeDtypeStruct((bsz, hkv, s, 2 * dh), F32),
        jax.ShapeDtypeStruct((bsz, hkv, s, 2 * dh), BF16),
        jax.ShapeDtypeStruct((bsz, hkv, s, 2 * dh), BF16),
        jax.ShapeDtypeStruct((bsz, hkv, dh, s), BF16),
        jax.ShapeDtypeStruct((bsz, hkv, dh, s), BF16),
        jax.ShapeDtypeStruct((bsz, hkv, s // ATT_Q, GQA_GROUP * dh, ATT_Q), BF16),
        jax.ShapeDtypeStruct((bsz, hkv, s // ATT_Q, per, ATT_Q), F32),
    )
    out_specs = (
        pl.BlockSpec((1, hkv, tm, 2 * dh), lambda b, j: (b, 0, j, 0)),
        pl.BlockSpec((1, hkv, tm, 2 * dh), lambda b, j: (b, 0, j, 0)),
        pl.BlockSpec((1, hkv, tm, 2 * dh), lambda b, j: (b, 0, j, 0)),
        pl.BlockSpec((1, hkv, dh, tm), lambda b, j: (b, 0, 0, j)),
        pl.BlockSpec((1, hkv, dh, tm), lambda b, j: (b, 0, 0, j)),
        pl.BlockSpec((1, hkv, tm // ATT_Q, GQA_GROUP * dh, ATT_Q), lambda b, j: (b, 0, j, 0, 0)),
        pl.BlockSpec((1, hkv, tm // ATT_Q, per, ATT_Q), lambda b, j: (b, 0, j, 0, 0)),
    )
    return pl.pallas_call(
        _proj_kernel,
        grid=(bsz, s // tm),
        in_specs=[
            pl.BlockSpec((1, tm, d), lambda b, j: (b, j, 0)),
            pl.BlockSpec((1, 6, d), lambda b, j: (b, 0, 0)),
            _const_spec(w_nat.shape),
            _const_spec(w_vt.shape),
            _const_spec(w_qgt.shape),
            _const_spec((nqg - N_HEADS * dh, 1)),
        ],
        out_specs=out_specs,
        out_shape=out_shape,
        compiler_params=_params("parallel", "parallel"),
        name="nsa_projections",
    )(x, mod, w_nat, w_vt, w_qgt, b_g)


def _compress_kernel(kvc_ref, pe_ref, w1_ref, b1_ref, w2_ref, b2_ref, nat_ref, tr_ref):
    n = kvc_ref.shape[2] // CMP_STRIDE
    hid2 = w1_ref.shape[-1]
    p = jnp.zeros((n, hid2), F32)
    q = jnp.zeros((n, hid2), F32)
    for l in range(CMP_STRIDE):
        x = kvc_ref[0, 0, pl.ds(l, n, stride=CMP_STRIDE), :]
        p = p + _dot((x + pe_ref[l:l + 1, :]).astype(BF16), w1_ref[0, l])
        q = q + _dot((x + pe_ref[CMP_STRIDE + l:CMP_STRIDE + l + 1, :]).astype(BF16), w1_ref[1, l])
    pre = p + pltpu.roll(q, n - 1, 0) + b1_ref[...]
    hdn = 0.5 * pre * (1.0 + jnp.tanh(0.7978845608028654 * (pre + 0.044715 * (pre * pre * pre))))
    out = _dot(hdn.astype(BF16), w2_ref[...]) + b2_ref[...]
    nat_ref[0, 0] = out.astype(BF16)
    tr_ref[0, 0] = out[:, HEAD_DIM:].T.astype(BF16)


def _compress(kvc, cmp_pe, cmp_w1, cmp_b1, cmp_w2, cmp_b2):
    bsz, hkv, s, two_dh = kvc.shape
    dh = two_dh // 2
    n = s // CMP_STRIDE
    hid = cmp_w1.shape[-1]
    zw = jnp.zeros((2, CMP_STRIDE, dh, hid), F32)
    w1 = cmp_w1.reshape(2, 2, CMP_STRIDE, dh, hid)
    w1 = jnp.concatenate([jnp.concatenate([w1[0], zw], axis=-1), jnp.concatenate([zw, w1[1]], axis=-1)], axis=-2)
    zd = jnp.zeros((hid, dh), F32)
    w2 = jnp.concatenate([jnp.concatenate([cmp_w2[0], zd], axis=1), jnp.concatenate([zd, cmp_w2[1]], axis=1)], axis=0)
    pe = jnp.concatenate([cmp_pe[0], cmp_pe[1]], axis=1)
    return pl.pallas_call(
        _compress_kernel,
        grid=(bsz, hkv),
        in_specs=[
            pl.BlockSpec((1, 1, s, two_dh), lambda b, h: (b, h, 0, 0)),
            _const_spec(pe.shape),
            _const_spec(w1.shape),
            _const_spec((1, 2 * hid)),
            _const_spec(w2.shape),
            _const_spec((1, two_dh)),
        ],
        out_specs=(
            pl.BlockSpec((1, 1, n, two_dh), lambda b, h: (b, h, 0, 0)),
            pl.BlockSpec((1, 1, dh, n), lambda b, h: (b, h, 0, 0)),
        ),
        out_shape=(
            jax.ShapeDtypeStruct((bsz, hkv, n, two_dh), BF16),
            jax.ShapeDtypeStruct((bsz, hkv, dh, n), BF16),
        ),
        compiler_params=_params("parallel", "parallel"),
        name="nsa_compress",
    )(kvc, pe, w1.astype(BF16), cmp_b1.reshape(1, 2 * hid), w2.astype(BF16), cmp_b2.reshape(1, two_dh))


def _mask_patterns():
    assert WINDOW == 2 * KEY_TILE and ATT_Q == KEY_TILE
    keyl = np.arange(KEY_TILE)[:, None]
    ql = np.arange(ATT_Q)[None, :]
    true = np.ones((KEY_TILE, ATT_Q), bool)
    valid = np.stack([
        true,
        keyl <= ql,
        keyl > ql,
        ~true,
    ])
    return jnp.asarray(np.where(valid, 0.0, NEG), F32)


def _compressed_visibility(seq):
    n_cmp = seq // CMP_STRIDE
    r = np.arange(-n_cmp, n_cmp)[:, None]
    ql = np.arange(ATT_Q)[None, :]
    return jnp.asarray(np.where(r * CMP_STRIDE + CMP_BLOCK - 1 <= ql, 0.0, NEG), F32)


def _job_tables(seq):
    nqb, nkt, u = seq // ATT_Q, seq // KEY_TILE, ATT_UNROLL
    tile, pat, acc, qs, base = [], [], [], [], []
    for i in range(nqb):
        base.append(len(tile))
        for j in range(i + 1):
            tile.append(j)
            pat.append(PAT_DIAG if j == i else PAT_ZERO)
            acc.append(2 * (i % ACC_RING))
            qs.append(i % 2)
        for w in range(min(i, WINDOW // KEY_TILE) + 1):
            tile.append(nkt + i - w)
            pat.append((PAT_DIAG, PAT_ZERO, PAT_ANTI)[w])
            acc.append(2 * (i % ACC_RING) + 1)
            qs.append(i % 2)
    n_real = len(tile)
    n_steps = (n_real - 1 + 2 * u) // u + 1
    pad = lambda x, fill: np.array([fill] * (2 * u) + x + [fill] * (n_steps * u - n_real), np.int32)
    step_lo = np.array([-(-b // u) for b in base] + [n_steps], np.int32)
    assert all(base[i + 2] >= step_lo[i + 1] * u for i in range(nqb - 2))
    return pad(tile, 0), pad(pat, PAT_NONE), pad(acc, 2 * ACC_RING), pad(qs, 0), step_lo


def _attn_kernel(jt_ref, jp_ref, ja_ref, jq_ref, lo_ref,
                 q_ref, g_ref, kcv_ref, vct_ref, ks_ref, kw_ref, vs_ref, vw_ref, pat_ref, cvis_ref, o_ref,
                 kall_ref, vt_ref, sc_ref, bias_ref, qaug_ref, s_ref, p_ref, al_ref, mt_ref, m_ref, acc_ref, oc_ref):
    seq = ks_ref.shape[2]
    nq = GQA_GROUP * ATT_Q
    dh = HEAD_DIM
    n_key_tiles = seq // KEY_TILE
    n_qb = seq // ATT_Q
    u_jobs = ATT_UNROLL

    kall_ref[0:seq, :] = ks_ref[0, 0]
    kall_ref[seq:2 * seq, :] = kw_ref[0, 0]
    ones_rows = (lax.broadcasted_iota(jnp.int32, (VT_ROWS - dh, KEY_TILE), 0) == 0).astype(BF16)
    for n in range(n_key_tiles):
        vt_ref[n, 0:dh, :] = vs_ref[0, 0, :, n * KEY_TILE:(n + 1) * KEY_TILE]
        vt_ref[n_key_tiles + n, 0:dh, :] = vw_ref[0, 0, :, n * KEY_TILE:(n + 1) * KEY_TILE]
        vt_ref[n, dh:VT_ROWS, :] = ones_rows
        vt_ref[n_key_tiles + n, dh:VT_ROWS, :] = ones_rows

    s_ref[...] = jnp.full(s_ref.shape, NEG, F32)
    mt_ref[...] = jnp.full(mt_ref.shape, NEG, F32)
    p_ref[...] = jnp.zeros(p_ref.shape, BF16)
    al_ref[...] = jnp.ones(al_ref.shape, F32)
    m_ref[...] = jnp.full(m_ref.shape, NEG, F32)
    acc_ref[...] = jnp.ones(acc_ref.shape, F32)
    oc_ref[...] = jnp.zeros(oc_ref.shape, F32)

    def prologue(qb):
        slot = qb % ACC_RING
        qa = q_ref[0, 0, qb]
        qt = jnp.concatenate([qa[g * dh:(g + 1) * dh, :] for g in range(GQA_GROUP)], axis=1)
        t_row = qb * ATT_Q + lax.broadcasted_iota(jnp.int32, (1, ATT_Q), 1)
        t4 = jnp.concatenate([t_row] * GQA_GROUP, axis=1)

        n_cmp = kcv_ref.shape[2]
        q_c = jnp.concatenate([qt, jnp.zeros((kcv_ref.shape[3] - dh, nq), BF16)], axis=0)
        per_qb = ATT_Q // CMP_STRIDE
        vis = cvis_ref[pl.ds(pl.multiple_of(n_cmp - per_qb * qb, per_qb), n_cmp), :]
        s_c = _dot(kcv_ref[0, 0], q_c) + jnp.concatenate([vis] * GQA_GROUP, axis=1)
        m_c = jnp.max(s_c, axis=0, keepdims=True)
        m_c = jnp.where(t4 >= CMP_BLOCK - 1, m_c, 0.0)
        e_c = jnp.exp2(s_c - m_c)
        p_c = e_c / jnp.maximum(jnp.sum(e_c, axis=0, keepdims=True), 1e-30)
        oc_ref[slot] = _dot(vct_ref[0, 0], p_c.astype(BF16))

        n_sel = seq // SEL_BLOCK
        p_sum = sum(p_c[:, g * ATT_Q:(g + 1) * ATT_Q] for g in range(GQA_GROUP))
        jj = lax.broadcasted_iota(jnp.int32, (n_sel, n_cmp), 0) * SEL_BLOCK
        nn = lax.broadcasted_iota(jnp.int32, (n_sel, n_cmp), 1) * CMP_STRIDE
        ov = jnp.minimum(nn + CMP_BLOCK, jj + SEL_BLOCK) - jnp.maximum(nn, jj)
        ov_t = (jnp.maximum(ov, 0).astype(F32) * (1.0 / CMP_BLOCK)).astype(BF16)
        p_hi, p_lo = _split(p_sum)
        imp = _dot(ov_t, p_hi) + _dot(ov_t, p_lo)
        jb = lax.broadcasted_iota(jnp.int32, (n_sel, 1), 0)
        cur = t_row // SEL_BLOCK
        allowed = jb <= cur
        forced = (jb == 0) | (jb == cur) | (jb == cur - 1)
        score = jnp.where(forced & allowed, FORCE, jnp.where(allowed, imp, NEG))
        sc_ref[...] = score
        n_top = min(SEL_TOP, n_sel)

        def selection_bias(n_blk):
            sub = 8
            groups = [sc_ref[r:r + sub, :] for r in range(0, n_blk, sub)]
            ranks = [jnp.zeros((sub, ATT_Q), F32) for _ in groups]
            for k in range(n_blk):
                row = jnp.broadcast_to(sc_ref[k:k + 1, :], (sub, ATT_Q))
                for r, grp in enumerate(groups):
                    if r * sub > k:
                        ahead = row >= grp
                    elif r * sub + sub - 1 < k:
                        ahead = row > grp
                    else:
                        later = (r * sub + lax.broadcasted_iota(jnp.int32, (sub, 1), 0)) > k
                        ahead = (row > grp) | ((row == grp) & later)
                    ranks[r] = ranks[r] + jnp.where(ahead, 1.0, 0.0)
            chosen = (jnp.concatenate(ranks, axis=0) < n_top) & (jnp.concatenate(groups, axis=0) > 0.5 * NEG)
            bias = jnp.where(chosen, 0.0, NEG).astype(BF16)
            rest = bias_ref.shape[0] - n_blk
            return jnp.concatenate([bias, jnp.full((rest, ATT_Q), NEG, BF16)], axis=0) if rest else bias

        n_buckets = -(-n_sel // RANK_BUCKET)
        bucket = jnp.minimum((qb + 1) * (ATT_Q // SEL_BLOCK) - 1, n_sel - 1) // RANK_BUCKET
        for b in range(n_buckets):
            @pl.when(bucket == b)
            def _():
                bias_ref[...] = selection_bias(min((b + 1) * RANK_BUCKET, n_sel))
        qaug_ref[qb % 2] = jnp.concatenate([qt, jnp.concatenate([bias_ref[...]] * GQA_GROUP, axis=1)], axis=0)

        for br in range(2):
            m_ref[2 * slot + br] = jnp.full((1, nq), NEG, F32)
            acc_ref[2 * slot + br] = jnp.zeros((VT_ROWS, nq), F32)

    def epilogue(qb, slot):
        a_s, a_w = acc_ref[2 * slot], acc_ref[2 * slot + 1]
        o_s = a_s[:dh] / a_s[dh:dh + 1]
        o_w = a_w[:dh] / a_w[dh:dh + 1]
        gates = g_ref[0, 0, qb]

        def gate_row(br):
            return jnp.concatenate([gates[br * GQA_GROUP + g:br * GQA_GROUP + g + 1, :] for g in range(GQA_GROUP)],
                                   axis=1)
        o = gate_row(0) * oc_ref[slot] + gate_row(1) * o_s + gate_row(2) * o_w
        rows = pl.ds(pl.multiple_of(qb * ATT_Q, ATT_Q), ATT_Q)
        o_ref[0, rows, :] = jnp.concatenate(
            [o[:, g * ATT_Q:(g + 1) * ATT_Q].T for g in range(GQA_GROUP)], axis=1).astype(BF16)

    def stage_a(t, u):
        tile, pat, qs = jt_ref[t], jp_ref[t], jq_ref[t]
        k = kall_ref[pl.ds(pl.multiple_of(tile * KEY_TILE, KEY_TILE), KEY_TILE), :]
        s = _dot(k, qaug_ref[qs]) + jnp.concatenate([pat_ref[pat]] * GQA_GROUP, axis=1)
        s_ref[u] = s
        mt_ref[u] = jnp.max(s, axis=0, keepdims=True)

    def stage_b(t, u):
        a = ja_ref[t]
        m_old = m_ref[a]
        m_new = jnp.maximum(m_old, mt_ref[u])
        alpha = jnp.exp2(m_old - m_new)
        m_ref[a] = m_new
        al_ref[u] = alpha
        p_ref[u] = jnp.exp2(s_ref[u] - m_new).astype(BF16)

    def stage_c(t, u):
        tile, a = jt_ref[t], ja_ref[t]
        acc_ref[a] = al_ref[u] * acc_ref[a] + _dot(vt_ref[tile], p_ref[u])

    def step(g, carry):
        t0 = g * u_jobs
        for u in range(u_jobs):
            stage_c(t0 + u, u)
        for u in range(u_jobs):
            stage_b(t0 + u_jobs + u, u)
        for u in range(u_jobs):
            stage_a(t0 + 2 * u_jobs + u, u)
        return carry

    prologue(0)

    def block(i, carry):
        prologue(jnp.minimum(i + 1, n_qb - 1))
        epilogue(jnp.maximum(i - EPILOGUE_LAG, 0), (i - EPILOGUE_LAG) % ACC_RING)
        lax.fori_loop(lo_ref[i], lo_ref[i + 1], step, 0)
        return carry

    lax.fori_loop(0, n_qb, block, 0)
    for qb in range(max(n_qb - EPILOGUE_LAG, 0), n_qb):
        epilogue(qb, qb % ACC_RING)


def _attention(q_t, gates_t, kcv, vc_t, ks_aug, kw_aug, vs_t, vw_t):
    bsz, hkv, s, kdim = ks_aug.shape
    dh = vs_t.shape[2]
    n_cmp = kcv.shape[2]
    gd = GQA_GROUP * dh
    nq = GQA_GROUP * ATT_Q
    n_qb = s // ATT_Q
    patterns = _mask_patterns()
    cmp_vis = _compressed_visibility(s)
    tables = _job_tables(s)
    per_head = lambda shape: pl.BlockSpec((1, 1) + shape, lambda b, h, *_: (b, h) + (0,) * len(shape))
    n_acc = 2 * ACC_RING + 1
    return pl.pallas_call(
        _attn_kernel,
        grid_spec=pltpu.PrefetchScalarGridSpec(
            num_scalar_prefetch=len(tables),
            grid=(bsz, hkv),
            in_specs=[
                per_head((n_qb, gd, ATT_Q)),
                per_head((n_qb, 3 * GQA_GROUP, ATT_Q)),
                per_head((n_cmp, kcv.shape[3])),
                per_head((dh, n_cmp)),
                per_head((s, kdim)),
                per_head((s, kdim)),
                per_head((dh, s)),
                per_head((dh, s)),
                pl.BlockSpec(patterns.shape, lambda b, h, *_: (0, 0, 0)),
                pl.BlockSpec(cmp_vis.shape, lambda b, h, *_: (0, 0)),
            ],
            out_specs=pl.BlockSpec((1, s, gd), lambda b, h, *_: (b, 0, h)),
            scratch_shapes=[
                pltpu.VMEM((2 * s, kdim), BF16),
                pltpu.VMEM((2 * (s // KEY_TILE), VT_ROWS, KEY_TILE), BF16),
                pltpu.VMEM((s // SEL_BLOCK, ATT_Q), F32),
                pltpu.VMEM((kdim - dh, ATT_Q), BF16),
                pltpu.VMEM((2, kdim, nq), BF16),
                pltpu.VMEM((ATT_UNROLL, KEY_TILE, nq), F32),
                pltpu.VMEM((ATT_UNROLL, KEY_TILE, nq), BF16),
                pltpu.VMEM((ATT_UNROLL, 1, nq), F32),
                pltpu.VMEM((ATT_UNROLL, 1, nq), F32),
                pltpu.VMEM((n_acc, 1, nq), F32),
                pltpu.VMEM((n_acc, VT_ROWS, nq), F32),
                pltpu.VMEM((ACC_RING, dh, nq), F32),
            ],
        ),
        out_shape=jax.ShapeDtypeStruct((bsz, s, hkv * gd), BF16),
        compiler_params=_params("parallel", "arbitrary"),
        name="nsa_attention",
    )(*tables, q_t, gates_t, kcv, vc_t, ks_aug, kw_aug, vs_t, vw_t, patterns, cmp_vis)


def _oproj_kernel(o_ref, x_ref, mod_ref, w_ref, g_ref, b_ref, out_ref):
    y = _dot(o_ref[0], w_ref[...])
    out_ref[0] = _layer_norm(ALPHA * x_ref[0] + mod_ref[0, 2:3, :] * y, g_ref[...], b_ref[...])


def _out_projection(o, x, mod, w_o, ln_g, ln_b):
    bsz, s, d = x.shape
    tm = 512
    return pl.pallas_call(
        _oproj_kernel,
        grid=(bsz, s // tm),
        in_specs=[
            pl.BlockSpec((1, tm, o.shape[-1]), lambda b, j: (b, j, 0)),
            pl.BlockSpec((1, tm, d), lambda b, j: (b, j, 0)),
            pl.BlockSpec((1, 6, d), lambda b, j: (b, 0, 0)),
            _const_spec(w_o.shape),
            _const_spec((1, d)),
            _const_spec((1, d)),
        ],
        out_specs=pl.BlockSpec((1, tm, d), lambda b, j: (b, j, 0)),
        out_shape=jax.ShapeDtypeStruct((bsz, s, d), F32),
        compiler_params=_params("parallel", "parallel"),
        name="nsa_out_proj",
    )(o, x, mod, w_o, ln_g.reshape(1, d), ln_b.reshape(1, d))


def _moe_block(x, mod, router_wt, router_bias, w_gu, w_down, layer, ln_g, ln_b):
    bsz, s, d = x.shape
    t = bsz * s
    xf = x.reshape(t, d)
    wcol, route, counts = _router(xf, mod, router_wt, router_bias, s)

    cnt = counts[:, 0]
    padded = (cnt + EXPERT_TILE - 1) // EXPERT_TILE * EXPERT_TILE
    ends = jnp.cumsum(padded)
    offs = ends - padded
    n_tiles = (TOP_K * t) // EXPERT_TILE + N_EXPERTS
    tile_start = jnp.arange(n_tiles, dtype=jnp.int32) * EXPERT_TILE
    tile_expert = jnp.minimum(jnp.sum(tile_start[:, None] >= ends[None, :], axis=1), N_EXPERTS - 1).astype(jnp.int32)
    n_used = (ends[-1:] // EXPERT_TILE).astype(jnp.int32)
    n_tok_tiles = t // TOKEN_TILE
    idx = route[:2 * TOP_K].reshape(2 * TOP_K, n_tok_tiles, TOKEN_TILE).transpose(1, 0, 2).reshape(n_tok_tiles, 1, -1)
    offs = offs.astype(jnp.int32)

    last_tile = jnp.where(cnt > 0, ends - EXPERT_TILE, -1)
    spare = n_used + jnp.arange(N_EXPERTS)
    spare = jnp.where(spare < n_tiles, spare * EXPERT_TILE, -1)
    xs = _dispatch(jnp.concatenate([last_tile, spare]).astype(jnp.int32), offs, idx, xf, mod, n_tiles * EXPERT_TILE, s)
    ys = _experts(tile_expert, n_used, xs, w_gu, w_down, layer)
    out = _combine(offs, idx, xf, mod, wcol, ys, ln_g, ln_b, s)
    return out.reshape(bsz, s, d)


def _nsa_layer(x, mod, w_kv, cmp_pe, cmp_w1, cmp_b1, cmp_w2, cmp_b2, w_qg, b_g, w_o, ln_g, ln_b):
    bsz, s, d = x.shape
    hkv, dh, grp = N_KV_HEADS, HEAD_DIM, GQA_GROUP
    hw = hkv * dh
    kvw = w_kv.reshape(d, 6, hw)
    kvc_w = jnp.stack([kvw[:, 0].reshape(d, hkv, dh), kvw[:, 1].reshape(d, hkv, dh)], axis=2).reshape(d, 2 * hw)
    w_nat = jnp.concatenate([kvc_w, kvw[:, 2], kvw[:, 4]], axis=1).astype(BF16)
    w_vt = jnp.concatenate([kvw[:, 3], kvw[:, 5]], axis=1).T.astype(BF16)
    nq = N_HEADS * dh
    wg = w_qg[:, nq:].reshape(d, hkv, grp, 3).transpose(0, 1, 3, 2).reshape(d, 3 * N_HEADS)
    w_qgt = jnp.concatenate([w_qg[:, :nq], wg], axis=1).T.astype(BF16)
    bg = b_g.reshape(hkv, grp, 3).transpose(0, 2, 1).reshape(3 * N_HEADS, 1)
    kvc, ks_aug, kw_aug, vs_t, vw_t, q_t, gates_t = _projections(x, mod, w_nat, w_vt, w_qgt, bg)
    cmp_kv, cmp_vt = _compress(kvc, cmp_pe, cmp_w1, cmp_b1, cmp_w2, cmp_b2)
    o = _attention(q_t, gates_t, cmp_kv, cmp_vt, ks_aug, kw_aug, vs_t, vw_t)
    return _out_projection(o, x, mod, w_o.astype(BF16), ln_g, ln_b)


def kernel(x, c, ada_w, ada_b, ln_g, ln_b, conv_w_in, conv_w, conv_b, conv_w_out, w_kv, cmp_pe, cmp_w1, cmp_b1, cmp_w2, cmp_b2, w_qg, b_g, w_o, router_w, router_bias, w_gu, w_down):
    bsz, s, d = x.shape
    mod = _modulation(c, ada_w, ada_b).reshape(DEPTH, bsz, 6, d)
    router_wt = router_w.T

    x = _conv_layer(x, mod[0], conv_w_in[0].astype(BF16), conv_w[0], conv_b[0], conv_w_out[0].astype(BF16),
                    ln_g[0, 0], ln_b[0, 0])
    x = _moe_block(x, mod[0], router_wt, router_bias, w_gu, w_down, 0, ln_g[0, 1], ln_b[0, 1])

    x = _nsa_layer(x, mod[1], w_kv, cmp_pe, cmp_w1, cmp_b1, cmp_w2, cmp_b2, w_qg[0], b_g[0], w_o[0],
                   ln_g[1, 0], ln_b[1, 0])
    x = _moe_block(x, mod[1], router_wt, router_bias, w_gu, w_down, 1, ln_g[1, 1], ln_b[1, 1])
    return x
```

```python
import functools
import math

import jax
import jax.numpy as jnp
import numpy as np
from jax import lax
from jax.experimental import pallas as pl
from jax.experimental.pallas import tpu as pltpu

F32 = jnp.float32
BF16 = jnp.bfloat16

D_MODEL = 1024
DEPTH = 2
N_A_LAYERS = DEPTH // 2
CONV_WIDTH = 3
N_HEADS = 16
HEAD_DIM = D_MODEL // N_HEADS
N_KV_HEADS = 4
GQA_GROUP = N_HEADS // N_KV_HEADS
CMP_BLOCK = 32
CMP_STRIDE = 16
CMP_HIDDEN = 4 * HEAD_DIM
SEL_BLOCK = 64
SEL_TOP = 16
WINDOW = 512
N_EXPERTS = 16
N_GROUPS = 4
EXPERTS_PER_GROUP = N_EXPERTS // N_GROUPS
TOP_K = 2
D_FF_EXPERT = D_MODEL // 2
ALPHA = (2 * DEPTH) ** 0.25
LN_EPS = 1e-5
NEG = -1e30
FORCE = 1e9

LANES = 128
VMEM_LIMIT_BYTES = 56 * 1024 * 1024

LOG2E = math.log2(math.e)

ATT_Q = 256
KEY_TILE = 256
PAT_ZERO, PAT_DIAG, PAT_ANTI, PAT_NONE = range(4)
ATT_UNROLL = 4
ACC_RING = 8
EPILOGUE_LAG = 3
RANK_BUCKET = 16
VT_ROWS = HEAD_DIM + 16


def _dot(a, b):
    return jnp.dot(a, b, preferred_element_type=F32)


def _dot_nt(a, b):
    return lax.dot_general(a, b, (((1,), (1,)), ((), ())), preferred_element_type=F32)


def _split(x):
    hi = x.astype(BF16)
    lo = (x - hi.astype(F32)).astype(BF16)
    return hi, lo


def _dot3(a, b):
    ah, al = _split(a)
    bh, bl = _split(b)
    return _dot(ah, bh) + (_dot(ah, bl) + _dot(al, bh))


def _dot3_nt(a, b):
    ah, al = _split(a)
    bh, bl = _split(b)
    return _dot_nt(ah, bh) + (_dot_nt(ah, bl) + _dot_nt(al, bh))


def _layer_norm(r, g, b):
    mu = jnp.mean(r, axis=-1, keepdims=True)
    d = r - mu
    var = jnp.mean(d * d, axis=-1, keepdims=True)
    return d * lax.rsqrt(var + LN_EPS) * g + b


def _params(*sem):
    return pltpu.CompilerParams(dimension_semantics=sem, vmem_limit_bytes=VMEM_LIMIT_BYTES)


def _const_spec(shape):
    zeros = (0,) * len(shape)
    return pl.BlockSpec(shape, lambda *_: zeros)


def _mod_kernel(c_ref, w_ref, b_ref, o_ref):
    c = c_ref[...]
    s = c * jax.nn.sigmoid(c)
    o_ref[0] = _dot3(s, w_ref[0]) + b_ref[0]


def _modulation(c, ada_w, ada_b):
    depth, d, n = ada_w.shape
    bsz = c.shape[0]
    tn = 1536
    return pl.pallas_call(
        _mod_kernel,
        grid=(depth, n // tn),
        in_specs=[
            pl.BlockSpec((bsz, d), lambda l, j: (0, 0)),
            pl.BlockSpec((1, d, tn), lambda l, j: (l, 0, j)),
            pl.BlockSpec((1, 1, tn), lambda l, j: (l, 0, j)),
        ],
        out_specs=pl.BlockSpec((1, bsz, tn), lambda l, j: (l, 0, j)),
        out_shape=jax.ShapeDtypeStruct((depth, bsz, n), F32),
        compiler_params=_params("parallel", "parallel"),
        name="adaln_mod",
    )(c, ada_w, ada_b.reshape(depth, 1, n))


CONV_HALO = 8


def _conv_kernel(x_ref, mod_ref, win_ref, cw_ref, cb_ref, wout_ref, g_ref, b_ref, o_ref, z_ref):
    tm = x_ref.shape[1]
    d = x_ref.shape[2]

    @pl.when(pl.program_id(1) == 0)
    def _():
        z_ref[0:CONV_HALO, :] = jnp.zeros((CONV_HALO, d), F32)

    x = x_ref[0]
    sh = mod_ref[0, 0:1, :]
    sc = mod_ref[0, 1:2, :]
    gate = mod_ref[0, 2:3, :]
    u = (x * (1.0 + sc) + sh).astype(BF16)
    bch = _dot(u, win_ref[...])
    z = bch[:, d:2 * d] * bch[:, 2 * d:]
    z_ref[CONV_HALO:CONV_HALO + tm, :] = z
    z1 = z_ref[CONV_HALO - 1:CONV_HALO - 1 + tm, :]
    z2 = z_ref[CONV_HALO - 2:CONV_HALO - 2 + tm, :]
    conv = cw_ref[0:1, :] * z2 + cw_ref[1:2, :] * z1 + cw_ref[2:3, :] * z + cb_ref[...]
    v = (bch[:, :d] * conv).astype(BF16)
    y = _dot(v, wout_ref[...])
    o_ref[0] = _layer_norm(ALPHA * x + gate * y, g_ref[...], b_ref[...])
    z_ref[0:CONV_HALO, :] = z_ref[tm:tm + CONV_HALO, :]


def _conv_layer(x, mod, w_in, conv_w, conv_b, w_out, ln_g, ln_b):
    bsz, s, d = x.shape
    tm = 512
    return pl.pallas_call(
        _conv_kernel,
        grid=(bsz, s // tm),
        in_specs=[
            pl.BlockSpec((1, tm, d), lambda b, j: (b, j, 0)),
            pl.BlockSpec((1, 6, d), lambda b, j: (b, 0, 0)),
            _const_spec((d, 3 * d)),
            _const_spec((CONV_WIDTH, d)),
            _const_spec((1, d)),
            _const_spec((d, d)),
            _const_spec((1, d)),
            _const_spec((1, d)),
        ],
        out_specs=pl.BlockSpec((1, tm, d), lambda b, j: (b, j, 0)),
        out_shape=jax.ShapeDtypeStruct((bsz, s, d), F32),
        scratch_shapes=[pltpu.VMEM((tm + CONV_HALO, d), F32)],
        compiler_params=_params("arbitrary", "arbitrary"),
        name="conv_mixer",
    )(x, mod, w_in, conv_w, conv_b.reshape(1, d), w_out, ln_g.reshape(1, d), ln_b.reshape(1, d))


def _router_kernel(x_ref, mod_ref, rwt_ref, rb_ref, wcol_ref, route_ref, count_ref, base_ref, tri_ref):
    tm = x_ref.shape[0]
    x = x_ref[...]
    u = x * (1.0 + mod_ref[0, 4:5, :]) + mod_ref[0, 3:4, :]
    logits = _dot3_nt(rwt_ref[...], u)
    m = jnp.max(logits, axis=0, keepdims=True)
    e = jnp.exp(logits - m)
    aff = e / jnp.sum(e, axis=0, keepdims=True)
    biased = aff + rb_ref[...]
    aff_r = [aff[i:i + 1, :] for i in range(N_EXPERTS)]
    row = [biased[i:i + 1, :] for i in range(N_EXPERTS)]

    best_s, best = None, None
    for g in range(N_GROUPS):
        r = row[g * EXPERTS_PER_GROUP:(g + 1) * EXPERTS_PER_GROUP]
        gs = None
        for i in range(EXPERTS_PER_GROUP):
            for j in range(i + 1, EXPERTS_PER_GROUP):
                p = r[i] + r[j]
                gs = p if gs is None else jnp.maximum(gs, p)
        if g == 0:
            best_s, best = gs, jnp.zeros((1, tm), jnp.int32)
        else:
            upd = gs > best_s
            best = jnp.where(upd, g, best)
            best_s = jnp.where(upd, gs, best_s)

    masked = [jnp.where(best == (i // EXPERTS_PER_GROUP), row[i], NEG) for i in range(N_EXPERTS)]

    def first_argmax(vals):
        v, idx = vals[0], jnp.zeros((1, tm), jnp.int32)
        for i in range(1, N_EXPERTS):
            upd = vals[i] > v
            idx = jnp.where(upd, i, idx)
            v = jnp.where(upd, vals[i], v)
        return idx

    i0 = first_argmax(masked)
    i1 = first_argmax([jnp.where(i0 == i, -jnp.inf, masked[i]) for i in range(N_EXPERTS)])
    w0 = sum(jnp.where(i0 == i, aff_r[i], 0.0) for i in range(N_EXPERTS))
    w1 = sum(jnp.where(i1 == i, aff_r[i], 0.0) for i in range(N_EXPERTS))
    tot = w0 + w1
    w0 = w0 / tot
    w1 = w1 / tot
    w_t = jnp.concatenate([w0, w1, jnp.zeros((LANES - TOP_K, tm), F32)], axis=0)
    wcol_ref[...] = w_t.T

    @pl.when(pl.program_id(0) == 0)
    def _():
        base_ref[...] = jnp.zeros(base_ref.shape, F32)
        r_i = lax.broadcasted_iota(jnp.int32, tri_ref.shape, 0)
        c_i = lax.broadcasted_iota(jnp.int32, tri_ref.shape, 1)
        tri_ref[...] = jnp.where(r_i <= c_i, 1.0, 0.0).astype(BF16)

    hot = jnp.concatenate([jnp.where((i0 == i) | (i1 == i), 1.0, 0.0) for i in range(N_EXPERTS)], axis=0)
    incl = _dot(hot.astype(BF16), tri_ref[...])
    pos = base_ref[...] + (incl - hot)
    rank0 = sum(jnp.where(i0 == i, pos[i:i + 1, :], 0.0) for i in range(N_EXPERTS))
    rank1 = sum(jnp.where(i1 == i, pos[i:i + 1, :], 0.0) for i in range(N_EXPERTS))
    base = base_ref[...] + jnp.sum(hot, axis=1, keepdims=True)
    base_ref[...] = base
    route_ref[...] = jnp.concatenate(
        [i0, i1, rank0.astype(jnp.int32), rank1.astype(jnp.int32), jnp.zeros((4, tm), jnp.int32)], axis=0)
    count_ref[...] = jnp.broadcast_to(base, count_ref.shape).astype(jnp.int32)


def _router(x, mod, router_wt, router_bias, seq):
    t, d = x.shape
    tm = 1024
    return pl.pallas_call(
        _router_kernel,
        grid=(t // tm,),
        in_specs=[
            pl.BlockSpec((tm, d), lambda i: (i, 0)),
            pl.BlockSpec((1, 6, d), lambda i: ((i * tm) // seq, 0, 0)),
            _const_spec((N_EXPERTS, d)),
            _const_spec((N_EXPERTS, 1)),
        ],
        out_specs=(
            pl.BlockSpec((tm, LANES), lambda i: (i, 0)),
            pl.BlockSpec((8, tm), lambda i: (0, i)),
            _const_spec((N_EXPERTS, LANES)),
        ),
        out_shape=(
            jax.ShapeDtypeStruct((t, LANES), F32),
            jax.ShapeDtypeStruct((8, t), jnp.int32),
            jax.ShapeDtypeStruct((N_EXPERTS, LANES), jnp.int32),
        ),
        scratch_shapes=[pltpu.VMEM((N_EXPERTS, 1), F32), pltpu.VMEM((tm, tm), BF16)],
        compiler_params=_params("arbitrary"),
        name="moe_router",
    )(x, mod, router_wt, router_bias.reshape(N_EXPERTS, 1))


EXPERT_TILE = 512
TOKEN_TILE = 512
ROW_COPY_UNROLL = 8
ROW_TILES = D_MODEL // LANES


def _store_rows(ref, x):
    rows = x.shape[0]
    for c in range(ROW_TILES):
        ref[pl.ds(c, rows, stride=ROW_TILES), :] = x[:, c * LANES:(c + 1) * LANES]


def _load_rows(ref, rows):
    return jnp.concatenate([ref[pl.ds(c, rows, stride=ROW_TILES), :] for c in range(ROW_TILES)], axis=1)


def _row(ref, r):
    return ref.at[pl.ds(pl.multiple_of(r * ROW_TILES, ROW_TILES), ROW_TILES)]


def _start_row_copies(tm, row_copy):
    def start(t, c):
        for k in range(TOP_K):
            row_copy(t, k).start(priority=k % 2)
        return c

    lax.fori_loop(0, tm, start, 0, unroll=ROW_COPY_UNROLL)


def _dispatch_kernel(pad_ref, idx_ref, x_ref, mod_ref, xs_ref, u_ref, sem):
    tm = x_ref.shape[0]

    i = pl.program_id(0)
    slot = i % 2

    @pl.when(i == 0)
    def _():
        u_ref[1] = jnp.zeros(u_ref.shape[1:], F32)
        for wait in (False, True):
            for e in range(pad_ref.shape[0]):
                @pl.when(pad_ref[e] >= 0)
                def _():
                    row = pl.multiple_of(jnp.maximum(pad_ref[e], 0) * ROW_TILES, tm * ROW_TILES)
                    fill = pltpu.make_async_copy(u_ref.at[1], xs_ref.at[pl.ds(row, tm * ROW_TILES)], sem.at[1])
                    fill.wait() if wait else fill.start()

    _store_rows(u_ref.at[slot], x_ref[...] * (1.0 + mod_ref[0, 4:5, :]) + mod_ref[0, 3:4, :])

    def row_copy(t, k):
        dst = idx_ref[0, 0, k * tm + t]
        return pltpu.make_async_copy(_row(u_ref.at[slot], t), _row(xs_ref, dst), sem.at[slot])

    _start_row_copies(tm, row_copy)

    def wait_tile(s):
        for _ in range(TOP_K):
            pltpu.make_async_copy(u_ref.at[s], xs_ref.at[pl.ds(0, tm * ROW_TILES)], sem.at[s]).wait()

    @pl.when(i > 0)
    def _():
        wait_tile(1 - slot)

    @pl.when(i == pl.num_programs(0) - 1)
    def _():
        wait_tile(slot)


def _dispatch(pad_start, idx, x, mod, n_rows, seq):
    t, d = x.shape
    tm = TOKEN_TILE
    assert tm == EXPERT_TILE
    return pl.pallas_call(
        _dispatch_kernel,
        grid_spec=pltpu.PrefetchScalarGridSpec(
            num_scalar_prefetch=1,
            grid=(t // tm,),
            in_specs=[
                pl.BlockSpec((1, 1, TOP_K * tm), lambda i, pad: (i, 0, 0), memory_space=pltpu.SMEM),
                pl.BlockSpec((tm, d), lambda i, pad: (i, 0)),
                pl.BlockSpec((1, 6, d), lambda i, pad: ((i * tm) // seq, 0, 0)),
            ],
            out_specs=pl.BlockSpec(memory_space=pl.ANY),
            scratch_shapes=[pltpu.VMEM((2, tm * ROW_TILES, LANES), F32), pltpu.SemaphoreType.DMA((2,))],
        ),
        out_shape=jax.ShapeDtypeStruct((n_rows * ROW_TILES, LANES), F32),
        compiler_params=_params("arbitrary"),
        name="moe_dispatch",
    )(pad_start, idx, x, mod)


def _expert_kernel(te_ref, first_ref, par_ref, nu_ref, xs_ref, wgu0_ref, wgu_next_ref, wdn_ref, ys_ref, wgu_bf, wdn_bf):
    del te_ref
    s = pl.program_id(0)
    used = s < nu_ref[0]
    nxt = jnp.minimum(s + 1, pl.num_programs(0) - 1)

    @pl.when(s == 0)
    def _():
        wgu_bf[par_ref[0]] = wgu0_ref[0, 0].astype(BF16)

    @pl.when((s + 1 < nu_ref[0]) & (first_ref[nxt] == 1))
    def _():
        wgu_bf[par_ref[nxt]] = wgu_next_ref[0, 0].astype(BF16)

    @pl.when(used & (first_ref[s] == 1))
    def _():
        wdn_bf[...] = wdn_ref[0, 0].astype(BF16)

    @pl.when(used)
    def _():
        f = wdn_bf.shape[0]
        h = _dot(_load_rows(xs_ref, EXPERT_TILE).astype(BF16), wgu_bf[par_ref[s]])
        gate = h[:, :f]
        a = (gate * jax.nn.sigmoid(gate) * h[:, f:]).astype(BF16)
        _store_rows(ys_ref, _dot(a, wdn_bf[...]))

    @pl.when(jnp.logical_not(used))
    def _():
        ys_ref[...] = jnp.zeros(ys_ref.shape, F32)


def _experts(tile_expert, n_used, xs, w_gu, w_down, layer):
    _, _, d, f2 = w_gu.shape
    f = f2 // 2
    n_tiles = tile_expert.shape[0]
    first = jnp.concatenate([jnp.ones((1,), jnp.int32), (tile_expert[1:] != tile_expert[:-1]).astype(jnp.int32)])
    parity = ((jnp.cumsum(first) - 1) % 2).astype(jnp.int32)
    row_map = lambda s, te, fi, pa, nu: (s, 0)
    in_row_map = lambda s, te, fi, pa, nu: (jnp.minimum(s, nu[0]), 0)
    return pl.pallas_call(
        _expert_kernel,
        grid_spec=pltpu.PrefetchScalarGridSpec(
            num_scalar_prefetch=4,
            grid=(n_tiles,),
            in_specs=[
                pl.BlockSpec((EXPERT_TILE * ROW_TILES, LANES), in_row_map),
                pl.BlockSpec((1, 1, d, f2), lambda s, te, fi, pa, nu: (layer, te[0], 0, 0)),
                pl.BlockSpec((1, 1, d, f2), lambda s, te, fi, pa, nu: (layer, te[jnp.minimum(s + 1, n_tiles - 1)], 0, 0)),
                pl.BlockSpec((1, 1, f, d), lambda s, te, fi, pa, nu: (layer, te[s], 0, 0)),
            ],
            out_specs=pl.BlockSpec((EXPERT_TILE * ROW_TILES, LANES), row_map),
            scratch_shapes=[pltpu.VMEM((2, d, f2), BF16), pltpu.VMEM((f, d), BF16)],
        ),
        out_shape=jax.ShapeDtypeStruct(xs.shape, F32),
        compiler_params=_params("arbitrary"),
        name="moe_experts",
    )(tile_expert, first, parity, n_used, xs, w_gu, w_gu, w_down)


def _combine_kernel(idx_ref, next_idx_ref, x_ref, mod_ref, w_ref, ys_ref, g_ref, b_ref, o_ref, y_ref, sem):
    tm = x_ref.shape[0]
    i = pl.program_id(0)
    slot = i % 2

    def gather(ids_ref, into):
        def row_copy(t, k):
            src = ids_ref[0, 0, k * tm + t]
            return pltpu.make_async_copy(_row(ys_ref, src), _row(y_ref.at[into, k], t), sem.at[into])

        _start_row_copies(tm, row_copy)

    @pl.when(i == 0)
    def _():
        gather(idx_ref, 0)

    for into in range(2):
        @pl.when((i + 1 < pl.num_programs(0)) & (slot == 1 - into))
        def _():
            gather(next_idx_ref, into)

    for k in range(TOP_K):
        pltpu.make_async_copy(ys_ref.at[pl.ds(0, tm * ROW_TILES)], y_ref.at[slot, k], sem.at[slot]).wait()
    w = w_ref[...]
    out = w[:, 0:1] * _load_rows(y_ref.at[slot, 0], tm) + w[:, 1:2] * _load_rows(y_ref.at[slot, 1], tm)
    r = ALPHA * x_ref[...] + mod_ref[0, 5:6, :] * out
    o_ref[...] = _layer_norm(r, g_ref[...], b_ref[...])


def _combine(idx, x, mod, wcol, ys, ln_g, ln_b, seq):
    t, d = x.shape
    tm = TOKEN_TILE
    n_tiles = t // tm
    return pl.pallas_call(
        _combine_kernel,
        grid=(n_tiles,),
        in_specs=[
            pl.BlockSpec((1, 1, TOP_K * tm), lambda i: (i, 0, 0), memory_space=pltpu.SMEM),
            pl.BlockSpec((1, 1, TOP_K * tm), lambda i: (jnp.minimum(i + 1, n_tiles - 1), 0, 0),
                         memory_space=pltpu.SMEM),
            pl.BlockSpec((tm, d), lambda i: (i, 0)),
            pl.BlockSpec((1, 6, d), lambda i: ((i * tm) // seq, 0, 0)),
            pl.BlockSpec((tm, LANES), lambda i: (i, 0)),
            pl.BlockSpec(memory_space=pl.ANY),
            _const_spec((1, d)),
            _const_spec((1, d)),
        ],
        out_specs=pl.BlockSpec((tm, d), lambda i: (i, 0)),
        out_shape=jax.ShapeDtypeStruct((t, d), F32),
        scratch_shapes=[pltpu.VMEM((2, TOP_K, tm * ROW_TILES, LANES), F32), pltpu.SemaphoreType.DMA((2,))],
        compiler_params=_params("arbitrary"),
        name="moe_combine",
    )(idx, idx, x, mod, wcol, ys, ln_g.reshape(1, d), ln_b.reshape(1, d))


def _proj_kernel(x_ref, mod_ref, wn_ref, wvt_ref, wqt_ref, bg_ref,
                 kvc_ref, ks_ref, kw_ref, vst_ref, vwt_ref, qt_ref, gt_ref):
    tm = x_ref.shape[1]
    j = pl.program_id(1)
    x = x_ref[0]
    xb = x.astype(BF16)
    nat = _dot(xb, wn_ref[...])
    vt = _dot_nt(wvt_ref[...], xb)
    u = (x * (1.0 + mod_ref[0, 1:2, :]) + mod_ref[0, 0:1, :]).astype(BF16)
    qg = _dot_nt(wqt_ref[...], u)
    nq = N_HEADS * HEAD_DIM
    q = (qg[:nq, :] * (HEAD_DIM ** -0.5 * LOG2E)).astype(BF16)
    gates = jax.nn.sigmoid(qg[nq:, :] + bg_ref[...])
    per = 3 * GQA_GROUP
    blk = (j * tm + lax.broadcasted_iota(jnp.int32, (tm, SEL_BLOCK), 0)) // SEL_BLOCK
    onehot = (blk == lax.broadcasted_iota(jnp.int32, (tm, SEL_BLOCK), 1)).astype(BF16)
    zeros = jnp.zeros((tm, SEL_BLOCK), BF16)
    hw = N_KV_HEADS * HEAD_DIM
    for h in range(N_KV_HEADS):
        c0 = h * HEAD_DIM
        kvc_ref[0, h] = nat[:, 2 * c0:2 * c0 + 2 * HEAD_DIM]
        ks = nat[:, 2 * hw + c0:2 * hw + c0 + HEAD_DIM].astype(BF16)
        ks_ref[0, h] = jnp.concatenate([ks, onehot], axis=1)
        kw = nat[:, 3 * hw + c0:3 * hw + c0 + HEAD_DIM].astype(BF16)
        kw_ref[0, h] = jnp.concatenate([kw, zeros], axis=1)
        vst_ref[0, h] = vt[c0:c0 + HEAD_DIM, :].astype(BF16)
        vwt_ref[0, h] = vt[hw + c0:hw + c0 + HEAD_DIM, :].astype(BF16)
        for p in range(tm // ATT_Q):
            cols = slice(p * ATT_Q, (p + 1) * ATT_Q)
            qt_ref[0, h, p] = q[h * GQA_GROUP * HEAD_DIM:(h + 1) * GQA_GROUP * HEAD_DIM, cols]
            gt_ref[0, h, p] = gates[h * per:(h + 1) * per, cols]


def _projections(x, mod, w_nat, w_vt, w_qgt, b_g):
    bsz, s, d = x.shape
    tm = 512
    hkv, dh = N_KV_HEADS, HEAD_DIM
    per = 3 * GQA_GROUP
    nqg = w_qgt.shape[0]
    out_shape = (
        jax.ShapeDtypeStruct((bsz, hkv, s, 2 * dh), F32),
        jax.ShapeDtypeStruct((bsz, hkv, s, 2 * dh), BF16),
        jax.ShapeDtypeStruct((bsz, hkv, s, 2 * dh), BF16),
        jax.ShapeDtypeStruct((bsz, hkv, dh, s), BF16),
        jax.ShapeDtypeStruct((bsz, hkv, dh, s), BF16),
        jax.ShapeDtypeStruct((bsz, hkv, s // ATT_Q, GQA_GROUP * dh, ATT_Q), BF16),
        jax.ShapeDtypeStruct((bsz, hkv, s // ATT_Q, per, ATT_Q), F32),
    )
    out_specs = (
        pl.BlockSpec((1, hkv, tm, 2 * dh), lambda b, j: (b, 0, j, 0)),
        pl.BlockSpec((1, hkv, tm, 2 * dh), lambda b, j: (b, 0, j, 0)),
        pl.BlockSpec((1, hkv, tm, 2 * dh), lambda b, j: (b, 0, j, 0)),
        pl.BlockSpec((1, hkv, dh, tm), lambda b, j: (b, 0, 0, j)),
        pl.BlockSpec((1, hkv, dh, tm), lambda b, j: (b, 0, 0, j)),
        pl.BlockSpec((1, hkv, tm // ATT_Q, GQA_GROUP * dh, ATT_Q), lambda b, j: (b, 0, j, 0, 0)),
        pl.BlockSpec((1, hkv, tm // ATT_Q, per, ATT_Q), lambda b, j: (b, 0, j, 0, 0)),
    )
    return pl.pallas_call(
        _proj_kernel,
        grid=(bsz, s // tm),
        in_specs=[
            pl.BlockSpec((1, tm, d), lambda b, j: (b, j, 0)),
            pl.BlockSpec((1, 6, d), lambda b, j: (b, 0, 0)),
            _const_spec(w_nat.shape),
            _const_spec(w_vt.shape),
            _const_spec(w_qgt.shape),
            _const_spec((nqg - N_HEADS * dh, 1)),
        ],
        out_specs=out_specs,
        out_shape=out_shape,
        compiler_params=_params("parallel", "parallel"),
        name="nsa_projections",
    )(x, mod, w_nat, w_vt, w_qgt, b_g)


def _compress_kernel(kvc_ref, pe_ref, w1_ref, b1_ref, w2_ref, b2_ref, nat_ref, tr_ref):
    n = kvc_ref.shape[2] // CMP_STRIDE
    hid2 = w1_ref.shape[-1]
    p = jnp.zeros((n, hid2), F32)
    q = jnp.zeros((n, hid2), F32)
    for l in range(CMP_STRIDE):
        x = kvc_ref[0, 0, pl.ds(l, n, stride=CMP_STRIDE), :]
        p = p + _dot((x + pe_ref[l:l + 1, :]).astype(BF16), w1_ref[0, l])
        q = q + _dot((x + pe_ref[CMP_STRIDE + l:CMP_STRIDE + l + 1, :]).astype(BF16), w1_ref[1, l])
    pre = p + pltpu.roll(q, n - 1, 0) + b1_ref[...]
    hdn = 0.5 * pre * (1.0 + jnp.tanh(0.7978845608028654 * (pre + 0.044715 * (pre * pre * pre))))
    out = _dot(hdn.astype(BF16), w2_ref[...]) + b2_ref[...]
    nat_ref[0, 0] = out.astype(BF16)
    tr_ref[0, 0] = out[:, HEAD_DIM:].T.astype(BF16)


def _compress(kvc, cmp_pe, cmp_w1, cmp_b1, cmp_w2, cmp_b2):
    bsz, hkv, s, two_dh = kvc.shape
    dh = two_dh // 2
    n = s // CMP_STRIDE
    hid = cmp_w1.shape[-1]
    zw = jnp.zeros((2, CMP_STRIDE, dh, hid), F32)
    w1 = cmp_w1.reshape(2, 2, CMP_STRIDE, dh, hid)
    w1 = jnp.concatenate([jnp.concatenate([w1[0], zw], axis=-1), jnp.concatenate([zw, w1[1]], axis=-1)], axis=-2)
    zd = jnp.zeros((hid, dh), F32)
    w2 = jnp.concatenate([jnp.concatenate([cmp_w2[0], zd], axis=1), jnp.concatenate([zd, cmp_w2[1]], axis=1)], axis=0)
    pe = jnp.concatenate([cmp_pe[0], cmp_pe[1]], axis=1)
    return pl.pallas_call(
        _compress_kernel,
        grid=(bsz, hkv),
        in_specs=[
            pl.BlockSpec((1, 1, s, two_dh), lambda b, h: (b, h, 0, 0)),
            _const_spec(pe.shape),
            _const_spec(w1.shape),
            _const_spec((1, 2 * hid)),
            _const_spec(w2.shape),
            _const_spec((1, two_dh)),
        ],
        out_specs=(
            pl.BlockSpec((1, 1, n, two_dh), lambda b, h: (b, h, 0, 0)),
            pl.BlockSpec((1, 1, dh, n), lambda b, h: (b, h, 0, 0)),
        ),
        out_shape=(
            jax.ShapeDtypeStruct((bsz, hkv, n, two_dh), BF16),
            jax.ShapeDtypeStruct((bsz, hkv, dh, n), BF16),
        ),
        compiler_params=_params("parallel", "parallel"),
        name="nsa_compress",
    )(kvc, pe, w1.astype(BF16), cmp_b1.reshape(1, 2 * hid), w2.astype(BF16), cmp_b2.reshape(1, two_dh))


def _mask_patterns():
    assert WINDOW == 2 * KEY_TILE and ATT_Q == KEY_TILE
    keyl = np.arange(KEY_TILE)[:, None]
    ql = np.arange(ATT_Q)[None, :]
    true = np.ones((KEY_TILE, ATT_Q), bool)
    valid = np.stack([
        true,
        keyl <= ql,
        keyl > ql,
        ~true,
    ])
    return jnp.asarray(np.where(valid, 0.0, NEG), F32)


def _compressed_visibility(seq):
    n_cmp = seq // CMP_STRIDE
    r = np.arange(-n_cmp, n_cmp)[:, None]
    ql = np.arange(ATT_Q)[None, :]
    return jnp.asarray(np.where(r * CMP_STRIDE + CMP_BLOCK - 1 <= ql, 0.0, NEG), F32)


def _job_tables(seq):
    nqb, nkt, u = seq // ATT_Q, seq // KEY_TILE, ATT_UNROLL
    tile, pat, acc, qs, base = [], [], [], [], []
    for i in range(nqb):
        base.append(len(tile))
        for j in range(i + 1):
            tile.append(j)
            pat.append(PAT_DIAG if j == i else PAT_ZERO)
            acc.append(2 * (i % ACC_RING))
            qs.append(i % 2)
        for w in range(min(i, WINDOW // KEY_TILE) + 1):
            tile.append(nkt + i - w)
            pat.append((PAT_DIAG, PAT_ZERO, PAT_ANTI)[w])
            acc.append(2 * (i % ACC_RING) + 1)
            qs.append(i % 2)
    n_real = len(tile)
    n_steps = (n_real - 1 + 2 * u) // u + 1
    pad = lambda x, fill: np.array([fill] * (2 * u) + x + [fill] * (n_steps * u - n_real), np.int32)
    step_lo = np.array([-(-b // u) for b in base] + [n_steps], np.int32)
    assert all(base[i + 2] >= step_lo[i + 1] * u for i in range(nqb - 2))
    return pad(tile, 0), pad(pat, PAT_NONE), pad(acc, 2 * ACC_RING), pad(qs, 0), step_lo


def _attn_kernel(jt_ref, jp_ref, ja_ref, jq_ref, lo_ref,
                 q_ref, g_ref, kcv_ref, vct_ref, ks_ref, kw_ref, vs_ref, vw_ref, pat_ref, cvis_ref, o_ref,
                 kall_ref, vt_ref, sc_ref, bias_ref, qaug_ref, s_ref, p_ref, al_ref, mt_ref, m_ref, acc_ref, oc_ref):
    seq = ks_ref.shape[2]
    nq = GQA_GROUP * ATT_Q
    dh = HEAD_DIM
    n_key_tiles = seq // KEY_TILE
    n_qb = seq // ATT_Q
    u_jobs = ATT_UNROLL

    kall_ref[0:seq, :] = ks_ref[0, 0]
    kall_ref[seq:2 * seq, :] = kw_ref[0, 0]
    ones_rows = (lax.broadcasted_iota(jnp.int32, (VT_ROWS - dh, KEY_TILE), 0) == 0).astype(BF16)
    for n in range(n_key_tiles):
        vt_ref[n, 0:dh, :] = vs_ref[0, 0, :, n * KEY_TILE:(n + 1) * KEY_TILE]
        vt_ref[n_key_tiles + n, 0:dh, :] = vw_ref[0, 0, :, n * KEY_TILE:(n + 1) * KEY_TILE]
        vt_ref[n, dh:VT_ROWS, :] = ones_rows
        vt_ref[n_key_tiles + n, dh:VT_ROWS, :] = ones_rows

    s_ref[...] = jnp.full(s_ref.shape, NEG, F32)
    mt_ref[...] = jnp.full(mt_ref.shape, NEG, F32)
    p_ref[...] = jnp.zeros(p_ref.shape, BF16)
    al_ref[...] = jnp.ones(al_ref.shape, F32)
    m_ref[...] = jnp.full(m_ref.shape, NEG, F32)
    acc_ref[...] = jnp.ones(acc_ref.shape, F32)
    oc_ref[...] = jnp.zeros(oc_ref.shape, F32)

    def prologue(qb):
        slot = qb % ACC_RING
        qa = q_ref[0, 0, qb]
        qt = jnp.concatenate([qa[g * dh:(g + 1) * dh, :] for g in range(GQA_GROUP)], axis=1)
        t_row = qb * ATT_Q + lax.broadcasted_iota(jnp.int32, (1, ATT_Q), 1)
        t4 = jnp.concatenate([t_row] * GQA_GROUP, axis=1)

        n_cmp = kcv_ref.shape[2]
        q_c = jnp.concatenate([qt, jnp.zeros((kcv_ref.shape[3] - dh, nq), BF16)], axis=0)
        per_qb = ATT_Q // CMP_STRIDE
        vis = cvis_ref[pl.ds(pl.multiple_of(n_cmp - per_qb * qb, per_qb), n_cmp), :]
        s_c = _dot(kcv_ref[0, 0], q_c) + jnp.concatenate([vis] * GQA_GROUP, axis=1)
        m_c = jnp.max(s_c, axis=0, keepdims=True)
        m_c = jnp.where(t4 >= CMP_BLOCK - 1, m_c, 0.0)
        e_c = jnp.exp2(s_c - m_c)
        p_c = e_c / jnp.maximum(jnp.sum(e_c, axis=0, keepdims=True), 1e-30)
        oc_ref[slot] = _dot(vct_ref[0, 0], p_c.astype(BF16))

        n_sel = seq // SEL_BLOCK
        p_sum = sum(p_c[:, g * ATT_Q:(g + 1) * ATT_Q] for g in range(GQA_GROUP))
        jj = lax.broadcasted_iota(jnp.int32, (n_sel, n_cmp), 0) * SEL_BLOCK
        nn = lax.broadcasted_iota(jnp.int32, (n_sel, n_cmp), 1) * CMP_STRIDE
        ov = jnp.minimum(nn + CMP_BLOCK, jj + SEL_BLOCK) - jnp.maximum(nn, jj)
        ov_t = (jnp.maximum(ov, 0).astype(F32) * (1.0 / CMP_BLOCK)).astype(BF16)
        p_hi, p_lo = _split(p_sum)
        imp = _dot(ov_t, p_hi) + _dot(ov_t, p_lo)
        jb = lax.broadcasted_iota(jnp.int32, (n_sel, 1), 0)
        cur = t_row // SEL_BLOCK
        allowed = jb <= cur
        forced = (jb == 0) | (jb == cur) | (jb == cur - 1)
        score = jnp.where(forced & allowed, FORCE, jnp.where(allowed, imp, NEG))
        sc_ref[...] = score
        n_top = min(SEL_TOP, n_sel)

        def selection_bias(n_blk):
            sub = 8
            groups = [sc_ref[r:r + sub, :] for r in range(0, n_blk, sub)]
            ranks = [jnp.zeros((sub, ATT_Q), F32) for _ in groups]
            for k in range(n_blk):
                row = jnp.broadcast_to(sc_ref[k:k + 1, :], (sub, ATT_Q))
                for r, grp in enumerate(groups):
                    if r * sub > k:
                        ahead = row >= grp
                    elif r * sub + sub - 1 < k:
                        ahead = row > grp
                    else:
                        later = (r * sub + lax.broadcasted_iota(jnp.int32, (sub, 1), 0)) > k
                        ahead = (row > grp) | ((row == grp) & later)
                    ranks[r] = ranks[r] + jnp.where(ahead, 1.0, 0.0)
            chosen = (jnp.concatenate(ranks, axis=0) < n_top) & (jnp.concatenate(groups, axis=0) > 0.5 * NEG)
            bias = jnp.where(chosen, 0.0, NEG).astype(BF16)
            rest = bias_ref.shape[0] - n_blk
            return jnp.concatenate([bias, jnp.full((rest, ATT_Q), NEG, BF16)], axis=0) if rest else bias

        n_buckets = -(-n_sel // RANK_BUCKET)
        bucket = jnp.minimum((qb + 1) * (ATT_Q // SEL_BLOCK) - 1, n_sel - 1) // RANK_BUCKET
        for b in range(n_buckets):
            @pl.when(bucket == b)
            def _():
                bias_ref[...] = selection_bias(min((b + 1) * RANK_BUCKET, n_sel))
        qaug_ref[qb % 2] = jnp.concatenate([qt, jnp.concatenate([bias_ref[...]] * GQA_GROUP, axis=1)], axis=0)

        for br in range(2):
            m_ref[2 * slot + br] = jnp.full((1, nq), NEG, F32)
            acc_ref[2 * slot + br] = jnp.zeros((VT_ROWS, nq), F32)

    def epilogue(qb, slot):
        a_s, a_w = acc_ref[2 * slot], acc_ref[2 * slot + 1]
        o_s = a_s[:dh] / a_s[dh:dh + 1]
        o_w = a_w[:dh] / a_w[dh:dh + 1]
        gates = g_ref[0, 0, qb]

        def gate_row(br):
            return jnp.concatenate([gates[br * GQA_GROUP + g:br * GQA_GROUP + g + 1, :] for g in range(GQA_GROUP)],
                                   axis=1)
        o = gate_row(0) * oc_ref[slot] + gate_row(1) * o_s + gate_row(2) * o_w
        rows = pl.ds(pl.multiple_of(qb * ATT_Q, ATT_Q), ATT_Q)
        o_ref[0, rows, :] = jnp.concatenate(
            [o[:, g * ATT_Q:(g + 1) * ATT_Q].T for g in range(GQA_GROUP)], axis=1).astype(BF16)

    def stage_a(t, u):
        tile, pat, qs = jt_ref[t], jp_ref[t], jq_ref[t]
        k = kall_ref[pl.ds(pl.multiple_of(tile * KEY_TILE, KEY_TILE), KEY_TILE), :]
        s = _dot(k, qaug_ref[qs]) + jnp.concatenate([pat_ref[pat]] * GQA_GROUP, axis=1)
        s_ref[u] = s
        mt_ref[u] = jnp.max(s, axis=0, keepdims=True)

    def stage_b(t, u):
        a = ja_ref[t]
        m_old = m_ref[a]
        m_new = jnp.maximum(m_old, mt_ref[u])
        alpha = jnp.exp2(m_old - m_new)
        m_ref[a] = m_new
        al_ref[u] = alpha
        p_ref[u] = jnp.exp2(s_ref[u] - m_new).astype(BF16)

    def stage_c(t, u):
        tile, a = jt_ref[t], ja_ref[t]
        acc_ref[a] = al_ref[u] * acc_ref[a] + _dot(vt_ref[tile], p_ref[u])

    def step(g, carry):
        t0 = g * u_jobs
        for u in range(u_jobs):
            stage_c(t0 + u, u)
        for u in range(u_jobs):
            stage_b(t0 + u_jobs + u, u)
        for u in range(u_jobs):
            stage_a(t0 + 2 * u_jobs + u, u)
        return carry

    prologue(0)

    def block(i, carry):
        prologue(i + 1)
        epilogue(jnp.maximum(i - EPILOGUE_LAG, 0), (i - EPILOGUE_LAG) % ACC_RING)
        lax.fori_loop(lo_ref[i], lo_ref[i + 1], step, 0)
        return carry

    lax.fori_loop(0, n_qb - 1, block, 0)
    last = n_qb - 1
    epilogue(max(last - EPILOGUE_LAG, 0), (last - EPILOGUE_LAG) % ACC_RING)
    lax.fori_loop(lo_ref[last], lo_ref[last + 1], step, 0)
    for qb in range(max(n_qb - EPILOGUE_LAG, 0), n_qb):
        epilogue(qb, qb % ACC_RING)


def _attention(q_t, gates_t, kcv, vc_t, ks_aug, kw_aug, vs_t, vw_t):
    bsz, hkv, s, kdim = ks_aug.shape
    dh = vs_t.shape[2]
    n_cmp = kcv.shape[2]
    gd = GQA_GROUP * dh
    nq = GQA_GROUP * ATT_Q
    n_qb = s // ATT_Q
    patterns = _mask_patterns()
    cmp_vis = _compressed_visibility(s)
    tables = _job_tables(s)
    per_head = lambda shape: pl.BlockSpec((1, 1) + shape, lambda b, h, *_: (b, h) + (0,) * len(shape))
    n_acc = 2 * ACC_RING + 1
    return pl.pallas_call(
        _attn_kernel,
        grid_spec=pltpu.PrefetchScalarGridSpec(
            num_scalar_prefetch=len(tables),
            grid=(bsz, hkv),
            in_specs=[
                per_head((n_qb, gd, ATT_Q)),
                per_head((n_qb, 3 * GQA_GROUP, ATT_Q)),
                per_head((n_cmp, kcv.shape[3])),
                per_head((dh, n_cmp)),
                per_head((s, kdim)),
                per_head((s, kdim)),
                per_head((dh, s)),
                per_head((dh, s)),
                pl.BlockSpec(patterns.shape, lambda b, h, *_: (0, 0, 0)),
                pl.BlockSpec(cmp_vis.shape, lambda b, h, *_: (0, 0)),
            ],
            out_specs=pl.BlockSpec((1, s, gd), lambda b, h, *_: (b, 0, h)),
            scratch_shapes=[
                pltpu.VMEM((2 * s, kdim), BF16),
                pltpu.VMEM((2 * (s // KEY_TILE), VT_ROWS, KEY_TILE), BF16),
                pltpu.VMEM((s // SEL_BLOCK, ATT_Q), F32),
                pltpu.VMEM((kdim - dh, ATT_Q), BF16),
                pltpu.VMEM((2, kdim, nq), BF16),
                pltpu.VMEM((ATT_UNROLL, KEY_TILE, nq), F32),
                pltpu.VMEM((ATT_UNROLL, KEY_TILE, nq), BF16),
                pltpu.VMEM((ATT_UNROLL, 1, nq), F32),
                pltpu.VMEM((ATT_UNROLL, 1, nq), F32),
                pltpu.VMEM((n_acc, 1, nq), F32),
                pltpu.VMEM((n_acc, VT_ROWS, nq), F32),
                pltpu.VMEM((ACC_RING, dh, nq), F32),
            ],
        ),
        out_shape=jax.ShapeDtypeStruct((bsz, s, hkv * gd), BF16),
        compiler_params=_params("parallel", "arbitrary"),
        name="nsa_attention",
    )(*tables, q_t, gates_t, kcv, vc_t, ks_aug, kw_aug, vs_t, vw_t, patterns, cmp_vis)


def _oproj_kernel(o_ref, x_ref, mod_ref, w_ref, g_ref, b_ref, out_ref):
    y = _dot(o_ref[0], w_ref[...])
    out_ref[0] = _layer_norm(ALPHA * x_ref[0] + mod_ref[0, 2:3, :] * y, g_ref[...], b_ref[...])


def _out_projection(o, x, mod, w_o, ln_g, ln_b):
    bsz, s, d = x.shape
    tm = 512
    return pl.pallas_call(
        _oproj_kernel,
        grid=(bsz, s // tm),
        in_specs=[
            pl.BlockSpec((1, tm, o.shape[-1]), lambda b, j: (b, j, 0)),
            pl.BlockSpec((1, tm, d), lambda b, j: (b, j, 0)),
            pl.BlockSpec((1, 6, d), lambda b, j: (b, 0, 0)),
            _const_spec(w_o.shape),
            _const_spec((1, d)),
            _const_spec((1, d)),
        ],
        out_specs=pl.BlockSpec((1, tm, d), lambda b, j: (b, j, 0)),
        out_shape=jax.ShapeDtypeStruct((bsz, s, d), F32),
        compiler_params=_params("parallel", "parallel"),
        name="nsa_out_proj",
    )(o, x, mod, w_o, ln_g.reshape(1, d), ln_b.reshape(1, d))


def _moe_block(x, mod, router_wt, router_bias, w_gu, w_down, layer, ln_g, ln_b):
    bsz, s, d = x.shape
    t = bsz * s
    xf = x.reshape(t, d)
    wcol, route, counts = _router(xf, mod, router_wt, router_bias, s)

    cnt = counts[:, 0]
    padded = (cnt + EXPERT_TILE - 1) // EXPERT_TILE * EXPERT_TILE
    ends = jnp.cumsum(padded)
    offs = ends - padded
    n_tiles = (TOP_K * t) // EXPERT_TILE + N_EXPERTS
    tile_start = jnp.arange(n_tiles, dtype=jnp.int32) * EXPERT_TILE
    tile_expert = jnp.minimum(jnp.sum(tile_start[:, None] >= ends[None, :], axis=1), N_EXPERTS - 1).astype(jnp.int32)
    n_used = (ends[-1:] // EXPERT_TILE).astype(jnp.int32)
    experts, ranks = route[:TOP_K], route[TOP_K:2 * TOP_K]
    dst = ranks + sum(jnp.where(experts == e, offs[e], 0) for e in range(N_EXPERTS))
    idx = dst.reshape(TOP_K, t // TOKEN_TILE, TOKEN_TILE).transpose(1, 0, 2).reshape(t // TOKEN_TILE, 1, TOP_K * TOKEN_TILE)

    last_tile = jnp.where(cnt > 0, ends - EXPERT_TILE, -1)
    spare = n_used + jnp.arange(N_EXPERTS)
    spare = jnp.where(spare < n_tiles, spare * EXPERT_TILE, -1)
    xs = _dispatch(jnp.concatenate([last_tile, spare]).astype(jnp.int32), idx, xf, mod, n_tiles * EXPERT_TILE, s)
    ys = _experts(tile_expert, n_used, xs, w_gu, w_down, layer)
    out = _combine(idx, xf, mod, wcol, ys, ln_g, ln_b, s)
    return out.reshape(bsz, s, d)


def _nsa_layer(x, mod, w_kv, cmp_pe, cmp_w1, cmp_b1, cmp_w2, cmp_b2, w_qg, b_g, w_o, ln_g, ln_b):
    bsz, s, d = x.shape
    hkv, dh, grp = N_KV_HEADS, HEAD_DIM, GQA_GROUP
    hw = hkv * dh
    kvw = w_kv.reshape(d, 6, hw)
    kvc_w = jnp.stack([kvw[:, 0].reshape(d, hkv, dh), kvw[:, 1].reshape(d, hkv, dh)], axis=2).reshape(d, 2 * hw)
    w_nat = jnp.concatenate([kvc_w, kvw[:, 2], kvw[:, 4]], axis=1).astype(BF16)
    w_vt = jnp.concatenate([kvw[:, 3], kvw[:, 5]], axis=1).T.astype(BF16)
    nq = N_HEADS * dh
    wg = w_qg[:, nq:].reshape(d, hkv, grp, 3).transpose(0, 1, 3, 2).reshape(d, 3 * N_HEADS)
    w_qgt = jnp.concatenate([w_qg[:, :nq], wg], axis=1).T.astype(BF16)
    bg = b_g.reshape(hkv, grp, 3).transpose(0, 2, 1).reshape(3 * N_HEADS, 1)
    kvc, ks_aug, kw_aug, vs_t, vw_t, q_t, gates_t = _projections(x, mod, w_nat, w_vt, w_qgt, bg)
    cmp_kv, cmp_vt = _compress(kvc, cmp_pe, cmp_w1, cmp_b1, cmp_w2, cmp_b2)
    o = _attention(q_t, gates_t, cmp_kv, cmp_vt, ks_aug, kw_aug, vs_t, vw_t)
    return _out_projection(o, x, mod, w_o.astype(BF16), ln_g, ln_b)


def kernel(x, c, ada_w, ada_b, ln_g, ln_b, conv_w_in, conv_w, conv_b, conv_w_out, w_kv, cmp_pe, cmp_w1, cmp_b1, cmp_w2, cmp_b2, w_qg, b_g, w_o, router_w, router_bias, w_gu, w_down):
    bsz, s, d = x.shape
    mod = _modulation(c, ada_w, ada_b).reshape(DEPTH, bsz, 6, d)
    router_wt = router_w.T

    x = _conv_layer(x, mod[0], conv_w_in[0].astype(BF16), conv_w[0], conv_b[0], conv_w_out[0].astype(BF16),
                    ln_g[0, 0], ln_b[0, 0])
    x = _moe_block(x, mod[0], router_wt, router_bias, w_gu, w_down, 0, ln_g[0, 1], ln_b[0, 1])

    x = _nsa_layer(x, mod[1], w_kv, cmp_pe, cmp_w1, cmp_b1, cmp_w2, cmp_b2, w_qg[0], b_g[0], w_o[0],
                   ln_g[1, 0], ln_b[1, 0])
    x = _moe_block(x, mod[1], router_wt, router_bias, w_gu, w_down, 1, ln_g[1, 1], ln_b[1, 1])
    return x
```

```python
import functools
import math

import jax
import jax.numpy as jnp
import numpy as np
from jax import lax
from jax.experimental import pallas as pl
from jax.experimental.pallas import tpu as pltpu

F32 = jnp.float32
BF16 = jnp.bfloat16

D_MODEL = 1024
DEPTH = 2
N_A_LAYERS = DEPTH // 2
CONV_WIDTH = 3
N_HEADS = 16
HEAD_DIM = D_MODEL // N_HEADS
N_KV_HEADS = 4
GQA_GROUP = N_HEADS // N_KV_HEADS
CMP_BLOCK = 32
CMP_STRIDE = 16
CMP_HIDDEN = 4 * HEAD_DIM
SEL_BLOCK = 64
SEL_TOP = 16
WINDOW = 512
N_EXPERTS = 16
N_GROUPS = 4
EXPERTS_PER_GROUP = N_EXPERTS // N_GROUPS
TOP_K = 2
D_FF_EXPERT = D_MODEL // 2
ALPHA = (2 * DEPTH) ** 0.25
LN_EPS = 1e-5
NEG = -1e30
FORCE = 1e9

LANES = 128
VMEM_LIMIT_BYTES = 56 * 1024 * 1024

LOG2E = math.log2(math.e)

ATT_Q = 256
KEY_TILE = 256
PAT_ZERO, PAT_DIAG, PAT_ANTI, PAT_NONE = range(4)
ATT_UNROLL = 4
ACC_RING = 8
EPILOGUE_LAG = 3
RANK_BUCKET = 16
VT_ROWS = HEAD_DIM + 16


def _dot(a, b):
    return jnp.dot(a, b, preferred_element_type=F32)


def _dot_nt(a, b):
    return lax.dot_general(a, b, (((1,), (1,)), ((), ())), preferred_element_type=F32)


def _split(x):
    hi = x.astype(BF16)
    lo = (x - hi.astype(F32)).astype(BF16)
    return hi, lo


def _dot3(a, b):
    ah, al = _split(a)
    bh, bl = _split(b)
    return _dot(ah, bh) + (_dot(ah, bl) + _dot(al, bh))


def _dot3_nt(a, b):
    ah, al = _split(a)
    bh, bl = _split(b)
    return _dot_nt(ah, bh) + (_dot_nt(ah, bl) + _dot_nt(al, bh))


def _layer_norm(r, g, b):
    mu = jnp.mean(r, axis=-1, keepdims=True)
    d = r - mu
    var = jnp.mean(d * d, axis=-1, keepdims=True)
    return d * lax.rsqrt(var + LN_EPS) * g + b


def _params(*sem):
    return pltpu.CompilerParams(dimension_semantics=sem, vmem_limit_bytes=VMEM_LIMIT_BYTES)


def _const_spec(shape):
    zeros = (0,) * len(shape)
    return pl.BlockSpec(shape, lambda *_: zeros)


def _mod_kernel(c_ref, w_ref, b_ref, o_ref):
    c = c_ref[...]
    s = c * jax.nn.sigmoid(c)
    o_ref[0] = _dot3(s, w_ref[0]) + b_ref[0]


def _modulation(c, ada_w, ada_b):
    depth, d, n = ada_w.shape
    bsz = c.shape[0]
    tn = 1536
    return pl.pallas_call(
        _mod_kernel,
        grid=(depth, n // tn),
        in_specs=[
            pl.BlockSpec((bsz, d), lambda l, j: (0, 0)),
            pl.BlockSpec((1, d, tn), lambda l, j: (l, 0, j)),
            pl.BlockSpec((1, 1, tn), lambda l, j: (l, 0, j)),
        ],
        out_specs=pl.BlockSpec((1, bsz, tn), lambda l, j: (l, 0, j)),
        out_shape=jax.ShapeDtypeStruct((depth, bsz, n), F32),
        compiler_params=_params("parallel", "parallel"),
        name="adaln_mod",
    )(c, ada_w, ada_b.reshape(depth, 1, n))


CONV_HALO = 8


def _conv_kernel(x_ref, mod_ref, win_ref, cw_ref, cb_ref, wout_ref, g_ref, b_ref, o_ref, z_ref):
    tm = x_ref.shape[1]
    d = x_ref.shape[2]

    @pl.when(pl.program_id(1) == 0)
    def _():
        z_ref[0:CONV_HALO, :] = jnp.zeros((CONV_HALO, d), F32)

    x = x_ref[0]
    sh = mod_ref[0, 0:1, :]
    sc = mod_ref[0, 1:2, :]
    gate = mod_ref[0, 2:3, :]
    u = (x * (1.0 + sc) + sh).astype(BF16)
    bch = _dot(u, win_ref[...])
    z = bch[:, d:2 * d] * bch[:, 2 * d:]
    z_ref[CONV_HALO:CONV_HALO + tm, :] = z
    z1 = z_ref[CONV_HALO - 1:CONV_HALO - 1 + tm, :]
    z2 = z_ref[CONV_HALO - 2:CONV_HALO - 2 + tm, :]
    conv = cw_ref[0:1, :] * z2 + cw_ref[1:2, :] * z1 + cw_ref[2:3, :] * z + cb_ref[...]
    v = (bch[:, :d] * conv).astype(BF16)
    y = _dot(v, wout_ref[...])
    o_ref[0] = _layer_norm(ALPHA * x + gate * y, g_ref[...], b_ref[...])
    z_ref[0:CONV_HALO, :] = z_ref[tm:tm + CONV_HALO, :]


def _conv_layer(x, mod, w_in, conv_w, conv_b, w_out, ln_g, ln_b):
    bsz, s, d = x.shape
    tm = 512
    return pl.pallas_call(
        _conv_kernel,
        grid=(bsz, s // tm),
        in_specs=[
            pl.BlockSpec((1, tm, d), lambda b, j: (b, j, 0)),
            pl.BlockSpec((1, 6, d), lambda b, j: (b, 0, 0)),
            _const_spec((d, 3 * d)),
            _const_spec((CONV_WIDTH, d)),
            _const_spec((1, d)),
            _const_spec((d, d)),
            _const_spec((1, d)),
            _const_spec((1, d)),
        ],
        out_specs=pl.BlockSpec((1, tm, d), lambda b, j: (b, j, 0)),
        out_shape=jax.ShapeDtypeStruct((bsz, s, d), F32),
        scratch_shapes=[pltpu.VMEM((tm + CONV_HALO, d), F32)],
        compiler_params=_params("arbitrary", "arbitrary"),
        name="conv_mixer",
    )(x, mod, w_in, conv_w, conv_b.reshape(1, d), w_out, ln_g.reshape(1, d), ln_b.reshape(1, d))


def _router_kernel(x_ref, mod_ref, rwt_ref, rb_ref, wcol_ref, route_ref, count_ref, base_ref, tri_ref):
    tm = x_ref.shape[0]
    x = x_ref[...]
    u = x * (1.0 + mod_ref[0, 4:5, :]) + mod_ref[0, 3:4, :]
    wh, wl = _split(rwt_ref[...])
    uh, ul = _split(u)
    both = _dot_nt(jnp.concatenate([wh, wl], axis=0), uh)
    logits = both[:N_EXPERTS] + (_dot_nt(wh, ul) + both[N_EXPERTS:])
    m = jnp.max(logits, axis=0, keepdims=True)
    e = jnp.exp(logits - m)
    aff = e / jnp.sum(e, axis=0, keepdims=True)
    biased = aff + rb_ref[...]
    aff_r = [aff[i:i + 1, :] for i in range(N_EXPERTS)]
    row = [biased[i:i + 1, :] for i in range(N_EXPERTS)]

    best_s, best = None, None
    for g in range(N_GROUPS):
        r = row[g * EXPERTS_PER_GROUP:(g + 1) * EXPERTS_PER_GROUP]
        gs = None
        for i in range(EXPERTS_PER_GROUP):
            for j in range(i + 1, EXPERTS_PER_GROUP):
                p = r[i] + r[j]
                gs = p if gs is None else jnp.maximum(gs, p)
        if g == 0:
            best_s, best = gs, jnp.zeros((1, tm), jnp.int32)
        else:
            upd = gs > best_s
            best = jnp.where(upd, g, best)
            best_s = jnp.where(upd, gs, best_s)

    masked = [jnp.where(best == (i // EXPERTS_PER_GROUP), row[i], NEG) for i in range(N_EXPERTS)]

    def first_argmax(vals):
        v, idx = vals[0], jnp.zeros((1, tm), jnp.int32)
        for i in range(1, N_EXPERTS):
            upd = vals[i] > v
            idx = jnp.where(upd, i, idx)
            v = jnp.where(upd, vals[i], v)
        return idx

    i0 = first_argmax(masked)
    i1 = first_argmax([jnp.where(i0 == i, -jnp.inf, masked[i]) for i in range(N_EXPERTS)])
    w0 = sum(jnp.where(i0 == i, aff_r[i], 0.0) for i in range(N_EXPERTS))
    w1 = sum(jnp.where(i1 == i, aff_r[i], 0.0) for i in range(N_EXPERTS))
    tot = w0 + w1
    w0 = w0 / tot
    w1 = w1 / tot
    w_t = jnp.concatenate([w0, w1, jnp.zeros((LANES - TOP_K, tm), F32)], axis=0)
    wcol_ref[...] = w_t.T

    @pl.when(pl.program_id(0) == 0)
    def _():
        base_ref[...] = jnp.zeros(base_ref.shape, F32)
        r_i = lax.broadcasted_iota(jnp.int32, tri_ref.shape, 0)
        c_i = lax.broadcasted_iota(jnp.int32, tri_ref.shape, 1)
        tri_ref[...] = jnp.where(r_i <= c_i, 1.0, 0.0).astype(BF16)

    hot = jnp.concatenate([jnp.where((i0 == i) | (i1 == i), 1.0, 0.0) for i in range(N_EXPERTS)], axis=0)
    incl = _dot(hot.astype(BF16), tri_ref[...])
    pos = base_ref[...] + (incl - hot)
    rank0 = sum(jnp.where(i0 == i, pos[i:i + 1, :], 0.0) for i in range(N_EXPERTS))
    rank1 = sum(jnp.where(i1 == i, pos[i:i + 1, :], 0.0) for i in range(N_EXPERTS))
    base = base_ref[...] + jnp.sum(hot, axis=1, keepdims=True)
    base_ref[...] = base
    route_ref[...] = jnp.concatenate(
        [i0, i1, rank0.astype(jnp.int32), rank1.astype(jnp.int32), jnp.zeros((4, tm), jnp.int32)], axis=0)
    count_ref[...] = jnp.broadcast_to(base, count_ref.shape).astype(jnp.int32)


def _router(x, mod, router_wt, router_bias, seq):
    t, d = x.shape
    tm = 1024
    return pl.pallas_call(
        _router_kernel,
        grid=(t // tm,),
        in_specs=[
            pl.BlockSpec((tm, d), lambda i: (i, 0)),
            pl.BlockSpec((1, 6, d), lambda i: ((i * tm) // seq, 0, 0)),
            _const_spec((N_EXPERTS, d)),
            _const_spec((N_EXPERTS, 1)),
        ],
        out_specs=(
            pl.BlockSpec((tm, LANES), lambda i: (i, 0)),
            pl.BlockSpec((8, tm), lambda i: (0, i)),
            _const_spec((N_EXPERTS, LANES)),
        ),
        out_shape=(
            jax.ShapeDtypeStruct((t, LANES), F32),
            jax.ShapeDtypeStruct((8, t), jnp.int32),
            jax.ShapeDtypeStruct((N_EXPERTS, LANES), jnp.int32),
        ),
        scratch_shapes=[pltpu.VMEM((N_EXPERTS, 1), F32), pltpu.VMEM((tm, tm), BF16)],
        compiler_params=_params("arbitrary"),
        name="moe_router",
    )(x, mod, router_wt, router_bias.reshape(N_EXPERTS, 1))


EXPERT_TILE = 512
TOKEN_TILE = 512
ROW_COPY_UNROLL = 8
ROW_TILES = D_MODEL // LANES


def _store_rows(ref, x):
    rows = x.shape[0]
    for c in range(ROW_TILES):
        ref[pl.ds(c, rows, stride=ROW_TILES), :] = x[:, c * LANES:(c + 1) * LANES]


def _load_rows(ref, rows):
    return jnp.concatenate([ref[pl.ds(c, rows, stride=ROW_TILES), :] for c in range(ROW_TILES)], axis=1)


def _row(ref, r):
    return ref.at[pl.ds(pl.multiple_of(r * ROW_TILES, ROW_TILES), ROW_TILES)]


def _start_row_copies(tm, row_copy):
    def start(t, c):
        for k in range(TOP_K):
            row_copy(t, k).start(priority=k % 2)
        return c

    lax.fori_loop(0, tm, start, 0, unroll=ROW_COPY_UNROLL)


def _dispatch_kernel(pad_ref, idx_ref, x_ref, mod_ref, xs_ref, u_ref, sem):
    tm = x_ref.shape[0]

    i = pl.program_id(0)
    slot = i % 2

    @pl.when(i == 0)
    def _():
        u_ref[1] = jnp.zeros(u_ref.shape[1:], F32)
        for wait in (False, True):
            for e in range(pad_ref.shape[0]):
                @pl.when(pad_ref[e] >= 0)
                def _():
                    row = pl.multiple_of(jnp.maximum(pad_ref[e], 0) * ROW_TILES, tm * ROW_TILES)
                    fill = pltpu.make_async_copy(u_ref.at[1], xs_ref.at[pl.ds(row, tm * ROW_TILES)], sem.at[1])
                    fill.wait() if wait else fill.start()

    _store_rows(u_ref.at[slot], x_ref[...] * (1.0 + mod_ref[0, 4:5, :]) + mod_ref[0, 3:4, :])

    def row_copy(t, k):
        dst = idx_ref[0, 0, k * tm + t]
        return pltpu.make_async_copy(_row(u_ref.at[slot], t), _row(xs_ref, dst), sem.at[slot])

    _start_row_copies(tm, row_copy)

    def wait_tile(s):
        for _ in range(TOP_K):
            pltpu.make_async_copy(u_ref.at[s], xs_ref.at[pl.ds(0, tm * ROW_TILES)], sem.at[s]).wait()

    @pl.when(i > 0)
    def _():
        wait_tile(1 - slot)

    @pl.when(i == pl.num_programs(0) - 1)
    def _():
        wait_tile(slot)


def _dispatch(pad_start, idx, x, mod, n_rows, seq):
    t, d = x.shape
    tm = TOKEN_TILE
    assert tm == EXPERT_TILE
    return pl.pallas_call(
        _dispatch_kernel,
        grid_spec=pltpu.PrefetchScalarGridSpec(
            num_scalar_prefetch=1,
            grid=(t // tm,),
            in_specs=[
                pl.BlockSpec((1, 1, TOP_K * tm), lambda i, pad: (i, 0, 0), memory_space=pltpu.SMEM),
                pl.BlockSpec((tm, d), lambda i, pad: (i, 0)),
                pl.BlockSpec((1, 6, d), lambda i, pad: ((i * tm) // seq, 0, 0)),
            ],
            out_specs=pl.BlockSpec(memory_space=pl.ANY),
            scratch_shapes=[pltpu.VMEM((2, tm * ROW_TILES, LANES), F32), pltpu.SemaphoreType.DMA((2,))],
        ),
        out_shape=jax.ShapeDtypeStruct((n_rows * ROW_TILES, LANES), F32),
        compiler_params=_params("arbitrary"),
        name="moe_dispatch",
    )(pad_start, idx, x, mod)


def _expert_kernel(te_ref, first_ref, par_ref, nu_ref, xs_ref, wgu0_ref, wgu_next_ref, wdn_ref, ys_ref, wgu_bf, wdn_bf):
    del te_ref
    s = pl.program_id(0)
    used = s < nu_ref[0]
    nxt = jnp.minimum(s + 1, pl.num_programs(0) - 1)

    @pl.when(s == 0)
    def _():
        wgu_bf[par_ref[0]] = wgu0_ref[0, 0].astype(BF16)

    @pl.when((s + 1 < nu_ref[0]) & (first_ref[nxt] == 1))
    def _():
        wgu_bf[par_ref[nxt]] = wgu_next_ref[0, 0].astype(BF16)

    @pl.when(used & (first_ref[s] == 1))
    def _():
        wdn_bf[...] = wdn_ref[0, 0].astype(BF16)

    @pl.when(used)
    def _():
        f = wdn_bf.shape[0]
        h = _dot(_load_rows(xs_ref, EXPERT_TILE).astype(BF16), wgu_bf[par_ref[s]])
        gate = h[:, :f]
        a = (gate * jax.nn.sigmoid(gate) * h[:, f:]).astype(BF16)
        _store_rows(ys_ref, _dot(a, wdn_bf[...]))

    @pl.when(jnp.logical_not(used))
    def _():
        ys_ref[...] = jnp.zeros(ys_ref.shape, F32)


def _experts(tile_expert, n_used, xs, w_gu, w_down, layer):
    _, _, d, f2 = w_gu.shape
    f = f2 // 2
    n_tiles = tile_expert.shape[0]
    first = jnp.concatenate([jnp.ones((1,), jnp.int32), (tile_expert[1:] != tile_expert[:-1]).astype(jnp.int32)])
    parity = ((jnp.cumsum(first) - 1) % 2).astype(jnp.int32)
    row_map = lambda s, te, fi, pa, nu: (s, 0)
    in_row_map = lambda s, te, fi, pa, nu: (jnp.minimum(s, nu[0]), 0)
    return pl.pallas_call(
        _expert_kernel,
        grid_spec=pltpu.PrefetchScalarGridSpec(
            num_scalar_prefetch=4,
            grid=(n_tiles,),
            in_specs=[
                pl.BlockSpec((EXPERT_TILE * ROW_TILES, LANES), in_row_map),
                pl.BlockSpec((1, 1, d, f2), lambda s, te, fi, pa, nu: (layer, te[0], 0, 0)),
                pl.BlockSpec((1, 1, d, f2), lambda s, te, fi, pa, nu: (layer, te[jnp.minimum(s + 1, n_tiles - 1)], 0, 0)),
                pl.BlockSpec((1, 1, f, d), lambda s, te, fi, pa, nu: (layer, te[s], 0, 0)),
            ],
            out_specs=pl.BlockSpec((EXPERT_TILE * ROW_TILES, LANES), row_map),
            scratch_shapes=[pltpu.VMEM((2, d, f2), BF16), pltpu.VMEM((f, d), BF16)],
        ),
        out_shape=jax.ShapeDtypeStruct(xs.shape, F32),
        compiler_params=_params("arbitrary"),
        name="moe_experts",
    )(tile_expert, first, parity, n_used, xs, w_gu, w_gu, w_down)


def _combine_kernel(idx_ref, next_idx_ref, x_ref, mod_ref, w_ref, ys_ref, g_ref, b_ref, o_ref, y_ref, sem):
    tm = x_ref.shape[0]
    i = pl.program_id(0)
    slot = i % 2

    def gather(ids_ref, into):
        def row_copy(t, k):
            src = ids_ref[0, 0, k * tm + t]
            return pltpu.make_async_copy(_row(ys_ref, src), _row(y_ref.at[into, k], t), sem.at[into])

        _start_row_copies(tm, row_copy)

    @pl.when(i == 0)
    def _():
        gather(idx_ref, 0)

    for into in range(2):
        @pl.when((i + 1 < pl.num_programs(0)) & (slot == 1 - into))
        def _():
            gather(next_idx_ref, into)

    for k in range(TOP_K):
        pltpu.make_async_copy(ys_ref.at[pl.ds(0, tm * ROW_TILES)], y_ref.at[slot, k], sem.at[slot]).wait()
    w = w_ref[...]
    out = w[:, 0:1] * _load_rows(y_ref.at[slot, 0], tm) + w[:, 1:2] * _load_rows(y_ref.at[slot, 1], tm)
    r = ALPHA * x_ref[...] + mod_ref[0, 5:6, :] * out
    o_ref[...] = _layer_norm(r, g_ref[...], b_ref[...])


def _combine(idx, x, mod, wcol, ys, ln_g, ln_b, seq):
    t, d = x.shape
    tm = TOKEN_TILE
    n_tiles = t // tm
    return pl.pallas_call(
        _combine_kernel,
        grid=(n_tiles,),
        in_specs=[
            pl.BlockSpec((1, 1, TOP_K * tm), lambda i: (i, 0, 0), memory_space=pltpu.SMEM),
            pl.BlockSpec((1, 1, TOP_K * tm), lambda i: (jnp.minimum(i + 1, n_tiles - 1), 0, 0),
                         memory_space=pltpu.SMEM),
            pl.BlockSpec((tm, d), lambda i: (i, 0)),
            pl.BlockSpec((1, 6, d), lambda i: ((i * tm) // seq, 0, 0)),
            pl.BlockSpec((tm, LANES), lambda i: (i, 0)),
            pl.BlockSpec(memory_space=pl.ANY),
            _const_spec((1, d)),
            _const_spec((1, d)),
        ],
        out_specs=pl.BlockSpec((tm, d), lambda i: (i, 0)),
        out_shape=jax.ShapeDtypeStruct((t, d), F32),
        scratch_shapes=[pltpu.VMEM((2, TOP_K, tm * ROW_TILES, LANES), F32), pltpu.SemaphoreType.DMA((2,))],
        compiler_params=_params("arbitrary"),
        name="moe_combine",
    )(idx, idx, x, mod, wcol, ys, ln_g.reshape(1, d), ln_b.reshape(1, d))


def _proj_kernel(x_ref, mod_ref, wn_ref, wvt_ref, wqt_ref, bg_ref,
                 kvc_ref, ks_ref, kw_ref, vst_ref, vwt_ref, qt_ref, gt_ref):
    tm = x_ref.shape[1]
    j = pl.program_id(1)
    x = x_ref[0]
    xb = x.astype(BF16)
    nat = _dot(xb, wn_ref[...])
    vt = _dot_nt(wvt_ref[...], xb)
    u = (x * (1.0 + mod_ref[0, 1:2, :]) + mod_ref[0, 0:1, :]).astype(BF16)
    qg = _dot_nt(wqt_ref[...], u)
    nq = N_HEADS * HEAD_DIM
    q = (qg[:nq, :] * (HEAD_DIM ** -0.5 * LOG2E)).astype(BF16)
    gates = jax.nn.sigmoid(qg[nq:, :] + bg_ref[...])
    per = 3 * GQA_GROUP
    blk = (j * tm + lax.broadcasted_iota(jnp.int32, (tm, SEL_BLOCK), 0)) // SEL_BLOCK
    onehot = (blk == lax.broadcasted_iota(jnp.int32, (tm, SEL_BLOCK), 1)).astype(BF16)
    zeros = jnp.zeros((tm, SEL_BLOCK), BF16)
    hw = N_KV_HEADS * HEAD_DIM
    for h in range(N_KV_HEADS):
        c0 = h * HEAD_DIM
        kvc_ref[0, h] = nat[:, 2 * c0:2 * c0 + 2 * HEAD_DIM]
        ks = nat[:, 2 * hw + c0:2 * hw + c0 + HEAD_DIM].astype(BF16)
        ks_ref[0, h] = jnp.concatenate([ks, onehot], axis=1)
        kw = nat[:, 3 * hw + c0:3 * hw + c0 + HEAD_DIM].astype(BF16)
        kw_ref[0, h] = jnp.concatenate([kw, zeros], axis=1)
        vst_ref[0, h] = vt[c0:c0 + HEAD_DIM, :].astype(BF16)
        vwt_ref[0, h] = vt[hw + c0:hw + c0 + HEAD_DIM, :].astype(BF16)
        for p in range(tm // ATT_Q):
            cols = slice(p * ATT_Q, (p + 1) * ATT_Q)
            qt_ref[0, h, p] = q[h * GQA_GROUP * HEAD_DIM:(h + 1) * GQA_GROUP * HEAD_DIM, cols]
            gt_ref[0, h, p] = gates[h * per:(h + 1) * per, cols]


def _projections(x, mod, w_nat, w_vt, w_qgt, b_g):
    bsz, s, d = x.shape
    tm = 512
    hkv, dh = N_KV_HEADS, HEAD_DIM
    per = 3 * GQA_GROUP
    nqg = w_qgt.shape[0]
    out_shape = (
        jax.ShapeDtypeStruct((bsz, hkv, s, 2 * dh), F32),
        jax.ShapeDtypeStruct((bsz, hkv, s, 2 * dh), BF16),
        jax.ShapeDtypeStruct((bsz, hkv, s, 2 * dh), BF16),
        jax.ShapeDtypeStruct((bsz, hkv, dh, s), BF16),
        jax.ShapeDtypeStruct((bsz, hkv, dh, s), BF16),
        jax.ShapeDtypeStruct((bsz, hkv, s // ATT_Q, GQA_GROUP * dh, ATT_Q), BF16),
        jax.ShapeDtypeStruct((bsz, hkv, s // ATT_Q, per, ATT_Q), F32),
    )
    out_specs = (
        pl.BlockSpec((1, hkv, tm, 2 * dh), lambda b, j: (b, 0, j, 0)),
        pl.BlockSpec((1, hkv, tm, 2 * dh), lambda b, j: (b, 0, j, 0)),
        pl.BlockSpec((1, hkv, tm, 2 * dh), lambda b, j: (b, 0, j, 0)),
        pl.BlockSpec((1, hkv, dh, tm), lambda b, j: (b, 0, 0, j)),
        pl.BlockSpec((1, hkv, dh, tm), lambda b, j: (b, 0, 0, j)),
        pl.BlockSpec((1, hkv, tm // ATT_Q, GQA_GROUP * dh, ATT_Q), lambda b, j: (b, 0, j, 0, 0)),
        pl.BlockSpec((1, hkv, tm // ATT_Q, per, ATT_Q), lambda b, j: (b, 0, j, 0, 0)),
    )
    return pl.pallas_call(
        _proj_kernel,
        grid=(bsz, s // tm),
        in_specs=[
            pl.BlockSpec((1, tm, d), lambda b, j: (b, j, 0)),
            pl.BlockSpec((1, 6, d), lambda b, j: (b, 0, 0)),
            _const_spec(w_nat.shape),
            _const_spec(w_vt.shape),
            _const_spec(w_qgt.shape),
            _const_spec((nqg - N_HEADS * dh, 1)),
        ],
        out_specs=out_specs,
        out_shape=out_shape,
        compiler_params=_params("parallel", "parallel"),
        name="nsa_projections",
    )(x, mod, w_nat, w_vt, w_qgt, b_g)


def _compress_kernel(kvc_ref, pe_ref, w1_ref, b1_ref, w2_ref, b2_ref, nat_ref, tr_ref):
    n = kvc_ref.shape[2] // CMP_STRIDE
    hid2 = w1_ref.shape[-1]
    p = jnp.zeros((n, hid2), F32)
    q = jnp.zeros((n, hid2), F32)
    for l in range(CMP_STRIDE):
        x = kvc_ref[0, 0, pl.ds(l, n, stride=CMP_STRIDE), :]
        p = p + _dot((x + pe_ref[l:l + 1, :]).astype(BF16), w1_ref[0, l])
        q = q + _dot((x + pe_ref[CMP_STRIDE + l:CMP_STRIDE + l + 1, :]).astype(BF16), w1_ref[1, l])
    pre = p + pltpu.roll(q, n - 1, 0) + b1_ref[...]
    hdn = 0.5 * pre * (1.0 + jnp.tanh(0.7978845608028654 * (pre + 0.044715 * (pre * pre * pre))))
    out = _dot(hdn.astype(BF16), w2_ref[...]) + b2_ref[...]
    nat_ref[0, 0] = out.astype(BF16)
    tr_ref[0, 0] = out[:, HEAD_DIM:].T.astype(BF16)


def _compress(kvc, cmp_pe, cmp_w1, cmp_b1, cmp_w2, cmp_b2):
    bsz, hkv, s, two_dh = kvc.shape
    dh = two_dh // 2
    n = s // CMP_STRIDE
    hid = cmp_w1.shape[-1]
    zw = jnp.zeros((2, CMP_STRIDE, dh, hid), F32)
    w1 = cmp_w1.reshape(2, 2, CMP_STRIDE, dh, hid)
    w1 = jnp.concatenate([jnp.concatenate([w1[0], zw], axis=-1), jnp.concatenate([zw, w1[1]], axis=-1)], axis=-2)
    zd = jnp.zeros((hid, dh), F32)
    w2 = jnp.concatenate([jnp.concatenate([cmp_w2[0], zd], axis=1), jnp.concatenate([zd, cmp_w2[1]], axis=1)], axis=0)
    pe = jnp.concatenate([cmp_pe[0], cmp_pe[1]], axis=1)
    return pl.pallas_call(
        _compress_kernel,
        grid=(bsz, hkv),
        in_specs=[
            pl.BlockSpec((1, 1, s, two_dh), lambda b, h: (b, h, 0, 0)),
            _const_spec(pe.shape),
            _const_spec(w1.shape),
            _const_spec((1, 2 * hid)),
            _const_spec(w2.shape),
            _const_spec((1, two_dh)),
        ],
        out_specs=(
            pl.BlockSpec((1, 1, n, two_dh), lambda b, h: (b, h, 0, 0)),
            pl.BlockSpec((1, 1, dh, n), lambda b, h: (b, h, 0, 0)),
        ),
        out_shape=(
            jax.ShapeDtypeStruct((bsz, hkv, n, two_dh), BF16),
            jax.ShapeDtypeStruct((bsz, hkv, dh, n), BF16),
        ),
        compiler_params=_params("parallel", "parallel"),
        name="nsa_compress",
    )(kvc, pe, w1.astype(BF16), cmp_b1.reshape(1, 2 * hid), w2.astype(BF16), cmp_b2.reshape(1, two_dh))


def _mask_patterns():
    assert WINDOW == 2 * KEY_TILE and ATT_Q == KEY_TILE
    keyl = np.arange(KEY_TILE)[:, None]
    ql = np.arange(ATT_Q)[None, :]
    true = np.ones((KEY_TILE, ATT_Q), bool)
    valid = np.stack([
        true,
        keyl <= ql,
        keyl > ql,
        ~true,
    ])
    return jnp.asarray(np.where(valid, 0.0, NEG), F32)


def _compressed_visibility(seq):
    n_cmp = seq // CMP_STRIDE
    r = np.arange(-n_cmp, n_cmp)[:, None]
    ql = np.arange(ATT_Q)[None, :]
    return jnp.asarray(np.where(r * CMP_STRIDE + CMP_BLOCK - 1 <= ql, 0.0, NEG), F32)


def _job_tables(seq):
    nqb, nkt, u = seq // ATT_Q, seq // KEY_TILE, ATT_UNROLL
    tile, pat, acc, qs, base = [], [], [], [], []
    for i in range(nqb):
        base.append(len(tile))
        for j in range(i + 1):
            tile.append(j)
            pat.append(PAT_DIAG if j == i else PAT_ZERO)
            acc.append(2 * (i % ACC_RING))
            qs.append(i % 2)
        for w in range(min(i, WINDOW // KEY_TILE) + 1):
            tile.append(nkt + i - w)
            pat.append((PAT_DIAG, PAT_ZERO, PAT_ANTI)[w])
            acc.append(2 * (i % ACC_RING) + 1)
            qs.append(i % 2)
    n_real = len(tile)
    n_steps = (n_real - 1 + 2 * u) // u + 1
    pad = lambda x, fill: np.array([fill] * (2 * u) + x + [fill] * (n_steps * u - n_real), np.int32)
    step_lo = np.array([-(-b // u) for b in base] + [n_steps], np.int32)
    assert all(base[i + 2] >= step_lo[i + 1] * u for i in range(nqb - 2))
    return pad(tile, 0), pad(pat, PAT_NONE), pad(acc, 2 * ACC_RING), pad(qs, 0), step_lo


def _attn_kernel(jt_ref, jp_ref, ja_ref, jq_ref, lo_ref,
                 q_ref, g_ref, kcv_ref, vct_ref, ks_ref, kw_ref, vs_ref, vw_ref, pat_ref, cvis_ref, o_ref,
                 kall_ref, vt_ref, sc_ref, bias_ref, qaug_ref, s_ref, p_ref, al_ref, mt_ref, m_ref, acc_ref, oc_ref):
    seq = ks_ref.shape[2]
    nq = GQA_GROUP * ATT_Q
    dh = HEAD_DIM
    n_key_tiles = seq // KEY_TILE
    n_qb = seq // ATT_Q
    u_jobs = ATT_UNROLL

    kall_ref[0:seq, :] = ks_ref[0, 0]
    kall_ref[seq:2 * seq, :] = kw_ref[0, 0]
    ones_rows = (lax.broadcasted_iota(jnp.int32, (VT_ROWS - dh, KEY_TILE), 0) == 0).astype(BF16)
    for n in range(n_key_tiles):
        vt_ref[n, 0:dh, :] = vs_ref[0, 0, :, n * KEY_TILE:(n + 1) * KEY_TILE]
        vt_ref[n_key_tiles + n, 0:dh, :] = vw_ref[0, 0, :, n * KEY_TILE:(n + 1) * KEY_TILE]
        vt_ref[n, dh:VT_ROWS, :] = ones_rows
        vt_ref[n_key_tiles + n, dh:VT_ROWS, :] = ones_rows

    s_ref[...] = jnp.full(s_ref.shape, NEG, F32)
    mt_ref[...] = jnp.full(mt_ref.shape, NEG, F32)
    p_ref[...] = jnp.zeros(p_ref.shape, BF16)
    al_ref[...] = jnp.ones(al_ref.shape, F32)
    m_ref[...] = jnp.full(m_ref.shape, NEG, F32)
    acc_ref[...] = jnp.ones(acc_ref.shape, F32)
    oc_ref[...] = jnp.zeros(oc_ref.shape, F32)

    def prologue(qb):
        slot = qb % ACC_RING
        qa = q_ref[0, 0, qb]
        qt = jnp.concatenate([qa[g * dh:(g + 1) * dh, :] for g in range(GQA_GROUP)], axis=1)
        t_row = qb * ATT_Q + lax.broadcasted_iota(jnp.int32, (1, ATT_Q), 1)
        t4 = jnp.concatenate([t_row] * GQA_GROUP, axis=1)

        n_cmp = kcv_ref.shape[2]
        q_c = jnp.concatenate([qt, jnp.zeros((kcv_ref.shape[3] - dh, nq), BF16)], axis=0)
        per_qb = ATT_Q // CMP_STRIDE
        vis = cvis_ref[pl.ds(pl.multiple_of(n_cmp - per_qb * qb, per_qb), n_cmp), :]
        s_c = _dot(kcv_ref[0, 0], q_c) + jnp.concatenate([vis] * GQA_GROUP, axis=1)
        m_c = jnp.max(s_c, axis=0, keepdims=True)
        m_c = jnp.where(t4 >= CMP_BLOCK - 1, m_c, 0.0)
        e_c = jnp.exp2(s_c - m_c)
        p_c = e_c / jnp.maximum(jnp.sum(e_c, axis=0, keepdims=True), 1e-30)
        oc_ref[slot] = _dot(vct_ref[0, 0], p_c.astype(BF16))

        n_sel = seq // SEL_BLOCK
        p_sum = sum(p_c[:, g * ATT_Q:(g + 1) * ATT_Q] for g in range(GQA_GROUP))
        jj = lax.broadcasted_iota(jnp.int32, (n_sel, n_cmp), 0) * SEL_BLOCK
        nn = lax.broadcasted_iota(jnp.int32, (n_sel, n_cmp), 1) * CMP_STRIDE
        ov = jnp.minimum(nn + CMP_BLOCK, jj + SEL_BLOCK) - jnp.maximum(nn, jj)
        ov_t = (jnp.maximum(ov, 0).astype(F32) * (1.0 / CMP_BLOCK)).astype(BF16)
        p_hi, p_lo = _split(p_sum)
        imp = _dot(ov_t, p_hi) + _dot(ov_t, p_lo)
        jb = lax.broadcasted_iota(jnp.int32, (n_sel, 1), 0)
        cur = t_row // SEL_BLOCK
        allowed = jb <= cur
        forced = (jb == 0) | (jb == cur) | (jb == cur - 1)
        score = jnp.where(forced & allowed, FORCE, jnp.where(allowed, imp, NEG))
        sc_ref[...] = score
        n_top = min(SEL_TOP, n_sel)

        def selection_bias(n_blk):
            sub = 8
            groups = [sc_ref[r:r + sub, :] for r in range(0, n_blk, sub)]
            ranks = [jnp.zeros((sub, ATT_Q), F32) for _ in groups]
            for k in range(n_blk):
                row = jnp.broadcast_to(sc_ref[k:k + 1, :], (sub, ATT_Q))
                for r, grp in enumerate(groups):
                    if r * sub > k:
                        ahead = row >= grp
                    elif r * sub + sub - 1 < k:
                        ahead = row > grp
                    else:
                        later = (r * sub + lax.broadcasted_iota(jnp.int32, (sub, 1), 0)) > k
                        ahead = (row > grp) | ((row == grp) & later)
                    ranks[r] = ranks[r] + jnp.where(ahead, 1.0, 0.0)
            chosen = (jnp.concatenate(ranks, axis=0) < n_top) & (jnp.concatenate(groups, axis=0) > 0.5 * NEG)
            bias = jnp.where(chosen, 0.0, NEG).astype(BF16)
            rest = bias_ref.shape[0] - n_blk
            return jnp.concatenate([bias, jnp.full((rest, ATT_Q), NEG, BF16)], axis=0) if rest else bias

        n_buckets = -(-n_sel // RANK_BUCKET)
        bucket = jnp.minimum((qb + 1) * (ATT_Q // SEL_BLOCK) - 1, n_sel - 1) // RANK_BUCKET
        for b in range(n_buckets):
            @pl.when(bucket == b)
            def _():
                bias_ref[...] = selection_bias(min((b + 1) * RANK_BUCKET, n_sel))
        qaug_ref[qb % 2] = jnp.concatenate([qt, jnp.concatenate([bias_ref[...]] * GQA_GROUP, axis=1)], axis=0)

        for br in range(2):
            m_ref[2 * slot + br] = jnp.full((1, nq), NEG, F32)
            acc_ref[2 * slot + br] = jnp.zeros((VT_ROWS, nq), F32)

    def epilogue(qb, slot):
        a_s, a_w = acc_ref[2 * slot], acc_ref[2 * slot + 1]
        o_s = a_s[:dh] / a_s[dh:dh + 1]
        o_w = a_w[:dh] / a_w[dh:dh + 1]
        gates = g_ref[0, 0, qb]

        def gate_row(br):
            return jnp.concatenate([gates[br * GQA_GROUP + g:br * GQA_GROUP + g + 1, :] for g in range(GQA_GROUP)],
                                   axis=1)
        o = gate_row(0) * oc_ref[slot] + gate_row(1) * o_s + gate_row(2) * o_w
        rows = pl.ds(pl.multiple_of(qb * ATT_Q, ATT_Q), ATT_Q)
        o_ref[0, rows, :] = jnp.concatenate(
            [o[:, g * ATT_Q:(g + 1) * ATT_Q].T for g in range(GQA_GROUP)], axis=1).astype(BF16)

    def stage_a(t, u):
        tile, pat, qs = jt_ref[t], jp_ref[t], jq_ref[t]
        k = kall_ref[pl.ds(pl.multiple_of(tile * KEY_TILE, KEY_TILE), KEY_TILE), :]
        s = _dot(k, qaug_ref[qs]) + jnp.concatenate([pat_ref[pat]] * GQA_GROUP, axis=1)
        s_ref[u] = s
        mt_ref[u] = jnp.max(s, axis=0, keepdims=True)

    def stage_b(t, u):
        a = ja_ref[t]
        m_old = m_ref[a]
        m_new = jnp.maximum(m_old, mt_ref[u])
        alpha = jnp.exp2(m_old - m_new)
        m_ref[a] = m_new
        al_ref[u] = alpha
        p_ref[u] = jnp.exp2(s_ref[u] - m_new).astype(BF16)

    def stage_c(t, u):
        tile, a = jt_ref[t], ja_ref[t]
        acc_ref[a] = al_ref[u] * acc_ref[a] + _dot(vt_ref[tile], p_ref[u])

    def step(g, carry):
        t0 = g * u_jobs
        for u in range(u_jobs):
            stage_c(t0 + u, u)
        for u in range(u_jobs):
            stage_b(t0 + u_jobs + u, u)
        for u in range(u_jobs):
            stage_a(t0 + 2 * u_jobs + u, u)
        return carry

    prologue(0)

    def block(i, carry):
        prologue(i + 1)
        epilogue(jnp.maximum(i - EPILOGUE_LAG, 0), (i - EPILOGUE_LAG) % ACC_RING)
        lax.fori_loop(lo_ref[i], lo_ref[i + 1], step, 0)
        return carry

    lax.fori_loop(0, n_qb - 1, block, 0)
    last = n_qb - 1
    epilogue(max(last - EPILOGUE_LAG, 0), (last - EPILOGUE_LAG) % ACC_RING)
    lax.fori_loop(lo_ref[last], lo_ref[last + 1], step, 0)
    for qb in range(max(n_qb - EPILOGUE_LAG, 0), n_qb):
        epilogue(qb, qb % ACC_RING)


def _attention(q_t, gates_t, kcv, vc_t, ks_aug, kw_aug, vs_t, vw_t):
    bsz, hkv, s, kdim = ks_aug.shape
    dh = vs_t.shape[2]
    n_cmp = kcv.shape[2]
    gd = GQA_GROUP * dh
    nq = GQA_GROUP * ATT_Q
    n_qb = s // ATT_Q
    patterns = _mask_patterns()
    cmp_vis = _compressed_visibility(s)
    tables = _job_tables(s)
    per_head = lambda shape: pl.BlockSpec((1, 1) + shape, lambda b, h, *_: (b, h) + (0,) * len(shape))
    n_acc = 2 * ACC_RING + 1
    return pl.pallas_call(
        _attn_kernel,
        grid_spec=pltpu.PrefetchScalarGridSpec(
            num_scalar_prefetch=len(tables),
            grid=(bsz, hkv),
            in_specs=[
                per_head((n_qb, gd, ATT_Q)),
                per_head((n_qb, 3 * GQA_GROUP, ATT_Q)),
                per_head((n_cmp, kcv.shape[3])),
                per_head((dh, n_cmp)),
                per_head((s, kdim)),
                per_head((s, kdim)),
                per_head((dh, s)),
                per_head((dh, s)),
                pl.BlockSpec(patterns.shape, lambda b, h, *_: (0, 0, 0)),
                pl.BlockSpec(cmp_vis.shape, lambda b, h, *_: (0, 0)),
            ],
            out_specs=pl.BlockSpec((1, s, gd), lambda b, h, *_: (b, 0, h)),
            scratch_shapes=[
                pltpu.VMEM((2 * s, kdim), BF16),
                pltpu.VMEM((2 * (s // KEY_TILE), VT_ROWS, KEY_TILE), BF16),
                pltpu.VMEM((s // SEL_BLOCK, ATT_Q), F32),
                pltpu.VMEM((kdim - dh, ATT_Q), BF16),
                pltpu.VMEM((2, kdim, nq), BF16),
                pltpu.VMEM((ATT_UNROLL, KEY_TILE, nq), F32),
                pltpu.VMEM((ATT_UNROLL, KEY_TILE, nq), BF16),
                pltpu.VMEM((ATT_UNROLL, 1, nq), F32),
                pltpu.VMEM((ATT_UNROLL, 1, nq), F32),
                pltpu.VMEM((n_acc, 1, nq), F32),
                pltpu.VMEM((n_acc, VT_ROWS, nq), F32),
                pltpu.VMEM((ACC_RING, dh, nq), F32),
            ],
        ),
        out_shape=jax.ShapeDtypeStruct((bsz, s, hkv * gd), BF16),
        compiler_params=_params("parallel", "arbitrary"),
        name="nsa_attention",
    )(*tables, q_t, gates_t, kcv, vc_t, ks_aug, kw_aug, vs_t, vw_t, patterns, cmp_vis)


def _oproj_kernel(o_ref, x_ref, mod_ref, w_ref, g_ref, b_ref, out_ref):
    y = _dot(o_ref[0], w_ref[...])
    out_ref[0] = _layer_norm(ALPHA * x_ref[0] + mod_ref[0, 2:3, :] * y, g_ref[...], b_ref[...])


def _out_projection(o, x, mod, w_o, ln_g, ln_b):
    bsz, s, d = x.shape
    tm = 512
    return pl.pallas_call(
        _oproj_kernel,
        grid=(bsz, s // tm),
        in_specs=[
            pl.BlockSpec((1, tm, o.shape[-1]), lambda b, j: (b, j, 0)),
            pl.BlockSpec((1, tm, d), lambda b, j: (b, j, 0)),
            pl.BlockSpec((1, 6, d), lambda b, j: (b, 0, 0)),
            _const_spec(w_o.shape),
            _const_spec((1, d)),
            _const_spec((1, d)),
        ],
        out_specs=pl.BlockSpec((1, tm, d), lambda b, j: (b, j, 0)),
        out_shape=jax.ShapeDtypeStruct((bsz, s, d), F32),
        compiler_params=_params("parallel", "parallel"),
        name="nsa_out_proj",
    )(o, x, mod, w_o, ln_g.reshape(1, d), ln_b.reshape(1, d))


def _moe_block(x, mod, router_wt, router_bias, w_gu, w_down, layer, ln_g, ln_b):
    bsz, s, d = x.shape
    t = bsz * s
    xf = x.reshape(t, d)
    wcol, route, counts = _router(xf, mod, router_wt, router_bias, s)

    cnt = counts[:, 0]
    padded = (cnt + EXPERT_TILE - 1) // EXPERT_TILE * EXPERT_TILE
    ends = jnp.cumsum(padded)
    offs = ends - padded
    n_tiles = (TOP_K * t) // EXPERT_TILE + N_EXPERTS
    tile_start = jnp.arange(n_tiles, dtype=jnp.int32) * EXPERT_TILE
    tile_expert = jnp.minimum(jnp.sum(tile_start[:, None] >= ends[None, :], axis=1), N_EXPERTS - 1).astype(jnp.int32)
    n_used = (ends[-1:] // EXPERT_TILE).astype(jnp.int32)
    experts, ranks = route[:TOP_K], route[TOP_K:2 * TOP_K]
    dst = ranks + sum(jnp.where(experts == e, offs[e], 0) for e in range(N_EXPERTS))
    idx = dst.reshape(TOP_K, t // TOKEN_TILE, TOKEN_TILE).transpose(1, 0, 2).reshape(t // TOKEN_TILE, 1, TOP_K * TOKEN_TILE)

    last_tile = jnp.where(cnt > 0, ends - EXPERT_TILE, -1)
    spare = n_used + jnp.arange(N_EXPERTS)
    spare = jnp.where(spare < n_tiles, spare * EXPERT_TILE, -1)
    xs = _dispatch(jnp.concatenate([last_tile, spare]).astype(jnp.int32), idx, xf, mod, n_tiles * EXPERT_TILE, s)
    ys = _experts(tile_expert, n_used, xs, w_gu, w_down, layer)
    out = _combine(idx, xf, mod, wcol, ys, ln_g, ln_b, s)
    return out.reshape(bsz, s, d)


def _nsa_layer(x, mod, w_kv, cmp_pe, cmp_w1, cmp_b1, cmp_w2, cmp_b2, w_qg, b_g, w_o, ln_g, ln_b):
    bsz, s, d = x.shape
    hkv, dh, grp = N_KV_HEADS, HEAD_DIM, GQA_GROUP
    hw = hkv * dh
    kvw = w_kv.reshape(d, 6, hw)
    kvc_w = jnp.stack([kvw[:, 0].reshape(d, hkv, dh), kvw[:, 1].reshape(d, hkv, dh)], axis=2).reshape(d, 2 * hw)
    w_nat = jnp.concatenate([kvc_w, kvw[:, 2], kvw[:, 4]], axis=1).astype(BF16)
    w_vt = jnp.concatenate([kvw[:, 3], kvw[:, 5]], axis=1).T.astype(BF16)
    nq = N_HEADS * dh
    wg = w_qg[:, nq:].reshape(d, hkv, grp, 3).transpose(0, 1, 3, 2).reshape(d, 3 * N_HEADS)
    w_qgt = jnp.concatenate([w_qg[:, :nq], wg], axis=1).T.astype(BF16)
    bg = b_g.reshape(hkv, grp, 3).transpose(0, 2, 1).reshape(3 * N_HEADS, 1)
    kvc, ks_aug, kw_aug, vs_t, vw_t, q_t, gates_t = _projections(x, mod, w_nat, w_vt, w_qgt, bg)
    cmp_kv, cmp_vt = _compress(kvc, cmp_pe, cmp_w1, cmp_b1, cmp_w2, cmp_b2)
    o = _attention(q_t, gates_t, cmp_kv, cmp_vt, ks_aug, kw_aug, vs_t, vw_t)
    return _out_projection(o, x, mod, w_o.astype(BF16), ln_g, ln_b)


def kernel(x, c, ada_w, ada_b, ln_g, ln_b, conv_w_in, conv_w, conv_b, conv_w_out, w_kv, cmp_pe, cmp_w1, cmp_b1, cmp_w2, cmp_b2, w_qg, b_g, w_o, router_w, router_bias, w_gu, w_down):
    bsz, s, d = x.shape
    mod = _modulation(c, ada_w, ada_b).reshape(DEPTH, bsz, 6, d)
    router_wt = router_w.T

    x = _conv_layer(x, mod[0], conv_w_in[0].astype(BF16), conv_w[0], conv_b[0], conv_w_out[0].astype(BF16),
                    ln_g[0, 0], ln_b[0, 0])
    x = _moe_block(x, mod[0], router_wt, router_bias, w_gu, w_down, 0, ln_g[0, 1], ln_b[0, 1])

    x = _nsa_layer(x, mod[1], w_kv, cmp_pe, cmp_w1, cmp_b1, cmp_w2, cmp_b2, w_qg[0], b_g[0], w_o[0],
                   ln_g[1, 0], ln_b[1, 0])
    x = _moe_block(x, mod[1], router_wt, router_bias, w_gu, w_down, 1, ln_g[1, 1], ln_b[1, 1])
    return x
```
